```python
import math
import jax, jax.numpy as jnp
from jax import lax
import numpy as np

D_MODEL = 2048
BATCH = 8
SEQ = 2048
DEPTH = 2

GRID_W = 64
CTX_LEN = 256
FN_WIDTH = D_MODEL // 4
FN_GROUPS = 4
HY_WIDTH = D_MODEL // 4
HG_WIDTH = D_MODEL // 2
HG_HEAD_DIM = 128
HG_HEADS = HG_WIDTH // HG_HEAD_DIM
HG_CHUNK = 64
HG_F_MIN = 1e-6
MIX_WIDTH = FN_WIDTH + HY_WIDTH + HG_WIDTH
HY_OFF = FN_WIDTH
HG_OFF = FN_WIDTH + 3 * HY_WIDTH
IN_WIDTH = FN_WIDTH + 3 * HY_WIDTH + 5 * HG_WIDTH
HY_EMB = 33
HY_BANDS = (HY_EMB - 1) // 2
HY_HIDDEN = 64
HY_SHORT = 3
HY_FAST_DECAY = 0.3
HY_SLOW_DECAY = 1.5
HY_TARGET = 1e-2
N_EXPERTS = 16
EC_CAPACITY = 2
EXPERT_FF = D_MODEL // 2
EPS = 1e-6

kernel_name = "hybrid_fnet_hyena_hgrn2_ec_moe_prefix_dit"

F32 = jnp.float32


def rmsnorm(x, g):
    x32 = x.astype(F32)
    y = x32 * lax.rsqrt(jnp.mean(x32 * x32, axis=-1, keepdims=True) + EPS)
    return (y * g.astype(F32)).astype(x.dtype)


def modulate(h, shift, scale):
    return h * (1 + scale) + shift


def fourier_mix(u, w_fn):
    B, L, _ = u.shape
    ug = u.astype(F32).reshape(B, L, FN_GROUPS, FN_WIDTH // FN_GROUPS)
    y = jnp.fft.fft2(ug, axes=(1, 3), norm="ortho").real.reshape(B, L, FN_WIDTH)
    return y.astype(u.dtype) @ w_fn


def short_conv(u, w, b):
    L = u.shape[1]
    half = HY_SHORT // 2
    up = jnp.pad(u, ((0, 0), (half, half), (0, 0)))
    return sum(up[:, j:j + L] * w[j] for j in range(HY_SHORT)) + b


def hyena_filters(L, w1, b1, w2, b2, w3, b3, w_out, freq):
    pos = jnp.arange(L, dtype=F32)
    t = (pos / max(L - 1, 1))[:, None]
    bands = jnp.linspace(1e-4, HY_BANDS - 1, HY_BANDS, dtype=F32)
    w = 2.0 * math.pi * pos[:, None] / L
    z = jnp.concatenate([t, jnp.cos(bands * w), -jnp.sin(bands * w)], axis=-1)
    hdn = jnp.sin(freq[0] * (z @ w1 + b1))
    hdn = jnp.sin(freq[1] * (hdn @ w2 + b2))
    hdn = jnp.sin(freq[2] * (hdn @ w3 + b3))
    h = (hdn @ w_out).astype(F32).reshape(L, 2, HY_WIDTH)
    max_decay = math.log(HY_TARGET) / HY_FAST_DECAY
    min_decay = math.log(HY_TARGET) / HY_SLOW_DECAY
    deltas = jnp.linspace(min_decay, max_decay, HY_WIDTH, dtype=F32)
    decay = jnp.exp(-t * jnp.abs(deltas))
    h = h * decay[:, None, :]
    return h[:, 0], h[:, 1]


def bidir_long_conv(u, h_f, h_b, d_bias):
    B, L, C = u.shape
    k = jnp.concatenate([h_f, jnp.zeros((1, C), F32), h_b[1:][::-1]], axis=0)
    U = jnp.fft.rfft(u.astype(F32), n=2 * L, axis=1)
    K = jnp.fft.rfft(k, axis=0)
    y = jnp.fft.irfft(U * K[None], n=2 * L, axis=1)[:, :L]
    return (y + u.astype(F32) * d_bias.astype(F32)).astype(u.dtype)


def hyena_mix(u3, conv_w, conv_b, w1, b1, w2, b2, w3, b3, w_out, freq, d_bias):
    L = u3.shape[1]
    uc = short_conv(u3, conv_w, conv_b)
    v = uc[..., :HY_WIDTH]
    x1 = uc[..., HY_WIDTH:2 * HY_WIDTH]
    x0 = uc[..., 2 * HY_WIDTH:]
    h_f, h_b = hyena_filters(L, w1, b1, w2, b2, w3, b3, w_out, freq)
    return x0 * bidir_long_conv(x1 * v, h_f, h_b, d_bias)


def hgrn2_chunk_scan(q, k, logf, v, s0):
    B, L, H, _ = q.shape
    DV = v.shape[-1]
    n = L // HG_CHUNK

    def chunks(a):
        return a.reshape(B, n, HG_CHUNK, H, a.shape[-1]).transpose(1, 0, 3, 2, 4)

    mask = jnp.tril(jnp.ones((HG_CHUNK, HG_CHUNK), dtype=bool))[:, :, None]

    def step(S, inp):
        qc, kc, gc, vc = inp
        b = jnp.cumsum(gc, axis=2)
        o_inter = jnp.einsum('bhtk,bhkv->bhtv', qc * jnp.exp(b), S)
        diff = b[:, :, :, None, :] - b[:, :, None, :, :]
        decay = jnp.where(mask, jnp.exp(jnp.where(mask, diff, 0.0)), 0.0)
        a = jnp.einsum('bhtk,bhsk,bhtsk->bhts', qc, kc, decay)
        o = o_inter + jnp.einsum('bhts,bhsv->bhtv', a, vc)
        b_last = b[:, :, -1]
        S = jnp.exp(b_last)[..., None] * S + jnp.einsum(
            'bhsk,bhsv->bhkv', kc * jnp.exp(b_last[:, :, None] - b), vc)
        return S, o

    S, o = lax.scan(step, s0, (chunks(q), chunks(k), chunks(logf), chunks(v)))
    o = o.transpose(1, 0, 3, 2, 4).reshape(B, L, H, DV)
    return o, S


def hgrn2_bidir(p_hg, lb_f, lb_b, s0_f, s0_b):
    B, L, _ = p_hg.shape
    W = HG_WIDTH
    zq, zf, zb, zi, g = (p_hg[..., j * W:(j + 1) * W] for j in range(5))

    def heads(a):
        return a.astype(F32).reshape(B, L, HG_HEADS, HG_HEAD_DIM)

    q = heads(jax.nn.silu(zq))
    v = heads(zi)

    def direction(z, lb, s0, flip):
        lb = lb.astype(F32)
        f = jnp.maximum(lb + (1.0 - lb) * jax.nn.sigmoid(z.astype(F32)), HG_F_MIN)
        logf = heads(jnp.log(f))
        k = heads(1.0 - f)
        qq, vv = q, v
        if flip:
            qq, k, logf, vv = qq[:, ::-1], k[:, ::-1], logf[:, ::-1], vv[:, ::-1]
        o, S = hgrn2_chunk_scan(qq, k, logf, vv, s0)
        if flip:
            o = o[:, ::-1]
        return o, S

    o_f, s_f = direction(zf, lb_f, s0_f, False)
    o_b, s_b = direction(zb, lb_b, s0_b, True)
    return o_f + o_b, g, s_f, s_b


def hgrn2_out(o, g, gain):
    B, L = o.shape[:2]
    y = o * lax.rsqrt(jnp.mean(o * o, axis=-1, keepdims=True) + EPS)
    y = y * gain.astype(F32).reshape(HG_HEADS, HG_HEAD_DIM)
    return (y.reshape(B, L, HG_WIDTH) * jax.nn.silu(g.astype(F32))).astype(g.dtype)


def ec_moe(h, w_router, w_gate, w_up, w_down):
    B, L, D = h.shape
    cap = EC_CAPACITY * L // N_EXPERTS
    aff = jax.nn.softmax((h @ w_router).astype(F32), axis=-1)
    gates, idx = lax.top_k(jnp.swapaxes(aff, 1, 2), cap)
    xs = jax.vmap(lambda hb, ib: hb[ib])(h, idx)
    hid = jax.nn.silu(jnp.einsum('becd,edf->becf', xs, w_gate)) * jnp.einsum('becd,edf->becf', xs, w_up)
    ys = jnp.einsum('becf,efd->becd', hid, w_down) * gates[..., None].astype(h.dtype)
    return jax.vmap(lambda ib, yb: jnp.zeros((L, D), yb.dtype).at[ib.reshape(-1)].add(yb.reshape(-1, D)))(idx, ys)


def setup_inputs(seed: int = 0) -> dict:
    key = jax.random.key(seed)
    ks = jax.random.split(key, 32)
    nrm = lambda k, shape, s: jax.random.normal(k, shape, F32) * s
    D = D_MODEL
    return {
        "x": nrm(ks[0], (BATCH, SEQ, D), 1.0),
        "c": nrm(ks[1], (BATCH, D), 1.0),
        "ctx": nrm(ks[2], (BATCH, CTX_LEN, D), 1.0),
        "c_ctx": nrm(ks[3], (D,), 1.0),
        "norm_mix_g": 1.0 + nrm(ks[4], (DEPTH, D), 0.05),
        "norm_ffn_g": 1.0 + nrm(ks[5], (DEPTH, D), 0.05),
        "final_norm_g": 1.0 + nrm(ks[6], (D,), 0.05),
        "w_mod": nrm(ks[7], (DEPTH, D, 6 * D), D ** -0.5),
        "b_mod": nrm(ks[8], (DEPTH, 6 * D), 0.02),
        "w_in": nrm(ks[9], (DEPTH, D, IN_WIDTH), D ** -0.5),
        "w_out": nrm(ks[10], (DEPTH, MIX_WIDTH, D), MIX_WIDTH ** -0.5),
        "w_fnet": nrm(ks[11], (DEPTH, FN_WIDTH, FN_WIDTH), FN_WIDTH ** -0.5),
        "hy_conv_w": nrm(ks[12], (DEPTH, HY_SHORT, 3 * HY_WIDTH), HY_SHORT ** -0.5),
        "hy_conv_b": nrm(ks[13], (DEPTH, 3 * HY_WIDTH), 0.02),
        "hy_w1": nrm(ks[14], (DEPTH, HY_EMB, HY_HIDDEN), HY_EMB ** -0.5),
        "hy_b1": nrm(ks[15], (DEPTH, HY_HIDDEN), 0.02),
        "hy_w2": nrm(ks[16], (DEPTH, HY_HIDDEN, HY_HIDDEN), HY_HIDDEN ** -0.5),
        "hy_b2": nrm(ks[17], (DEPTH, HY_HIDDEN), 0.02),
        "hy_w3": nrm(ks[18], (DEPTH, HY_HIDDEN, HY_HIDDEN), HY_HIDDEN ** -0.5),
        "hy_b3": nrm(ks[19], (DEPTH, HY_HIDDEN), 0.02),
        "hy_w_out": nrm(ks[20], (DEPTH, HY_HIDDEN, 2 * HY_WIDTH), 0.1 * HY_HIDDEN ** -0.5),
        "hy_freq": 1.0 + nrm(ks[21], (DEPTH, 3, HY_HIDDEN), 0.1),
        "hy_bias": nrm(ks[22], (DEPTH, HY_WIDTH), 0.5),
        "hg_lb": nrm(ks[23], (DEPTH, 2, HG_WIDTH), 0.5),
        "hg_norm_g": 1.0 + nrm(ks[24], (DEPTH, HG_WIDTH), 0.05),
        "w_router": nrm(ks[25], (DEPTH, D, N_EXPERTS), D ** -0.5),
        "w_gate": nrm(ks[26], (DEPTH, N_EXPERTS, D, EXPERT_FF), D ** -0.5),
        "w_up": nrm(ks[27], (DEPTH, N_EXPERTS, D, EXPERT_FF), D ** -0.5),
        "w_down": nrm(ks[28], (DEPTH, N_EXPERTS, EXPERT_FF, D), EXPERT_FF ** -0.5),
    }


def reference(x, c, ctx, c_ctx, norm_mix_g, norm_ffn_g, final_norm_g, w_mod, b_mod, w_in, w_out, w_fnet,
              hy_conv_w, hy_conv_b, hy_w1, hy_b1, hy_w2, hy_b2, hy_w3, hy_b3, hy_w_out, hy_freq, hy_bias,
              hg_lb, hg_norm_g, w_router, w_gate, w_up, w_down):
    B = x.shape[0]
    p = jax.nn.softmax(hg_lb.astype(F32), axis=0)
    lbs = jnp.cumsum(p, axis=0) - p[0:1]
    xc = ctx
    for l in range(DEPTH):
        last = l == DEPTH - 1
        mod = (jax.nn.silu(c) @ w_mod[l] + b_mod[l])[:, None, :]
        mod_c = jax.nn.silu(c_ctx) @ w_mod[l] + b_mod[l]
        sh1, sc1, g1, sh2, sc2, g2 = jnp.split(mod, 6, axis=-1)
        csh1, csc1, cg1, csh2, csc2, cg2 = jnp.split(mod_c, 6, axis=-1)
        hy_args = (hy_conv_w[l], hy_conv_b[l], hy_w1[l], hy_b1[l], hy_w2[l], hy_b2[l],
                   hy_w3[l], hy_b3[l], hy_w_out[l], hy_freq[l], hy_bias[l])

        def mixers_out(h, o_hg, g_hg):
            y_fn = fourier_mix(h[..., :FN_WIDTH], w_fnet[l])
            y_hy = hyena_mix(h[..., HY_OFF:HG_OFF], *hy_args)
            y_hg = hgrn2_out(o_hg, g_hg, hg_norm_g[l])
            return jnp.concatenate([y_fn, y_hy, y_hg], axis=-1) @ w_out[l]

        h = modulate(rmsnorm(x, norm_mix_g[l]), sh1, sc1) @ w_in[l]
        hc = modulate(rmsnorm(xc, norm_mix_g[l]), csh1, csc1) @ w_in[l]
        zero = jnp.zeros((B, HG_HEADS, HG_HEAD_DIM, HG_HEAD_DIM), F32)
        o_c, g_c, s_f, s_b = hgrn2_bidir(hc[..., HG_OFF:], lbs[l, 0], lbs[l, 1], zero, zero)
        o_x, g_x, _, _ = hgrn2_bidir(h[..., HG_OFF:], lbs[l, 0], lbs[l, 1], s_f, s_b)
        x = x + g1 * mixers_out(h, o_x, g_x)
        if not last:
            xc = xc + cg1 * mixers_out(hc, o_c, g_c)
        x = x + g2 * ec_moe(modulate(rmsnorm(x, norm_ffn_g[l]), sh2, sc2),
                            w_router[l], w_gate[l], w_up[l], w_down[l])
        if not last:
            xc = xc + cg2 * ec_moe(modulate(rmsnorm(xc, norm_ffn_g[l]), csh2, csc2),
                                   w_router[l], w_gate[l], w_up[l], w_down[l])
    return rmsnorm(x, final_norm_g)
```

```python
import functools
import math

import jax
import jax.numpy as jnp
from jax import lax
from jax.experimental import pallas as pl
from jax.experimental.pallas import tpu as pltpu

F32 = jnp.float32
BF16 = jnp.bfloat16

D_MODEL = 2048
FN_WIDTH = 512
FN_GROUP = 128
HY_WIDTH = 512
HG_WIDTH = 1024
HG_HEAD = 128
HG_HEADS = HG_WIDTH // HG_HEAD
HG_F_MIN = 1e-6
IN_WIDTH = FN_WIDTH + 3 * HY_WIDTH + 5 * HG_WIDTH
HY_BANDS = 16
HY_PAD = 128
N_EXPERTS = 16
EC_CAPACITY = 2
EXPERT_FF = 1024
EPS = 1e-6

HG_CHUNK = 128
HG_BASE = 8
VMEM_LIMIT = 56 * 1024 * 1024

_COL_Q, _COL_FF, _COL_FB, _COL_I, _COL_G = 2, 3, 4, 5, 6


def _params(*sem):
    return pltpu.CompilerParams(dimension_semantics=sem, vmem_limit_bytes=VMEM_LIMIT)


def _dot(a, b):
    return jnp.dot(a, b, preferred_element_type=F32)


def _dot_nt(a, b):
    return lax.dot_general(a, b, (((1,), (1,)), ((), ())), preferred_element_type=F32)


def _dot_tn(a, b):
    return lax.dot_general(a, b, (((0,), (0,)), ((), ())), preferred_element_type=F32)


def _split2(x):
    hi = x.astype(BF16)
    lo = (x - hi.astype(F32)).astype(BF16)
    return hi, lo


def _split3(x):
    hi = x.astype(BF16)
    r = x - hi.astype(F32)
    mid = r.astype(BF16)
    lo = (r - mid.astype(F32)).astype(BF16)
    return hi, mid, lo


def _dot3(a, b, dot=_dot):
    ah, al = _split2(a)
    bh, bl = _split2(b)
    return dot(ah, bh) + dot(ah, bl) + dot(al, bh)


def _silu(x):
    return x * jax.nn.sigmoid(x)


def _norm_mod(x, g, sh, sc):
    ms = jnp.mean(x * x, axis=-1, keepdims=True)
    return (x * lax.rsqrt(ms + EPS) * g) * (1.0 + sc) + sh


def _mod_kernel(a_ref, w_ref, b_ref, o_ref):
    a = _silu(a_ref[...]).astype(BF16)
    o_ref[...] = _dot(a, w_ref[...].astype(BF16)) + b_ref[...]


def _modulation(cc, w_mod, b_mod3, l):
    rows, d = cc.shape
    n = w_mod.shape[-1]
    tn = 1024
    return pl.pallas_call(
        _mod_kernel,
        grid=(n // tn,),
        in_specs=[
            pl.BlockSpec((rows, d), lambda j: (0, 0)),
            pl.BlockSpec((None, d, tn), lambda j: (l, 0, j)),
            pl.BlockSpec((None, 1, tn), lambda j: (l, 0, j)),
        ],
        out_specs=pl.BlockSpec((rows, tn), lambda j: (0, j)),
        out_shape=jax.ShapeDtypeStruct((rows, n), F32),
        compiler_params=_params("arbitrary"),
        name="modulation",
    )(cc, w_mod, b_mod3)


def _in_kernel(x_ref, g_ref, sh_ref, sc_ref, w_ref, o_ref, xm_ref):
    @pl.when(pl.program_id(2) == 0)
    def _():
        xm_ref[...] = _norm_mod(x_ref[...], g_ref[...], sh_ref[...], sc_ref[...]).astype(BF16)

    o_ref[...] = _dot(xm_ref[...], w_ref[...]).astype(o_ref.dtype)


def _in_proj(x, mod4, mrow, gamma3, w_in_bf, l):
    bsz, seq, d = x.shape
    n = w_in_bf.shape[-1]
    tm = min(seq, 1024)
    tn = 1024
    return pl.pallas_call(
        _in_kernel,
        grid=(bsz, seq // tm, n // tn),
        in_specs=[
            pl.BlockSpec((None, tm, d), lambda b, i, j: (b, i, 0)),
            pl.BlockSpec((None, 1, d), lambda b, i, j: (l, 0, 0)),
            pl.BlockSpec((None, None, 1, d), lambda b, i, j: (mrow(b), 0, 0, 0)),
            pl.BlockSpec((None, None, 1, d), lambda b, i, j: (mrow(b), 1, 0, 0)),
            pl.BlockSpec((None, d, tn), lambda b, i, j: (l, 0, j)),
        ],
        out_specs=pl.BlockSpec((None, tm, tn), lambda b, i, j: (b, i, j)),
        out_shape=jax.ShapeDtypeStruct((bsz, seq, n), BF16),
        scratch_shapes=[pltpu.VMEM((tm, d), BF16)],
        compiler_params=_params("arbitrary", "arbitrary", "arbitrary"),
        name="in_proj",
    )(x, gamma3, mod4, mod4, w_in_bf)


def _hg_kernel(rev, fuse_out, *refs):
    if fuse_out:
        (zq_ref, zf_ref, zi_ref, g_ref, of_ref, lb_ref, gain_ref, s0_ref, y_ref, st_ref, s_ref) = refs
    else:
        (zq_ref, zf_ref, zi_ref, lb_ref, s0_ref, y_ref, st_ref, s_ref) = refs
    C = HG_CHUNK
    c = pl.program_id(1)

    @pl.when(c == 0)
    def _():
        s_ref[...] = s0_ref[...]

    row = lax.broadcasted_iota(jnp.int32, (C, C), 0)
    col = lax.broadcasted_iota(jnp.int32, (C, C), 1)
    rowi = lax.broadcasted_iota(jnp.int32, (C, HG_HEAD), 0)
    before = (col >= row) if rev else (col <= row)
    tri = jnp.where(before, 1.0, 0.0).astype(BF16)
    sizes = []
    s = C
    while s > HG_BASE:
        sizes.append(s)
        s //= 2
    same_block = {s: jnp.where((row // s) == (col // s), 1.0, 0.0) for s in sizes[1:]}
    base_mask = jnp.where((row // HG_BASE) == (col // HG_BASE), tri.astype(F32), 0.0)
    second_half = {s: jnp.where(((rowi % s) >= s // 2) != rev, 1.0, 0.0) for s in sizes}

    def ref_rows(b, s, r):
        nb = C // s
        b3 = b.reshape(nb, s, HG_HEAD)
        return jnp.broadcast_to(b3[:, r:r + 1, :], (nb, s, HG_HEAD)).reshape(C, HG_HEAD)

    def head(h, carry):
        hs = pl.ds(pl.multiple_of(h * HG_HEAD, HG_HEAD), HG_HEAD)
        zq = zq_ref[:, hs].astype(F32)
        zf = zf_ref[:, hs].astype(F32)
        v = zi_ref[:, hs]
        lb = lb_ref[:, hs]
        q = _silu(zq)
        f = jnp.maximum(lb + (1.0 - lb) * jax.nn.sigmoid(zf), HG_F_MIN)
        logf = jnp.log(f)
        kk = 1.0 - f
        hi, mid, lo = _split3(logf)
        b = _dot(tri, hi) + _dot(tri, mid) + _dot(tri, lo)
        btot = b[0:1, :] if rev else b[C - 1:C, :]

        a = None
        for s in sizes:
            hh = s // 2
            e = ref_rows(b, s, hh if rev else hh - 1)
            w = jnp.exp(-jnp.abs(b - e))
            qh = (q * w * second_half[s]).astype(BF16)
            kh = (kk * w * (1.0 - second_half[s])).astype(BF16)
            m = _dot_nt(qh, kh)
            if s < C:
                m = m * same_block[s]
            a = m if a is None else a + m
        e = ref_rows(b, HG_BASE, HG_BASE // 2 if rev else HG_BASE // 2 - 1)
        d = b - e
        m = _dot_nt((q * jnp.exp(d)).astype(BF16), (kk * jnp.exp(-d)).astype(BF16))
        a = a + jnp.where(base_mask > 0.5, m, 0.0)

        st = s_ref[h]
        o = _dot_nt((q * jnp.exp(b)).astype(BF16), st.astype(BF16)) + _dot(a.astype(BF16), v)
        kd = (kk * jnp.exp(btot - b)).astype(BF16)
        s_ref[h] = st * jnp.exp(btot) + _dot_tn(v, kd)

        if fuse_out:
            o = o + of_ref[:, hs]
            ms = jnp.mean(o * o, axis=-1, keepdims=True)
            g = g_ref[:, hs].astype(F32)
            y_ref[:, hs] = (o * lax.rsqrt(ms + EPS) * gain_ref[:, hs] * _silu(g)).astype(y_ref.dtype)
        else:
            y_ref[:, hs] = o
        return carry

    lax.fori_loop(0, HG_HEADS, head, 0)

    @pl.when(c == pl.num_programs(1) - 1)
    def _():
        st_ref[...] = s_ref[...]


def _hgrn(h, lbs4, gain3, s0, l, rev, o_fwd=None):
    bsz, seq, _ = h.shape
    C = HG_CHUNK
    nc = seq // C
    W = HG_WIDTH
    fuse_out = o_fwd is not None
    cidx = (lambda c: nc - 1 - c) if rev else (lambda c: c)

    def hcol(k):
        return pl.BlockSpec((None, C, W), lambda b, c: (b, cidx(c), k))

    lb_spec = pl.BlockSpec((None, None, 1, W), lambda b, c: (l, 1 if rev else 0, 0, 0))
    s_spec = pl.BlockSpec((None, HG_HEADS, HG_HEAD, HG_HEAD), lambda b, c: (b, 0, 0, 0))
    o_spec = pl.BlockSpec((None, C, W), lambda b, c: (b, cidx(c), 0))
    if fuse_out:
        in_specs = [hcol(_COL_Q), hcol(_COL_FB if rev else _COL_FF), hcol(_COL_I), hcol(_COL_G), o_spec,
                    lb_spec, pl.BlockSpec((None, 1, W), lambda b, c: (l, 0, 0)), s_spec]
        args = (h, h, h, h, o_fwd, lbs4, gain3, s0)
        out_dtype = BF16
    else:
        in_specs = [hcol(_COL_Q), hcol(_COL_FB if rev else _COL_FF), hcol(_COL_I), lb_spec, s_spec]
        args = (h, h, h, lbs4, s0)
        out_dtype = F32
    return pl.pallas_call(
        functools.partial(_hg_kernel, rev, fuse_out),
        grid=(bsz, nc),
        in_specs=in_specs,
        out_specs=[o_spec, s_spec],
        out_shape=[jax.ShapeDtypeStruct((bsz, seq, W), out_dtype),
                   jax.ShapeDtypeStruct((bsz, HG_HEADS, HG_HEAD, HG_HEAD), F32)],
        scratch_shapes=[pltpu.VMEM((HG_HEADS, HG_HEAD, HG_HEAD), F32)],
        compiler_params=_params("arbitrary", "arbitrary"),
        name="hgrn_bwd" if rev else "hgrn_fwd",
    )(*args)


def _fn_prep_kernel(cs_ref, w_ref, o_ref):
    w = w_ref[...]
    o_ref[:, :FN_WIDTH] = _dot3(cs_ref[0], w).astype(BF16)
    o_ref[:, FN_WIDTH:] = _dot3(cs_ref[1], w).astype(BF16)


def _fn_prep(chan_dft, w_fnet, l):
    return pl.pallas_call(
        _fn_prep_kernel,
        grid=(1,),
        in_specs=[pl.BlockSpec((2, FN_WIDTH, FN_WIDTH), lambda i: (0, 0, 0)),
                  pl.BlockSpec((None, FN_WIDTH, FN_WIDTH), lambda i: (l, 0, 0))],
        out_specs=pl.BlockSpec((FN_WIDTH, 2 * FN_WIDTH), lambda i: (0, 0)),
        out_shape=jax.ShapeDtypeStruct((FN_WIDTH, 2 * FN_WIDTH), BF16),
        compiler_params=_params("arbitrary"),
        name="fnet_prep",
    )(chan_dft, w_fnet)


def _fn_kernel(u_ref, wc_ref, t_ref, o_ref, p_ref):
    seq = u_ref.shape[0]
    rc = min(seq, 512)

    @pl.when(pl.program_id(1) == 0)
    def _():
        def rows(i, carry):
            r0 = pl.multiple_of(i * rc, rc)
            p = _dot(u_ref[pl.ds(r0, rc), :], wc_ref[...])
            p_ref[pl.ds(r0, rc), :] = p[:, :FN_WIDTH].astype(BF16)
            p_ref[pl.ds(pl.multiple_of(seq + r0, rc), rc), :] = p[:, FN_WIDTH:].astype(BF16)
            return carry

        lax.fori_loop(0, seq // rc, rows, 0)

    o_ref[...] = _dot(t_ref[...], p_ref[...]).astype(o_ref.dtype)


def _fourier(h, wc, table):
    bsz, seq, _ = h.shape
    tt = min(seq, 512)
    return pl.pallas_call(
        _fn_kernel,
        grid=(bsz, seq // tt),
        in_specs=[pl.BlockSpec((None, seq, FN_WIDTH), lambda b, t: (b, 0, 0)),
                  pl.BlockSpec((FN_WIDTH, 2 * FN_WIDTH), lambda b, t: (0, 0)),
                  pl.BlockSpec((tt, 2 * seq), lambda b, t: (t, 0))],
        out_specs=pl.BlockSpec((None, tt, FN_WIDTH), lambda b, t: (b, t, 0)),
        out_shape=jax.ShapeDtypeStruct((bsz, seq, FN_WIDTH), BF16),
        scratch_shapes=[pltpu.VMEM((2 * seq, FN_WIDTH), BF16)],
        compiler_params=_params("arbitrary", "arbitrary"),
        name="fourier",
    )(h, wc, table)


def _hy_filter_kernel(seq, wc_ref, ws_ref, w1_ref, b1_ref, w2_ref, b2_ref, w3_ref, b3_ref, wo_ref, fr_ref,
                      dl_ref, kc_ref, ks_ref, h_ref):
    i = pl.program_id(0)
    tf = wc_ref.shape[0]
    nfft = 2 * seq

    @pl.when(i == 0)
    def _():
        pos = lax.broadcasted_iota(jnp.int32, (seq, HY_PAD), 0).astype(F32)
        lane = lax.broadcasted_iota(jnp.int32, (seq, HY_PAD), 1)
        t = pos / float(max(seq - 1, 1))
        w = (2.0 * math.pi) * pos / float(seq)
        band_id = jnp.where(lane <= HY_BANDS, lane - 1, lane - 1 - HY_BANDS).astype(F32)
        band = 1e-4 + band_id * ((HY_BANDS - 1 - 1e-4) / (HY_BANDS - 1))
        arg = band * w
        z = jnp.where(lane == 0, t,
                      jnp.where(lane <= HY_BANDS, jnp.cos(arg),
                                jnp.where(lane <= 2 * HY_BANDS, -jnp.sin(arg), 0.0)))
        fr = fr_ref[...]
        hdn = jnp.sin(fr[0:1] * (_dot3(z, w1_ref[...]) + b1_ref[...]))
        hdn = jnp.sin(fr[1:2] * (_dot3(hdn, w2_ref[...]) + b2_ref[...]))
        hdn = jnp.sin(fr[2:3] * (_dot3(hdn, w3_ref[...]) + b3_ref[...]))
        hf = _dot3(hdn, wo_ref[...])
        decay = jnp.exp(-t[:, 0:1] * jnp.abs(dl_ref[...]))
        first = lax.broadcasted_iota(jnp.int32, (seq, HY_WIDTH), 0) == 0
        h_ref[:, :HY_WIDTH] = (hf[:, :HY_WIDTH] * decay).astype(BF16)
        h_ref[:, HY_WIDTH:] = jnp.where(first, 0.0, hf[:, HY_WIDTH:] * decay).astype(BF16)

    gc = _dot(wc_ref[...], h_ref[...])
    gs = _dot(ws_ref[...], h_ref[...])
    first = (lax.broadcasted_iota(jnp.int32, (tf, HY_WIDTH), 0) + i * tf) == 0
    scale = jnp.where(first, 1.0 / nfft, 2.0 / nfft)
    kc_ref[...] = (gc[:, :HY_WIDTH] + gc[:, HY_WIDTH:]) * scale
    ks_ref[...] = jnp.where(first, gs[:, :HY_WIDTH] + gs[:, HY_WIDTH:], gs[:, :HY_WIDTH] - gs[:, HY_WIDTH:]) * scale


def _hy_filter(seq, dft, mlp, deltas, l):
    w1p, b1p, w2p, b2p, w3p, b3p, wop, frp = mlp
    tf = min(seq, 512)
    nf = seq // tf

    def full(a):
        shp = a.shape[1:]
        return pl.BlockSpec((None,) + shp, lambda i: (l,) + (0,) * len(shp))

    return pl.pallas_call(
        functools.partial(_hy_filter_kernel, seq),
        grid=(nf,),
        in_specs=[pl.BlockSpec((tf, seq), lambda i: (i, 0)),
                  pl.BlockSpec((tf, seq), lambda i: (i + nf, 0)),
                  full(w1p), full(b1p), full(w2p), full(b2p), full(w3p), full(b3p), full(wop), full(frp),
                  pl.BlockSpec((1, HY_WIDTH), lambda i: (0, 0))],
        out_specs=[pl.BlockSpec((tf, HY_WIDTH), lambda i: (i, 0)),
                   pl.BlockSpec((tf, HY_WIDTH), lambda i: (i, 0))],
        out_shape=[jax.ShapeDtypeStruct((seq, HY_WIDTH), F32), jax.ShapeDtypeStruct((seq, HY_WIDTH), F32)],
        scratch_shapes=[pltpu.VMEM((seq, 2 * HY_WIDTH), BF16)],
        compiler_params=_params("arbitrary"),
        name="hyena_filter",
    )(dft, dft, w1p, b1p, w2p, b2p, w3p, b3p, wop, frp, deltas)


def _hy_pre_kernel(uv_ref, u1_ref, u0_ref, cw_ref, cb_ref, z_ref, x0_ref):
    seq = uv_ref.shape[0]
    rowi = lax.broadcasted_iota(jnp.int32, (seq, 128), 0)
    j = pl.program_id(1)

    def conv(u_ref, part):
        u = u_ref[...].astype(F32)
        prev = jnp.where(rowi == 0, 0.0, pltpu.roll(u, 1, 0))
        nxt = jnp.where(rowi == seq - 1, 0.0, pltpu.roll(u, seq - 1, 0))
        cs = pl.ds(pl.multiple_of(part * HY_WIDTH + j * 128, 128), 128)
        return prev * cw_ref[0:1, cs] + u * cw_ref[1:2, cs] + nxt * cw_ref[2:3, cs] + cb_ref[:, cs]

    z_ref[...] = (conv(u1_ref, 1) * conv(uv_ref, 0)).astype(BF16)
    x0_ref[...] = conv(u0_ref, 2).astype(BF16)


def _hy_pre(h, conv_w, conv_b3, l):
    bsz, seq, _ = h.shape
    lanes = 128
    nj = HY_WIDTH // lanes
    off = FN_WIDTH // lanes

    def part(p):
        return pl.BlockSpec((None, seq, lanes), lambda b, j: (b, 0, off + p * nj + j))

    o_spec = pl.BlockSpec((None, seq, lanes), lambda b, j: (b, 0, j))
    return pl.pallas_call(
        _hy_pre_kernel,
        grid=(bsz, nj),
        in_specs=[part(0), part(1), part(2),
                  pl.BlockSpec((None, 3, 3 * HY_WIDTH), lambda b, j: (l, 0, 0)),
                  pl.BlockSpec((None, 1, 3 * HY_WIDTH), lambda b, j: (l, 0, 0))],
        out_specs=[o_spec, o_spec],
        out_shape=[jax.ShapeDtypeStruct((bsz, seq, HY_WIDTH), BF16)] * 2,
        compiler_params=_params("arbitrary", "arbitrary"),
        name="hyena_pre",
    )(h, h, h, conv_w, conv_b3)


def _hy_fwd_kernel(wc_ref, ws_ref, z_ref, kc_ref, ks_ref, yc_ref, ys_ref):
    tf = wc_ref.shape[0]
    z = z_ref[...]
    uc = _dot(wc_ref[...], z)
    us = _dot(ws_ref[...], z)
    kc = kc_ref[...]
    ks = ks_ref[...]
    first = (lax.broadcasted_iota(jnp.int32, (tf, HY_WIDTH), 0) + pl.program_id(0) * tf) == 0
    ss = us * ks
    yc_ref[...] = (uc * kc - jnp.where(first, 0.0, ss)).astype(BF16)
    ys_ref[...] = jnp.where(first, ss, uc * ks + us * kc).astype(BF16)


def _hy_fwd(z, dft, kc, ks):
    bsz, seq, _ = z.shape
    tf = min(seq, 512)
    nf = seq // tf
    k_spec = pl.BlockSpec((tf, HY_WIDTH), lambda i, b: (i, 0))
    y_spec = pl.BlockSpec((None, tf, HY_WIDTH), lambda i, b: (b, i, 0))
    return pl.pallas_call(
        _hy_fwd_kernel,
        grid=(nf, bsz),
        in_specs=[pl.BlockSpec((tf, seq), lambda i, b: (i, 0)),
                  pl.BlockSpec((tf, seq), lambda i, b: (i + nf, 0)),
                  pl.BlockSpec((None, seq, HY_WIDTH), lambda i, b: (b, 0, 0)),
                  k_spec, k_spec],
        out_specs=[y_spec, y_spec],
        out_shape=[jax.ShapeDtypeStruct((bsz, seq, HY_WIDTH), BF16)] * 2,
        compiler_params=_params("arbitrary", "arbitrary"),
        name="hyena_dft",
    )(dft, dft, z, kc, ks)


def _hy_inv_kernel(tc_ref, ts_ref, yc_ref, ys_ref, z_ref, x0_ref, db_ref, o_ref):
    y = _dot(tc_ref[...], yc_ref[...]) + _dot(ts_ref[...], ys_ref[...])
    z = z_ref[...].astype(F32)
    o_ref[...] = (x0_ref[...].astype(F32) * (y + z * db_ref[...])).astype(BF16)


def _hy_inv(yc, ys, dft_t, z, x0, hy_bias3, l):
    bsz, seq, _ = z.shape
    tt = min(seq, 512)
    y_spec = pl.BlockSpec((None, seq, HY_WIDTH), lambda t, b: (b, 0, 0))
    r_spec = pl.BlockSpec((None, tt, HY_WIDTH), lambda t, b: (b, t, 0))
    return pl.pallas_call(
        _hy_inv_kernel,
        grid=(seq // tt, bsz),
        in_specs=[pl.BlockSpec((tt, seq), lambda t, b: (t, 0)),
                  pl.BlockSpec((tt, seq), lambda t, b: (t, 1)),
                  y_spec, y_spec, r_spec, r_spec,
                  pl.BlockSpec((None, 1, HY_WIDTH), lambda t, b: (l, 0, 0))],
        out_specs=r_spec,
        out_shape=jax.ShapeDtypeStruct((bsz, seq, HY_WIDTH), BF16),
        compiler_params=_params("arbitrary", "arbitrary"),
        name="hyena_idft",
    )(dft_t, dft_t, yc, ys, z, x0, hy_bias3)


def _out_kernel(yf_ref, yh_ref, yg_ref, wf_ref, wh_ref, wg_ref, x_ref, g_ref, o_ref):
    mix = _dot(yf_ref[...], wf_ref[...]) + _dot(yh_ref[...], wh_ref[...]) + _dot(yg_ref[...], wg_ref[...])
    o_ref[...] = x_ref[...] + g_ref[...] * mix


def _out_proj(x, y_fn, y_hy, y_hg, w_out_bf, mod4, mrow, l):
    bsz, seq, d = x.shape
    tm = min(seq, 1024)
    tn = 1024
    half = FN_WIDTH
    return pl.pallas_call(
        _out_kernel,
        grid=(bsz, seq // tm, d // tn),
        in_specs=[pl.BlockSpec((None, tm, half), lambda b, i, j: (b, i, 0)),
                  pl.BlockSpec((None, tm, half), lambda b, i, j: (b, i, 0)),
                  pl.BlockSpec((None, tm, HG_WIDTH), lambda b, i, j: (b, i, 0)),
                  pl.BlockSpec((None, half, tn), lambda b, i, j: (l, 0, j)),
                  pl.BlockSpec((None, half, tn), lambda b, i, j: (l, 1, j)),
                  pl.BlockSpec((None, HG_WIDTH, tn), lambda b, i, j: (l, 1, j)),
                  pl.BlockSpec((None, tm, tn), lambda b, i, j: (b, i, j)),
                  pl.BlockSpec((None, None, 1, tn), lambda b, i, j: (mrow(b), 2, 0, j))],
        out_specs=pl.BlockSpec((None, tm, tn), lambda b, i, j: (b, i, j)),
        out_shape=jax.ShapeDtypeStruct((bsz, seq, d), F32),
        compiler_params=_params("arbitrary", "arbitrary", "arbitrary"),
        name="out_proj",
    )(y_fn, y_hy, y_hg, w_out_bf, w_out_bf, w_out_bf, x, mod4)


def _route_kernel(x_ref, g_ref, sh_ref, sc_ref, wr_ref, xm_ref, aff_ref):
    xm = _norm_mod(x_ref[...], g_ref[...], sh_ref[...], sc_ref[...])
    xm_ref[...] = xm.astype(BF16)
    logits = _dot3(wr_ref[...], xm, dot=_dot_nt)
    mx = jnp.max(logits, axis=0, keepdims=True)
    ex = jnp.exp(logits - mx)
    aff_ref[...] = ex / jnp.sum(ex, axis=0, keepdims=True)


def _route(x, mod4, mrow, gamma3, w_router_t, l):
    bsz, seq, d = x.shape
    tm = min(seq, 512)
    return pl.pallas_call(
        _route_kernel,
        grid=(bsz, seq // tm),
        in_specs=[pl.BlockSpec((None, tm, d), lambda b, i: (b, i, 0)),
                  pl.BlockSpec((None, 1, d), lambda b, i: (l, 0, 0)),
                  pl.BlockSpec((None, None, 1, d), lambda b, i: (mrow(b), 3, 0, 0)),
                  pl.BlockSpec((None, None, 1, d), lambda b, i: (mrow(b), 4, 0, 0)),
                  pl.BlockSpec((None, N_EXPERTS, d), lambda b, i: (l, 0, 0))],
        out_specs=[pl.BlockSpec((None, tm, d), lambda b, i: (b, i, 0)),
                   pl.BlockSpec((None, N_EXPERTS, tm), lambda b, i: (b, 0, i))],
        out_shape=[jax.ShapeDtypeStruct((bsz, seq, d), BF16),
                   jax.ShapeDtypeStruct((bsz, N_EXPERTS, seq), F32)],
        compiler_params=_params("arbitrary", "arbitrary"),
        name="moe_route",
    )(x, gamma3, mod4, mod4, w_router_t)


def _topk_kernel(cap, aff_ref, tri_ref, pos_ref):
    a = aff_ref[...]

    def count(mask):
        return jnp.sum(jnp.where(mask, 1.0, 0.0), axis=1, keepdims=True)

    def as_float(bits):
        return pltpu.bitcast(jnp.broadcast_to(bits, a.shape), F32)

    def step(i, thr_bits):
        cand = thr_bits | jnp.left_shift(jnp.int32(1), 30 - i)
        return jnp.where(count(a >= as_float(cand)) >= cap, cand, thr_bits)

    thr = as_float(lax.fori_loop(0, 31, step, jnp.zeros((a.shape[0], 1), jnp.int32)))
    above = a > thr
    tie = a == thr
    room = cap - count(above)
    tie_rank = _dot(jnp.where(tie, 1.0, 0.0).astype(BF16), tri_ref[...])
    sel = jnp.where(above, 1.0, jnp.where(tie, jnp.where(tie_rank <= room, 1.0, 0.0), 0.0))
    slot = _dot(sel.astype(BF16), tri_ref[...]) - 1.0
    pos_ref[...] = jnp.where(sel > 0.5, slot, -1.0).astype(jnp.int32)


def _topk(aff, tri_incl, cap):
    bsz, ne, seq = aff.shape
    return pl.pallas_call(
        functools.partial(_topk_kernel, cap),
        grid=(bsz,),
        in_specs=[pl.BlockSpec((None, ne, seq), lambda b: (b, 0, 0)),
                  pl.BlockSpec((seq, seq), lambda b: (0, 0))],
        out_specs=pl.BlockSpec((None, ne, seq), lambda b: (b, 0, 0)),
        out_shape=jax.ShapeDtypeStruct((bsz, ne, seq), jnp.int32),
        compiler_params=_params("arbitrary"),
        name="moe_topk",
    )(aff, tri_incl)


def _gather_kernel(cap, xm_ref, pos_ref, o_ref):
    seq = xm_ref.shape[0]
    pos = pos_ref[pl.ds(pl.program_id(1), 1), :]
    slot = lax.broadcasted_iota(jnp.int32, (cap, seq), 0)
    onehot = jnp.where(slot == pos, 1.0, 0.0).astype(BF16)
    o_ref[...] = _dot(onehot, xm_ref[...]).astype(BF16)


def _gather(xm, pos, cap):
    bsz, seq, d = xm.shape
    return pl.pallas_call(
        functools.partial(_gather_kernel, cap),
        grid=(bsz, N_EXPERTS),
        in_specs=[pl.BlockSpec((None, seq, d), lambda b, e: (b, 0, 0)),
                  pl.BlockSpec((None, N_EXPERTS, seq), lambda b, e: (b, 0, 0))],
        out_specs=pl.BlockSpec((None, None, cap, d), lambda b, e: (e, b, 0, 0)),
        out_shape=jax.ShapeDtypeStruct((N_EXPERTS, bsz, cap, d), BF16),
        compiler_params=_params("arbitrary", "arbitrary"),
        name="moe_gather",
    )(xm, pos)


def _ffn_kernel(xs_ref, wg_ref, wu_ref, wd_ref, o_ref, acc_ref):
    j = pl.program_id(2)
    xs = xs_ref[...]
    hid = _silu(_dot(xs, wg_ref[...].astype(BF16))) * _dot(xs, wu_ref[...].astype(BF16))
    part = _dot(hid.astype(BF16), wd_ref[...].astype(BF16))

    @pl.when(j == 0)
    def _():
        acc_ref[...] = part

    @pl.when(j > 0)
    def _():
        acc_ref[...] += part

    @pl.when(j == pl.num_programs(2) - 1)
    def _():
        o_ref[...] = acc_ref[...].astype(BF16)


def _ffn(xs, w_gate, w_up, w_down, l):
    ne, rows, d = xs.shape
    ff = w_gate.shape[-1]
    tm = min(rows, 1024)
    tj = 256
    return pl.pallas_call(
        _ffn_kernel,
        grid=(ne, rows // tm, ff // tj),
        in_specs=[pl.BlockSpec((None, tm, d), lambda e, m, j: (e, m, 0)),
                  pl.BlockSpec((None, None, d, tj), lambda e, m, j: (l, e, 0, j)),
                  pl.BlockSpec((None, None, d, tj), lambda e, m, j: (l, e, 0, j)),
                  pl.BlockSpec((None, None, tj, d), lambda e, m, j: (l, e, j, 0))],
        out_specs=pl.BlockSpec((None, tm, d), lambda e, m, j: (e, m, 0)),
        out_shape=jax.ShapeDtypeStruct((ne, rows, d), BF16),
        scratch_shapes=[pltpu.VMEM((tm, d), F32)],
        compiler_params=_params("arbitrary", "arbitrary", "arbitrary"),
        name="moe_ffn",
    )(xs, w_gate, w_up, w_down)


def _combine_kernel(cap, ys_ref, pos_ref, aff_ref, x_ref, g_ref, o_ref, acc_ref):
    e = pl.program_id(2)
    seq = x_ref.shape[0]
    pos = pos_ref[pl.ds(e, 1), :]
    gate = aff_ref[pl.ds(e, 1), :]
    slot = lax.broadcasted_iota(jnp.int32, (cap, seq), 0)
    weighted = jnp.where(slot == pos, gate, 0.0).astype(BF16)
    part = _dot_tn(weighted, ys_ref[...])

    @pl.when(e == 0)
    def _():
        acc_ref[...] = part

    @pl.when(e > 0)
    def _():
        acc_ref[...] += part

    @pl.when(e == pl.num_programs(2) - 1)
    def _():
        o_ref[...] = x_ref[...] + g_ref[...] * acc_ref[...]


def _combine(ys, pos, aff, x, mod4, mrow, cap):
    bsz, seq, d = x.shape
    tn = 1024
    return pl.pallas_call(
        functools.partial(_combine_kernel, cap),
        grid=(bsz, d // tn, N_EXPERTS),
        in_specs=[pl.BlockSpec((None, None, cap, tn), lambda b, n, e: (e, b, 0, n)),
                  pl.BlockSpec((None, N_EXPERTS, seq), lambda b, n, e: (b, 0, 0)),
                  pl.BlockSpec((None, N_EXPERTS, seq), lambda b, n, e: (b, 0, 0)),
                  pl.BlockSpec((None, seq, tn), lambda b, n, e: (b, 0, n)),
                  pl.BlockSpec((None, None, 1, tn), lambda b, n, e: (mrow(b), 5, 0, n))],
        out_specs=pl.BlockSpec((None, seq, tn), lambda b, n, e: (b, 0, n)),
        out_shape=jax.ShapeDtypeStruct((bsz, seq, d), F32),
        scratch_shapes=[pltpu.VMEM((seq, tn), F32)],
        compiler_params=_params("arbitrary", "arbitrary", "arbitrary"),
        name="moe_combine",
    )(ys, pos, aff, x, mod4)


def _final_kernel(x_ref, g_ref, o_ref):
    x = x_ref[...]
    ms = jnp.mean(x * x, axis=-1, keepdims=True)
    o_ref[...] = x * lax.rsqrt(ms + EPS) * g_ref[...]


def _final_norm(x, g2):
    bsz, seq, d = x.shape
    tm = min(seq, 1024)
    return pl.pallas_call(
        _final_kernel,
        grid=(bsz, seq // tm),
        in_specs=[pl.BlockSpec((None, tm, d), lambda b, i: (b, i, 0)),
                  pl.BlockSpec((1, d), lambda b, i: (0, 0))],
        out_specs=pl.BlockSpec((None, tm, d), lambda b, i: (b, i, 0)),
        out_shape=jax.ShapeDtypeStruct((bsz, seq, d), F32),
        compiler_params=_params("arbitrary", "arbitrary"),
        name="final_norm",
    )(x, g2)


def _angles(rows, cols, n):
    r = lax.broadcasted_iota(jnp.int32, (rows, cols), 0)
    c = lax.broadcasted_iota(jnp.int32, (rows, cols), 1)
    return ((r * c) % n).astype(F32) * (2.0 * math.pi / n)


def _fourier_table(seq):
    ang = _angles(seq, seq, seq)
    return jnp.concatenate([jnp.cos(ang), -jnp.sin(ang)], axis=1).astype(BF16)


def _channel_dft(seq):
    ang = _angles(FN_GROUP, FN_GROUP, FN_GROUP)
    scale = 1.0 / math.sqrt(seq * FN_GROUP)
    eye = jnp.eye(FN_WIDTH // FN_GROUP, dtype=F32)
    return jnp.stack([jnp.kron(eye, jnp.cos(ang) * scale), jnp.kron(eye, jnp.sin(ang) * scale)])


def _hyena_dft(seq):
    nfft = 2 * seq
    ang = _angles(seq, seq, nfft)
    s = lax.broadcasted_iota(jnp.int32, (seq, seq), 1)
    f = lax.broadcasted_iota(jnp.int32, (seq, seq), 0)
    nyq = (1 - 2 * (s % 2)).astype(F32)
    sin_block = jnp.where(f == 0, nyq, jnp.sin(ang))
    dft = jnp.concatenate([jnp.cos(ang), sin_block], axis=0).astype(BF16)
    return dft, dft.T


def _tri_incl(seq):
    r = lax.broadcasted_iota(jnp.int32, (seq, seq), 0)
    c = lax.broadcasted_iota(jnp.int32, (seq, seq), 1)
    return (r <= c).astype(BF16)


def _pad_to(a, shape):
    return jnp.pad(a, [(0, t - s) for s, t in zip(a.shape, shape)])


def kernel(x, c, ctx, c_ctx, norm_mix_g, norm_ffn_g, final_norm_g, w_mod, b_mod, w_in, w_out, w_fnet,
           hy_conv_w, hy_conv_b, hy_w1, hy_b1, hy_w2, hy_b2, hy_w3, hy_b3, hy_w_out, hy_freq, hy_bias,
           hg_lb, hg_norm_g, w_router, w_gate, w_up, w_down):
    bsz, seq, d = x.shape
    ctx_len = ctx.shape[1]
    depth = w_in.shape[0]

    p = jax.nn.softmax(hg_lb.astype(F32), axis=0)
    lbs4 = (jnp.cumsum(p, axis=0) - p[0:1]).reshape(depth, 2, 1, HG_WIDTH)
    w_in_bf = w_in.astype(BF16)
    w_out_bf = w_out.astype(BF16)
    w_router_t = jnp.swapaxes(w_router, 1, 2)
    g_mix3 = norm_mix_g.reshape(depth, 1, d)
    g_ffn3 = norm_ffn_g.reshape(depth, 1, d)
    gain3 = hg_norm_g.reshape(depth, 1, HG_WIDTH)
    b_mod3 = b_mod.reshape(depth, 1, 6 * d)
    conv_b3 = hy_conv_b.reshape(depth, 1, 3 * HY_WIDTH)
    hy_bias3 = hy_bias.reshape(depth, 1, HY_WIDTH)
    mlp = (_pad_to(hy_w1, (depth, HY_PAD, HY_PAD)), _pad_to(hy_b1.reshape(depth, 1, -1), (depth, 1, HY_PAD)),
           _pad_to(hy_w2, (depth, HY_PAD, HY_PAD)), _pad_to(hy_b2.reshape(depth, 1, -1), (depth, 1, HY_PAD)),
           _pad_to(hy_w3, (depth, HY_PAD, HY_PAD)), _pad_to(hy_b3.reshape(depth, 1, -1), (depth, 1, HY_PAD)),
           _pad_to(hy_w_out, (depth, HY_PAD, 2 * HY_WIDTH)), _pad_to(hy_freq, (depth, 8, HY_PAD)))
    max_decay = math.log(1e-2) / 0.3
    min_decay = math.log(1e-2) / 1.5
    deltas = jnp.linspace(min_decay, max_decay, HY_WIDTH, dtype=F32).reshape(1, HY_WIDTH)
    rows = 16
    cc = jnp.zeros((rows, d), F32).at[:bsz].set(c).at[bsz].set(c_ctx)

    tables = {}
    for n in {seq, ctx_len}:
        dft, dft_t = _hyena_dft(n)
        tables[n] = dict(fourier=_fourier_table(n), chan=_channel_dft(n), dft=dft, dft_t=dft_t, tri=_tri_incl(n))

    x_row = lambda b: b
    ctx_row = lambda b: bsz
    zero_state = jnp.zeros((bsz, HG_HEADS, HG_HEAD, HG_HEAD), F32)

    def mixers(hh, y_hg, n, l):
        t = tables[n]
        y_fn = _fourier(hh, _fn_prep(t["chan"], w_fnet, l), t["fourier"])
        kc, ks = _hy_filter(n, t["dft"], mlp, deltas, l)
        z, x0 = _hy_pre(hh, hy_conv_w, conv_b3, l)
        yc, ys = _hy_fwd(z, t["dft"], kc, ks)
        y_hy = _hy_inv(yc, ys, t["dft_t"], z, x0, hy_bias3, l)
        return y_fn, y_hy, y_hg

    def moe(xx, mod4, mrow, n, l):
        cap = EC_CAPACITY * n // N_EXPERTS
        xm, aff = _route(xx, mod4, mrow, g_ffn3, w_router_t, l)
        pos = _topk(aff, tables[n]["tri"], cap)
        xs = _gather(xm, pos, cap)
        ys = _ffn(xs.reshape(N_EXPERTS, bsz * cap, d), w_gate, w_up, w_down, l)
        return _combine(ys.reshape(N_EXPERTS, bsz, cap, d), pos, aff, xx, mod4, mrow, cap)

    xc = ctx
    for l in range(depth):
        last = l == depth - 1
        mod4 = _modulation(cc, w_mod, b_mod3, l).reshape(rows, 6, 1, d)
        h = _in_proj(x, mod4, x_row, g_mix3, w_in_bf, l)
        hc = _in_proj(xc, mod4, ctx_row, g_mix3, w_in_bf, l)
        o_cf, s_f = _hgrn(hc, lbs4, gain3, zero_state, l, False)
        y_hg_c, s_b = _hgrn(hc, lbs4, gain3, zero_state, l, True, o_fwd=o_cf)
        o_xf, _ = _hgrn(h, lbs4, gain3, s_f, l, False)
        y_hg_x, _ = _hgrn(h, lbs4, gain3, s_b, l, True, o_fwd=o_xf)
        x = _out_proj(x, *mixers(h, y_hg_x, seq, l), w_out_bf, mod4, x_row, l)
        if not last:
            xc = _out_proj(xc, *mixers(hc, y_hg_c, ctx_len, l), w_out_bf, mod4, ctx_row, l)
        x = moe(x, mod4, x_row, seq, l)
        if not last:
            xc = moe(xc, mod4, ctx_row, ctx_len, l)
    return _final_norm(x, final_norm_g.reshape(1, d))
```

```python
import functools
import math

import jax
import jax.numpy as jnp
from jax import lax
from jax.experimental import pallas as pl
from jax.experimental.pallas import tpu as pltpu

F32 = jnp.float32
BF16 = jnp.bfloat16

D_MODEL = 2048
FN_WIDTH = 512
FN_GROUP = 128
HY_WIDTH = 512
HG_WIDTH = 1024
HG_HEAD = 128
HG_HEADS = HG_WIDTH // HG_HEAD
HG_F_MIN = 1e-6
IN_WIDTH = FN_WIDTH + 3 * HY_WIDTH + 5 * HG_WIDTH
HY_BANDS = 16
HY_PAD = 128
N_EXPERTS = 16
EC_CAPACITY = 2
EXPERT_FF = 1024
EPS = 1e-6

HG_CHUNK = 128
HG_BASE = 8
VMEM_LIMIT = 56 * 1024 * 1024

_COL_Q, _COL_FF, _COL_FB, _COL_I, _COL_G = 2, 3, 4, 5, 6


def _params(*sem):
    return pltpu.CompilerParams(dimension_semantics=sem, vmem_limit_bytes=VMEM_LIMIT)


def _dot(a, b):
    return jnp.dot(a, b, preferred_element_type=F32)


def _dot_nt(a, b):
    return lax.dot_general(a, b, (((1,), (1,)), ((), ())), preferred_element_type=F32)


def _dot_tn(a, b):
    return lax.dot_general(a, b, (((0,), (0,)), ((), ())), preferred_element_type=F32)


def _split2(x):
    hi = x.astype(BF16)
    lo = (x - hi.astype(F32)).astype(BF16)
    return hi, lo


def _split3(x):
    hi = x.astype(BF16)
    r = x - hi.astype(F32)
    mid = r.astype(BF16)
    lo = (r - mid.astype(F32)).astype(BF16)
    return hi, mid, lo


def _dot3(a, b, dot=_dot):
    ah, al = _split2(a)
    bh, bl = _split2(b)
    return dot(ah, bh) + dot(ah, bl) + dot(al, bh)


def _silu(x):
    return x * jax.nn.sigmoid(x)


def _norm_mod(x, g, sh, sc):
    ms = jnp.mean(x * x, axis=-1, keepdims=True)
    return (x * lax.rsqrt(ms + EPS) * g) * (1.0 + sc) + sh


def _mod_kernel(a_ref, w_ref, b_ref, o_ref):
    a = _silu(a_ref[...]).astype(BF16)
    o_ref[...] = _dot(a, w_ref[...].astype(BF16)) + b_ref[...]


def _modulation(cc, w_mod, b_mod3, l):
    rows, d = cc.shape
    n = w_mod.shape[-1]
    tn = 1024
    return pl.pallas_call(
        _mod_kernel,
        grid=(n // tn,),
        in_specs=[
            pl.BlockSpec((rows, d), lambda j: (0, 0)),
            pl.BlockSpec((None, d, tn), lambda j: (l, 0, j)),
            pl.BlockSpec((None, 1, tn), lambda j: (l, 0, j)),
        ],
        out_specs=pl.BlockSpec((rows, tn), lambda j: (0, j)),
        out_shape=jax.ShapeDtypeStruct((rows, n), F32),
        compiler_params=_params("arbitrary"),
        name="modulation",
    )(cc, w_mod, b_mod3)


def _in_kernel(x_ref, g_ref, sh_ref, sc_ref, w_ref, o_ref, xm_ref):
    @pl.when(pl.program_id(2) == 0)
    def _():
        xm_ref[...] = _norm_mod(x_ref[...], g_ref[...], sh_ref[...], sc_ref[...]).astype(BF16)

    o_ref[...] = _dot(xm_ref[...], w_ref[...]).astype(o_ref.dtype)


def _in_proj(x, mod4, mrow, gamma3, w_in_bf, l):
    bsz, seq, d = x.shape
    n = w_in_bf.shape[-1]
    tm = min(seq, 1024)
    tn = 1024
    return pl.pallas_call(
        _in_kernel,
        grid=(bsz, seq // tm, n // tn),
        in_specs=[
            pl.BlockSpec((None, tm, d), lambda b, i, j: (b, i, 0)),
            pl.BlockSpec((None, 1, d), lambda b, i, j: (l, 0, 0)),
            pl.BlockSpec((None, None, 1, d), lambda b, i, j: (mrow(b), 0, 0, 0)),
            pl.BlockSpec((None, None, 1, d), lambda b, i, j: (mrow(b), 1, 0, 0)),
            pl.BlockSpec((None, d, tn), lambda b, i, j: (l, 0, j)),
        ],
        out_specs=pl.BlockSpec((None, tm, tn), lambda b, i, j: (b, i, j)),
        out_shape=jax.ShapeDtypeStruct((bsz, seq, n), BF16),
        scratch_shapes=[pltpu.VMEM((tm, d), BF16)],
        compiler_params=_params("arbitrary", "arbitrary", "arbitrary"),
        name="in_proj",
    )(x, gamma3, mod4, mod4, w_in_bf)


def _hg_kernel(rev, fuse_out, *refs):
    if fuse_out:
        (zq_ref, zf_ref, zi_ref, g_ref, of_ref, lb_ref, gain_ref, s0_ref, y_ref, st_ref,
         s_ref, qh_ref, kh_ref, qe_ref, kd_ref, sdec_ref) = refs
    else:
        (zq_ref, zf_ref, zi_ref, lb_ref, s0_ref, y_ref, st_ref,
         s_ref, qh_ref, kh_ref, qe_ref, kd_ref, sdec_ref) = refs
    C = HG_CHUNK
    W = HG_WIDTH
    c = pl.program_id(1)

    @pl.when(c == 0)
    def _():
        s_ref[...] = s0_ref[...]

    row = lax.broadcasted_iota(jnp.int32, (C, C), 0)
    col = lax.broadcasted_iota(jnp.int32, (C, C), 1)
    if rev:
        row, col = C - 1 - row, C - 1 - col
    tri = jnp.where(col <= row, 1.0, 0.0)
    sizes = []
    s = C
    while s > HG_BASE:
        sizes.append(s)
        s //= 2
    level_mask = [jnp.where(((row // s) == (col // s)) & ((row % s) >= s // 2) & ((col % s) < s // 2), 1.0, 0.0)
                  for s in sizes]
    base_mask = jnp.where((row // HG_BASE) == (col // HG_BASE), tri, 0.0)
    tri = tri.astype(BF16)

    def ref_rows(b, s, r):
        nb = C // s
        b3 = b.reshape(nb, s, W)
        return jnp.broadcast_to(b3[:, r:r + 1, :], (nb, s, W)).reshape(C, W)

    zq = zq_ref[...].astype(F32)
    zf = zf_ref[...].astype(F32)
    lb = lb_ref[...]
    q = _silu(zq)
    f = jnp.maximum(lb + (1.0 - lb) * jax.nn.sigmoid(zf), HG_F_MIN)
    logf = jnp.log(f)
    kk = 1.0 - f
    hi, mid, lo = _split3(logf)
    b = _dot(tri, hi) + _dot(tri, mid) + _dot(tri, lo)
    btot = b[0:1, :] if rev else b[C - 1:C, :]
    for i, s in enumerate(sizes):
        hh = s // 2
        e = ref_rows(b, s, hh if rev else hh - 1)
        w = jnp.exp(-jnp.abs(b - e))
        qh_ref[i] = (q * w).astype(BF16)
        kh_ref[i] = (kk * w).astype(BF16)
    e = ref_rows(b, HG_BASE, HG_BASE // 2 if rev else HG_BASE // 2 - 1)
    d = b - e
    qh_ref[len(sizes)] = (q * jnp.exp(d)).astype(BF16)
    kh_ref[len(sizes)] = (kk * jnp.exp(-d)).astype(BF16)
    qe_ref[...] = (q * jnp.exp(b)).astype(BF16)
    kd_ref[...] = (kk * jnp.exp(btot - b)).astype(BF16)
    sdec_ref[...] = jnp.exp(btot)

    for h in range(HG_HEADS):
        hs = slice(h * HG_HEAD, (h + 1) * HG_HEAD)
        v = zi_ref[:, hs]
        a = None
        for i in range(len(sizes)):
            m = _dot_nt(qh_ref[i, :, hs], kh_ref[i, :, hs]) * level_mask[i]
            a = m if a is None else a + m
        m = _dot_nt(qh_ref[len(sizes), :, hs], kh_ref[len(sizes), :, hs])
        a = a + jnp.where(base_mask > 0.5, m, 0.0)
        st = s_ref[h]
        o = _dot_nt(qe_ref[:, hs], st.astype(BF16)) + _dot(a.astype(BF16), v)
        s_ref[h] = st * sdec_ref[:, hs] + _dot_tn(v, kd_ref[:, hs])
        if fuse_out:
            o = o + of_ref[:, hs]
            ms = jnp.mean(o * o, axis=-1, keepdims=True)
            g = g_ref[:, hs].astype(F32)
            y_ref[:, hs] = (o * lax.rsqrt(ms + EPS) * gain_ref[:, hs] * _silu(g)).astype(y_ref.dtype)
        else:
            y_ref[:, hs] = o

    @pl.when(c == pl.num_programs(1) - 1)
    def _():
        st_ref[...] = s_ref[...]


def _hgrn(h, lbs4, gain3, s0, l, rev, o_fwd=None):
    bsz, seq, _ = h.shape
    C = HG_CHUNK
    nc = seq // C
    W = HG_WIDTH
    fuse_out = o_fwd is not None
    n_factor = (C // HG_BASE).bit_length()
    cidx = (lambda c: nc - 1 - c) if rev else (lambda c: c)

    def hcol(k):
        return pl.BlockSpec((None, C, W), lambda b, c: (b, cidx(c), k))

    lb_spec = pl.BlockSpec((None, None, 1, W), lambda b, c: (l, 1 if rev else 0, 0, 0))
    s_spec = pl.BlockSpec((None, HG_HEADS, HG_HEAD, HG_HEAD), lambda b, c: (b, 0, 0, 0))
    o_spec = pl.BlockSpec((None, C, W), lambda b, c: (b, cidx(c), 0))
    if fuse_out:
        in_specs = [hcol(_COL_Q), hcol(_COL_FB if rev else _COL_FF), hcol(_COL_I), hcol(_COL_G), o_spec,
                    lb_spec, pl.BlockSpec((None, 1, W), lambda b, c: (l, 0, 0)), s_spec]
        args = (h, h, h, h, o_fwd, lbs4, gain3, s0)
        out_dtype = BF16
    else:
        in_specs = [hcol(_COL_Q), hcol(_COL_FB if rev else _COL_FF), hcol(_COL_I), lb_spec, s_spec]
        args = (h, h, h, lbs4, s0)
        out_dtype = F32
    return pl.pallas_call(
        functools.partial(_hg_kernel, rev, fuse_out),
        grid=(bsz, nc),
        in_specs=in_specs,
        out_specs=[o_spec, s_spec],
        out_shape=[jax.ShapeDtypeStruct((bsz, seq, W), out_dtype),
                   jax.ShapeDtypeStruct((bsz, HG_HEADS, HG_HEAD, HG_HEAD), F32)],
        scratch_shapes=[pltpu.VMEM((HG_HEADS, HG_HEAD, HG_HEAD), F32),
                        pltpu.VMEM((n_factor, C, W), BF16), pltpu.VMEM((n_factor, C, W), BF16),
                        pltpu.VMEM((C, W), BF16), pltpu.VMEM((C, W), BF16), pltpu.VMEM((1, W), F32)],
        compiler_params=_params("arbitrary", "arbitrary"),
        name="hgrn_bwd" if rev else "hgrn_fwd",
    )(*args)


def _fn_prep_kernel(cs_ref, w_ref, o_ref):
    w = w_ref[...]
    o_ref[:, :FN_WIDTH] = _dot3(cs_ref[0], w).astype(BF16)
    o_ref[:, FN_WIDTH:] = _dot3(cs_ref[1], w).astype(BF16)


def _fn_prep(chan_dft, w_fnet, l):
    return pl.pallas_call(
        _fn_prep_kernel,
        grid=(1,),
        in_specs=[pl.BlockSpec((2, FN_WIDTH, FN_WIDTH), lambda i: (0, 0, 0)),
                  pl.BlockSpec((None, FN_WIDTH, FN_WIDTH), lambda i: (l, 0, 0))],
        out_specs=pl.BlockSpec((FN_WIDTH, 2 * FN_WIDTH), lambda i: (0, 0)),
        out_shape=jax.ShapeDtypeStruct((FN_WIDTH, 2 * FN_WIDTH), BF16),
        compiler_params=_params("arbitrary"),
        name="fnet_prep",
    )(chan_dft, w_fnet)


def _fn_kernel(u_ref, wc_ref, t_ref, o_ref, p_ref):
    seq = u_ref.shape[0]
    rc = min(seq, 512)

    @pl.when(pl.program_id(1) == 0)
    def _():
        def rows(i, carry):
            r0 = pl.multiple_of(i * rc, rc)
            p = _dot(u_ref[pl.ds(r0, rc), :], wc_ref[...])
            p_ref[pl.ds(r0, rc), :] = p[:, :FN_WIDTH].astype(BF16)
            p_ref[pl.ds(pl.multiple_of(seq + r0, rc), rc), :] = p[:, FN_WIDTH:].astype(BF16)
            return carry

        lax.fori_loop(0, seq // rc, rows, 0)

    o_ref[...] = _dot(t_ref[...], p_ref[...]).astype(o_ref.dtype)


def _fourier(h, wc, table):
    bsz, seq, _ = h.shape
    tt = min(seq, 512)
    return pl.pallas_call(
        _fn_kernel,
        grid=(bsz, seq // tt),
        in_specs=[pl.BlockSpec((None, seq, FN_WIDTH), lambda b, t: (b, 0, 0)),
                  pl.BlockSpec((FN_WIDTH, 2 * FN_WIDTH), lambda b, t: (0, 0)),
                  pl.BlockSpec((tt, 2 * seq), lambda b, t: (t, 0))],
        out_specs=pl.BlockSpec((None, tt, FN_WIDTH), lambda b, t: (b, t, 0)),
        out_shape=jax.ShapeDtypeStruct((bsz, seq, FN_WIDTH), BF16),
        scratch_shapes=[pltpu.VMEM((2 * seq, FN_WIDTH), BF16)],
        compiler_params=_params("arbitrary", "arbitrary"),
        name="fourier",
    )(h, wc, table)


def _hy_filter_kernel(seq, wc_ref, ws_ref, w1_ref, b1_ref, w2_ref, b2_ref, w3_ref, b3_ref, wo_ref, fr_ref,
                      dl_ref, kc_ref, ks_ref, h_ref):
    i = pl.program_id(0)
    tf = wc_ref.shape[0]
    nfft = 2 * seq

    @pl.when(i == 0)
    def _():
        pos = lax.broadcasted_iota(jnp.int32, (seq, HY_PAD), 0).astype(F32)
        lane = lax.broadcasted_iota(jnp.int32, (seq, HY_PAD), 1)
        t = pos / float(max(seq - 1, 1))
        w = (2.0 * math.pi) * pos / float(seq)
        band_id = jnp.where(lane <= HY_BANDS, lane - 1, lane - 1 - HY_BANDS).astype(F32)
        band = 1e-4 + band_id * ((HY_BANDS - 1 - 1e-4) / (HY_BANDS - 1))
        arg = band * w
        z = jnp.where(lane == 0, t,
                      jnp.where(lane <= HY_BANDS, jnp.cos(arg),
                                jnp.where(lane <= 2 * HY_BANDS, -jnp.sin(arg), 0.0)))
        fr = fr_ref[...]
        hdn = jnp.sin(fr[0:1] * (_dot3(z, w1_ref[...]) + b1_ref[...]))
        hdn = jnp.sin(fr[1:2] * (_dot3(hdn, w2_ref[...]) + b2_ref[...]))
        hdn = jnp.sin(fr[2:3] * (_dot3(hdn, w3_ref[...]) + b3_ref[...]))
        hf = _dot3(hdn, wo_ref[...])
        decay = jnp.exp(-t[:, 0:1] * jnp.abs(dl_ref[...]))
        first = lax.broadcasted_iota(jnp.int32, (seq, HY_WIDTH), 0) == 0
        h_ref[:, :HY_WIDTH] = (hf[:, :HY_WIDTH] * decay).astype(BF16)
        h_ref[:, HY_WIDTH:] = jnp.where(first, 0.0, hf[:, HY_WIDTH:] * decay).astype(BF16)

    gc = _dot(wc_ref[...], h_ref[...])
    gs = _dot(ws_ref[...], h_ref[...])
    first = (lax.broadcasted_iota(jnp.int32, (tf, HY_WIDTH), 0) + i * tf) == 0
    scale = jnp.where(first, 1.0 / nfft, 2.0 / nfft)
    kc_ref[...] = (gc[:, :HY_WIDTH] + gc[:, HY_WIDTH:]) * scale
    ks_ref[...] = jnp.where(first, gs[:, :HY_WIDTH] + gs[:, HY_WIDTH:], gs[:, :HY_WIDTH] - gs[:, HY_WIDTH:]) * scale


def _hy_filter(seq, dft, mlp, deltas, l):
    w1p, b1p, w2p, b2p, w3p, b3p, wop, frp = mlp
    tf = min(seq, 512)
    nf = seq // tf

    def full(a):
        shp = a.shape[1:]
        return pl.BlockSpec((None,) + shp, lambda i: (l,) + (0,) * len(shp))

    return pl.pallas_call(
        functools.partial(_hy_filter_kernel, seq),
        grid=(nf,),
        in_specs=[pl.BlockSpec((tf, seq), lambda i: (i, 0)),
                  pl.BlockSpec((tf, seq), lambda i: (i + nf, 0)),
                  full(w1p), full(b1p), full(w2p), full(b2p), full(w3p), full(b3p), full(wop), full(frp),
                  pl.BlockSpec((1, HY_WIDTH), lambda i: (0, 0))],
        out_specs=[pl.BlockSpec((tf, HY_WIDTH), lambda i: (i, 0)),
                   pl.BlockSpec((tf, HY_WIDTH), lambda i: (i, 0))],
        out_shape=[jax.ShapeDtypeStruct((seq, HY_WIDTH), F32), jax.ShapeDtypeStruct((seq, HY_WIDTH), F32)],
        scratch_shapes=[pltpu.VMEM((seq, 2 * HY_WIDTH), BF16)],
        compiler_params=_params("arbitrary"),
        name="hyena_filter",
    )(dft, dft, w1p, b1p, w2p, b2p, w3p, b3p, wop, frp, deltas)


def _hy_pre_kernel(uv_ref, u1_ref, u0_ref, cw_ref, cb_ref, z_ref, x0_ref):
    seq = uv_ref.shape[0]
    rowi = lax.broadcasted_iota(jnp.int32, (seq, 128), 0)
    j = pl.program_id(1)

    def conv(u_ref, part):
        u = u_ref[...].astype(F32)
        prev = jnp.where(rowi == 0, 0.0, pltpu.roll(u, 1, 0))
        nxt = jnp.where(rowi == seq - 1, 0.0, pltpu.roll(u, seq - 1, 0))
        cs = pl.ds(pl.multiple_of(part * HY_WIDTH + j * 128, 128), 128)
        return prev * cw_ref[0:1, cs] + u * cw_ref[1:2, cs] + nxt * cw_ref[2:3, cs] + cb_ref[:, cs]

    z_ref[...] = (conv(u1_ref, 1) * conv(uv_ref, 0)).astype(BF16)
    x0_ref[...] = conv(u0_ref, 2).astype(BF16)


def _hy_pre(h, conv_w, conv_b3, l):
    bsz, seq, _ = h.shape
    lanes = 128
    nj = HY_WIDTH // lanes
    off = FN_WIDTH // lanes

    def part(p):
        return pl.BlockSpec((None, seq, lanes), lambda b, j: (b, 0, off + p * nj + j))

    o_spec = pl.BlockSpec((None, seq, lanes), lambda b, j: (b, 0, j))
    return pl.pallas_call(
        _hy_pre_kernel,
        grid=(bsz, nj),
        in_specs=[part(0), part(1), part(2),
                  pl.BlockSpec((None, 3, 3 * HY_WIDTH), lambda b, j: (l, 0, 0)),
                  pl.BlockSpec((None, 1, 3 * HY_WIDTH), lambda b, j: (l, 0, 0))],
        out_specs=[o_spec, o_spec],
        out_shape=[jax.ShapeDtypeStruct((bsz, seq, HY_WIDTH), BF16)] * 2,
        compiler_params=_params("arbitrary", "arbitrary"),
        name="hyena_pre",
    )(h, h, h, conv_w, conv_b3)


def _hy_fwd_kernel(wc_ref, ws_ref, z_ref, kc_ref, ks_ref, yc_ref, ys_ref):
    tf = wc_ref.shape[0]
    z = z_ref[...]
    uc = _dot(wc_ref[...], z)
    us = _dot(ws_ref[...], z)
    kc = kc_ref[...]
    ks = ks_ref[...]
    first = (lax.broadcasted_iota(jnp.int32, (tf, HY_WIDTH), 0) + pl.program_id(0) * tf) == 0
    ss = us * ks
    yc_ref[...] = (uc * kc - jnp.where(first, 0.0, ss)).astype(BF16)
    ys_ref[...] = jnp.where(first, ss, uc * ks + us * kc).astype(BF16)


def _hy_fwd(z, dft, kc, ks):
    bsz, seq, _ = z.shape
    tf = min(seq, 512)
    nf = seq // tf
    k_spec = pl.BlockSpec((tf, HY_WIDTH), lambda i, b: (i, 0))
    y_spec = pl.BlockSpec((None, tf, HY_WIDTH), lambda i, b: (b, i, 0))
    return pl.pallas_call(
        _hy_fwd_kernel,
        grid=(nf, bsz),
        in_specs=[pl.BlockSpec((tf, seq), lambda i, b: (i, 0)),
                  pl.BlockSpec((tf, seq), lambda i, b: (i + nf, 0)),
                  pl.BlockSpec((None, seq, HY_WIDTH), lambda i, b: (b, 0, 0)),
                  k_spec, k_spec],
        out_specs=[y_spec, y_spec],
        out_shape=[jax.ShapeDtypeStruct((bsz, seq, HY_WIDTH), BF16)] * 2,
        compiler_params=_params("arbitrary", "arbitrary"),
        name="hyena_dft",
    )(dft, dft, z, kc, ks)


def _hy_inv_kernel(tc_ref, ts_ref, yc_ref, ys_ref, z_ref, x0_ref, db_ref, o_ref):
    y = _dot(tc_ref[...], yc_ref[...]) + _dot(ts_ref[...], ys_ref[...])
    z = z_ref[...].astype(F32)
    o_ref[...] = (x0_ref[...].astype(F32) * (y + z * db_ref[...])).astype(BF16)


def _hy_inv(yc, ys, dft_t, z, x0, hy_bias3, l):
    bsz, seq, _ = z.shape
    tt = min(seq, 512)
    y_spec = pl.BlockSpec((None, seq, HY_WIDTH), lambda t, b: (b, 0, 0))
    r_spec = pl.BlockSpec((None, tt, HY_WIDTH), lambda t, b: (b, t, 0))
    return pl.pallas_call(
        _hy_inv_kernel,
        grid=(seq // tt, bsz),
        in_specs=[pl.BlockSpec((tt, seq), lambda t, b: (t, 0)),
                  pl.BlockSpec((tt, seq), lambda t, b: (t, 1)),
                  y_spec, y_spec, r_spec, r_spec,
                  pl.BlockSpec((None, 1, HY_WIDTH), lambda t, b: (l, 0, 0))],
        out_specs=r_spec,
        out_shape=jax.ShapeDtypeStruct((bsz, seq, HY_WIDTH), BF16),
        compiler_params=_params("arbitrary", "arbitrary"),
        name="hyena_idft",
    )(dft_t, dft_t, yc, ys, z, x0, hy_bias3)


def _out_kernel(yf_ref, yh_ref, yg_ref, wf_ref, wh_ref, wg_ref, x_ref, g_ref, o_ref):
    mix = _dot(yf_ref[...], wf_ref[...]) + _dot(yh_ref[...], wh_ref[...]) + _dot(yg_ref[...], wg_ref[...])
    o_ref[...] = x_ref[...] + g_ref[...] * mix


def _out_proj(x, y_fn, y_hy, y_hg, w_out_bf, mod4, mrow, l):
    bsz, seq, d = x.shape
    tm = min(seq, 1024)
    tn = 1024
    half = FN_WIDTH
    return pl.pallas_call(
        _out_kernel,
        grid=(bsz, seq // tm, d // tn),
        in_specs=[pl.BlockSpec((None, tm, half), lambda b, i, j: (b, i, 0)),
                  pl.BlockSpec((None, tm, half), lambda b, i, j: (b, i, 0)),
                  pl.BlockSpec((None, tm, HG_WIDTH), lambda b, i, j: (b, i, 0)),
                  pl.BlockSpec((None, half, tn), lambda b, i, j: (l, 0, j)),
                  pl.BlockSpec((None, half, tn), lambda b, i, j: (l, 1, j)),
                  pl.BlockSpec((None, HG_WIDTH, tn), lambda b, i, j: (l, 1, j)),
                  pl.BlockSpec((None, tm, tn), lambda b, i, j: (b, i, j)),
                  pl.BlockSpec((None, None, 1, tn), lambda b, i, j: (mrow(b), 2, 0, j))],
        out_specs=pl.BlockSpec((None, tm, tn), lambda b, i, j: (b, i, j)),
        out_shape=jax.ShapeDtypeStruct((bsz, seq, d), F32),
        compiler_params=_params("arbitrary", "arbitrary", "arbitrary"),
        name="out_proj",
    )(y_fn, y_hy, y_hg, w_out_bf, w_out_bf, w_out_bf, x, mod4)


def _route_kernel(x_ref, g_ref, sh_ref, sc_ref, wr_ref, xm_ref, aff_ref):
    xm = _norm_mod(x_ref[...], g_ref[...], sh_ref[...], sc_ref[...])
    xm_ref[...] = xm.astype(BF16)
    logits = _dot3(wr_ref[...], xm, dot=_dot_nt)
    mx = jnp.max(logits, axis=0, keepdims=True)
    ex = jnp.exp(logits - mx)
    aff_ref[...] = ex / jnp.sum(ex, axis=0, keepdims=True)


def _route(x, mod4, mrow, gamma3, w_router_t, l):
    bsz, seq, d = x.shape
    tm = min(seq, 512)
    return pl.pallas_call(
        _route_kernel,
        grid=(bsz, seq // tm),
        in_specs=[pl.BlockSpec((None, tm, d), lambda b, i: (b, i, 0)),
                  pl.BlockSpec((None, 1, d), lambda b, i: (l, 0, 0)),
                  pl.BlockSpec((None, None, 1, d), lambda b, i: (mrow(b), 3, 0, 0)),
                  pl.BlockSpec((None, None, 1, d), lambda b, i: (mrow(b), 4, 0, 0)),
                  pl.BlockSpec((None, N_EXPERTS, d), lambda b, i: (l, 0, 0))],
        out_specs=[pl.BlockSpec((None, tm, d), lambda b, i: (b, i, 0)),
                   pl.BlockSpec((None, N_EXPERTS, tm), lambda b, i: (b, 0, i))],
        out_shape=[jax.ShapeDtypeStruct((bsz, seq, d), BF16),
                   jax.ShapeDtypeStruct((bsz, N_EXPERTS, seq), F32)],
        compiler_params=_params("arbitrary", "arbitrary"),
        name="moe_route",
    )(x, gamma3, mod4, mod4, w_router_t)


def _topk_kernel(cap, aff_ref, tri_ref, pos_ref):
    a = aff_ref[...]

    def count(mask):
        return jnp.sum(jnp.where(mask, 1.0, 0.0), axis=1, keepdims=True)

    def as_float(bits):
        return pltpu.bitcast(jnp.broadcast_to(bits, a.shape), F32)

    def step(i, thr_bits):
        cand = thr_bits | jnp.left_shift(jnp.int32(1), 30 - i)
        return jnp.where(count(a >= as_float(cand)) >= cap, cand, thr_bits)

    thr = as_float(lax.fori_loop(0, 31, step, jnp.zeros((a.shape[0], 1), jnp.int32)))
    above = a > thr
    tie = a == thr
    room = cap - count(above)
    tie_rank = _dot(jnp.where(tie, 1.0, 0.0).astype(BF16), tri_ref[...])
    sel = jnp.where(above, 1.0, jnp.where(tie, jnp.where(tie_rank <= room, 1.0, 0.0), 0.0))
    slot = _dot(sel.astype(BF16), tri_ref[...]) - 1.0
    pos_ref[...] = jnp.where(sel > 0.5, slot, -1.0).astype(jnp.int32)


def _topk(aff, tri_incl, cap):
    bsz, ne, seq = aff.shape
    return pl.pallas_call(
        functools.partial(_topk_kernel, cap),
        grid=(bsz,),
        in_specs=[pl.BlockSpec((None, ne, seq), lambda b: (b, 0, 0)),
                  pl.BlockSpec((seq, seq), lambda b: (0, 0))],
        out_specs=pl.BlockSpec((None, ne, seq), lambda b: (b, 0, 0)),
        out_shape=jax.ShapeDtypeStruct((bsz, ne, seq), jnp.int32),
        compiler_params=_params("arbitrary"),
        name="moe_topk",
    )(aff, tri_incl)


def _gather_kernel(cap, xm_ref, pos_ref, o_ref):
    seq = xm_ref.shape[0]
    pos = pos_ref[pl.ds(pl.program_id(1), 1), :]
    slot = lax.broadcasted_iota(jnp.int32, (cap, seq), 0)
    onehot = jnp.where(slot == pos, 1.0, 0.0).astype(BF16)
    o_ref[...] = _dot(onehot, xm_ref[...]).astype(BF16)


def _gather(xm, pos, cap):
    bsz, seq, d = xm.shape
    return pl.pallas_call(
        functools.partial(_gather_kernel, cap),
        grid=(bsz, N_EXPERTS),
        in_specs=[pl.BlockSpec((None, seq, d), lambda b, e: (b, 0, 0)),
                  pl.BlockSpec((None, N_EXPERTS, seq), lambda b, e: (b, 0, 0))],
        out_specs=pl.BlockSpec((None, None, cap, d), lambda b, e: (e, b, 0, 0)),
        out_shape=jax.ShapeDtypeStruct((N_EXPERTS, bsz, cap, d), BF16),
        compiler_params=_params("arbitrary", "arbitrary"),
        name="moe_gather",
    )(xm, pos)


def _ffn_kernel(xs_ref, wg_ref, wu_ref, wd_ref, o_ref, acc_ref):
    j = pl.program_id(2)
    xs = xs_ref[...]
    hid = _silu(_dot(xs, wg_ref[...].astype(BF16))) * _dot(xs, wu_ref[...].astype(BF16))
    part = _dot(hid.astype(BF16), wd_ref[...].astype(BF16))

    @pl.when(j == 0)
    def _():
        acc_ref[...] = part

    @pl.when(j > 0)
    def _():
        acc_ref[...] += part

    @pl.when(j == pl.num_programs(2) - 1)
    def _():
        o_ref[...] = acc_ref[...].astype(BF16)


def _ffn(xs, w_gate, w_up, w_down, l):
    ne, rows, d = xs.shape
    ff = w_gate.shape[-1]
    tm = min(rows, 1024)
    tj = 512
    return pl.pallas_call(
        _ffn_kernel,
        grid=(ne, rows // tm, ff // tj),
        in_specs=[pl.BlockSpec((None, tm, d), lambda e, m, j: (e, m, 0)),
                  pl.BlockSpec((None, None, d, tj), lambda e, m, j: (l, e, 0, j)),
                  pl.BlockSpec((None, None, d, tj), lambda e, m, j: (l, e, 0, j)),
                  pl.BlockSpec((None, None, tj, d), lambda e, m, j: (l, e, j, 0))],
        out_specs=pl.BlockSpec((None, tm, d), lambda e, m, j: (e, m, 0)),
        out_shape=jax.ShapeDtypeStruct((ne, rows, d), BF16),
        scratch_shapes=[pltpu.VMEM((tm, d), F32)],
        compiler_params=_params("arbitrary", "arbitrary", "arbitrary"),
        name="moe_ffn",
    )(xs, w_gate, w_up, w_down)


def _combine_kernel(cap, ys_ref, pos_ref, aff_ref, x_ref, g_ref, o_ref, pos_t_ref, gate_t_ref, w_ref):
    i = pl.program_id(1)
    tm = x_ref.shape[0]
    ne = pos_ref.shape[0]

    @pl.when(i == 0)
    def _():
        pos_t_ref[...] = pos_ref[...].astype(F32).T
        gate_t_ref[...] = aff_ref[...].T

    r0 = pl.multiple_of(i * tm, tm)
    pos = pos_t_ref[pl.ds(r0, tm), :]
    gate = gate_t_ref[pl.ds(r0, tm), :]
    if cap % 128 == 0:
        lane = lax.broadcasted_iota(jnp.int32, (tm, cap), 1).astype(F32)
        for e in range(ne):
            w_ref[:, e * cap:(e + 1) * cap] = jnp.where(lane == pos[:, e:e + 1], gate[:, e:e + 1], 0.0).astype(BF16)
    else:
        lane = lax.broadcasted_iota(jnp.int32, (tm, ne * cap), 1).astype(F32)
        w = jnp.zeros((tm, ne * cap), F32)
        for e in range(ne):
            hit = jnp.logical_and(lane == pos[:, e:e + 1] + float(e * cap), pos[:, e:e + 1] >= 0.0)
            w = w + jnp.where(hit, gate[:, e:e + 1], 0.0)
        w_ref[...] = w.astype(BF16)
    ys = ys_ref[...].reshape(ne * cap, ys_ref.shape[-1])
    o_ref[...] = x_ref[...] + g_ref[...] * _dot(w_ref[...], ys)


def _combine(ys, pos, aff, x, mod4, mrow, cap):
    bsz, seq, d = x.shape
    ne = pos.shape[1]
    tm = min(seq, 256)
    return pl.pallas_call(
        functools.partial(_combine_kernel, cap),
        grid=(bsz, seq // tm),
        in_specs=[pl.BlockSpec((ne, None, cap, d), lambda b, i: (0, b, 0, 0)),
                  pl.BlockSpec((None, ne, seq), lambda b, i: (b, 0, 0)),
                  pl.BlockSpec((None, ne, seq), lambda b, i: (b, 0, 0)),
                  pl.BlockSpec((None, tm, d), lambda b, i: (b, i, 0)),
                  pl.BlockSpec((None, None, 1, d), lambda b, i: (mrow(b), 5, 0, 0))],
        out_specs=pl.BlockSpec((None, tm, d), lambda b, i: (b, i, 0)),
        out_shape=jax.ShapeDtypeStruct((bsz, seq, d), F32),
        scratch_shapes=[pltpu.VMEM((seq, ne), F32), pltpu.VMEM((seq, ne), F32), pltpu.VMEM((tm, ne * cap), BF16)],
        compiler_params=_params("arbitrary", "arbitrary"),
        name="moe_combine",
    )(ys, pos, aff, x, mod4)


def _final_kernel(x_ref, g_ref, o_ref):
    x = x_ref[...]
    ms = jnp.mean(x * x, axis=-1, keepdims=True)
    o_ref[...] = x * lax.rsqrt(ms + EPS) * g_ref[...]


def _final_norm(x, g2):
    bsz, seq, d = x.shape
    tm = min(seq, 1024)
    return pl.pallas_call(
        _final_kernel,
        grid=(bsz, seq // tm),
        in_specs=[pl.BlockSpec((None, tm, d), lambda b, i: (b, i, 0)),
                  pl.BlockSpec((1, d), lambda b, i: (0, 0))],
        out_specs=pl.BlockSpec((None, tm, d), lambda b, i: (b, i, 0)),
        out_shape=jax.ShapeDtypeStruct((bsz, seq, d), F32),
        compiler_params=_params("arbitrary", "arbitrary"),
        name="final_norm",
    )(x, g2)


def _angles(row_ids, cols, n):
    c = lax.broadcasted_iota(jnp.int32, (row_ids.shape[0], cols), 1)
    return ((row_ids * c) % n).astype(F32) * (2.0 * math.pi / n)


def _cos_sin(rows, cols, n):
    step = min(rows, 32)
    hi = _angles(jnp.arange(0, rows, step, dtype=jnp.int32)[:, None], cols, n)[:, None, :]
    lo = _angles(jnp.arange(step, dtype=jnp.int32)[:, None], cols, n)[None, :, :]
    cos = jnp.cos(hi) * jnp.cos(lo) - jnp.sin(hi) * jnp.sin(lo)
    sin = jnp.sin(hi) * jnp.cos(lo) + jnp.cos(hi) * jnp.sin(lo)
    return cos.reshape(rows, cols), sin.reshape(rows, cols)


def _fourier_table(seq):
    cos, sin = _cos_sin(seq, seq, seq)
    return jnp.concatenate([cos, -sin], axis=1).astype(BF16)


def _channel_dft(seq):
    ang = _angles(jnp.arange(FN_GROUP, dtype=jnp.int32)[:, None], FN_GROUP, FN_GROUP)
    scale = 1.0 / math.sqrt(seq * FN_GROUP)
    eye = jnp.eye(FN_WIDTH // FN_GROUP, dtype=F32)
    return jnp.stack([jnp.kron(eye, jnp.cos(ang) * scale), jnp.kron(eye, jnp.sin(ang) * scale)])


def _hyena_dft(seq):
    cos, sin = _cos_sin(seq, seq, 2 * seq)
    r = lax.broadcasted_iota(jnp.int32, (seq, seq), 0)
    c = lax.broadcasted_iota(jnp.int32, (seq, seq), 1)
    dft = jnp.concatenate([cos, jnp.where(r == 0, (1 - 2 * (c % 2)).astype(F32), sin)], axis=0).astype(BF16)
    dft_t = jnp.concatenate([cos, jnp.where(c == 0, (1 - 2 * (r % 2)).astype(F32), sin)], axis=1).astype(BF16)
    return dft, dft_t


def _tri_incl(seq):
    r = lax.broadcasted_iota(jnp.int32, (seq, seq), 0)
    c = lax.broadcasted_iota(jnp.int32, (seq, seq), 1)
    return (r <= c).astype(BF16)


def _pad_to(a, shape):
    return jnp.pad(a, [(0, t - s) for s, t in zip(a.shape, shape)])


def kernel(x, c, ctx, c_ctx, norm_mix_g, norm_ffn_g, final_norm_g, w_mod, b_mod, w_in, w_out, w_fnet,
           hy_conv_w, hy_conv_b, hy_w1, hy_b1, hy_w2, hy_b2, hy_w3, hy_b3, hy_w_out, hy_freq, hy_bias,
           hg_lb, hg_norm_g, w_router, w_gate, w_up, w_down):
    bsz, seq, d = x.shape
    ctx_len = ctx.shape[1]
    depth = w_in.shape[0]

    p = jax.nn.softmax(hg_lb.astype(F32), axis=0)
    lbs4 = (jnp.cumsum(p, axis=0) - p[0:1]).reshape(depth, 2, 1, HG_WIDTH)
    w_in_bf = w_in.astype(BF16)
    w_out_bf = w_out.astype(BF16)
    w_router_t = jnp.swapaxes(w_router, 1, 2)
    g_mix3 = norm_mix_g.reshape(depth, 1, d)
    g_ffn3 = norm_ffn_g.reshape(depth, 1, d)
    gain3 = hg_norm_g.reshape(depth, 1, HG_WIDTH)
    b_mod3 = b_mod.reshape(depth, 1, 6 * d)
    conv_b3 = hy_conv_b.reshape(depth, 1, 3 * HY_WIDTH)
    hy_bias3 = hy_bias.reshape(depth, 1, HY_WIDTH)
    mlp = (_pad_to(hy_w1, (depth, HY_PAD, HY_PAD)), _pad_to(hy_b1.reshape(depth, 1, -1), (depth, 1, HY_PAD)),
           _pad_to(hy_w2, (depth, HY_PAD, HY_PAD)), _pad_to(hy_b2.reshape(depth, 1, -1), (depth, 1, HY_PAD)),
           _pad_to(hy_w3, (depth, HY_PAD, HY_PAD)), _pad_to(hy_b3.reshape(depth, 1, -1), (depth, 1, HY_PAD)),
           _pad_to(hy_w_out, (depth, HY_PAD, 2 * HY_WIDTH)), _pad_to(hy_freq, (depth, 8, HY_PAD)))
    max_decay = math.log(1e-2) / 0.3
    min_decay = math.log(1e-2) / 1.5
    deltas = jnp.linspace(min_decay, max_decay, HY_WIDTH, dtype=F32).reshape(1, HY_WIDTH)
    rows = 16
    cc = jnp.zeros((rows, d), F32).at[:bsz].set(c).at[bsz].set(c_ctx)

    tables = {}
    for n in {seq, ctx_len}:
        dft, dft_t = _hyena_dft(n)
        tables[n] = dict(fourier=_fourier_table(n), chan=_channel_dft(n), dft=dft, dft_t=dft_t, tri=_tri_incl(n))

    x_row = lambda b: b
    ctx_row = lambda b: bsz
    zero_state = jnp.zeros((bsz, HG_HEADS, HG_HEAD, HG_HEAD), F32)

    def mixers(hh, y_hg, n, l):
        t = tables[n]
        y_fn = _fourier(hh, _fn_prep(t["chan"], w_fnet, l), t["fourier"])
        kc, ks = _hy_filter(n, t["dft"], mlp, deltas, l)
        z, x0 = _hy_pre(hh, hy_conv_w, conv_b3, l)
        yc, ys = _hy_fwd(z, t["dft"], kc, ks)
        y_hy = _hy_inv(yc, ys, t["dft_t"], z, x0, hy_bias3, l)
        return y_fn, y_hy, y_hg

    def moe(xx, mod4, mrow, n, l):
        cap = EC_CAPACITY * n // N_EXPERTS
        xm, aff = _route(xx, mod4, mrow, g_ffn3, w_router_t, l)
        pos = _topk(aff, tables[n]["tri"], cap)
        xs = _gather(xm, pos, cap)
        ys = _ffn(xs.reshape(N_EXPERTS, bsz * cap, d), w_gate, w_up, w_down, l)
        return _combine(ys.reshape(N_EXPERTS, bsz, cap, d), pos, aff, xx, mod4, mrow, cap)

    xc = ctx
    for l in range(depth):
        last = l == depth - 1
        mod4 = _modulation(cc, w_mod, b_mod3, l).reshape(rows, 6, 1, d)
        h = _in_proj(x, mod4, x_row, g_mix3, w_in_bf, l)
        hc = _in_proj(xc, mod4, ctx_row, g_mix3, w_in_bf, l)
        o_cf, s_f = _hgrn(hc, lbs4, gain3, zero_state, l, False)
        y_hg_c, s_b = _hgrn(hc, lbs4, gain3, zero_state, l, True, o_fwd=o_cf)
        o_xf, _ = _hgrn(h, lbs4, gain3, s_f, l, False)
        y_hg_x, _ = _hgrn(h, lbs4, gain3, s_b, l, True, o_fwd=o_xf)
        x = _out_proj(x, *mixers(h, y_hg_x, seq, l), w_out_bf, mod4, x_row, l)
        if not last:
            xc = _out_proj(xc, *mixers(hc, y_hg_c, ctx_len, l), w_out_bf, mod4, ctx_row, l)
        x = moe(x, mod4, x_row, seq, l)
        if not last:
            xc = moe(xc, mod4, ctx_row, ctx_len, l)
    return _final_norm(x, final_norm_g.reshape(1, d))
```

```python
import functools
import math

import jax
import jax.numpy as jnp
from jax import lax
from jax.experimental import pallas as pl
from jax.experimental.pallas import tpu as pltpu

F32 = jnp.float32
BF16 = jnp.bfloat16

D_MODEL = 2048
FN_WIDTH = 512
FN_GROUP = 128
HY_WIDTH = 512
HG_WIDTH = 1024
HG_HEAD = 128
HG_HEADS = HG_WIDTH // HG_HEAD
HG_F_MIN = 1e-6
IN_WIDTH = FN_WIDTH + 3 * HY_WIDTH + 5 * HG_WIDTH
HY_BANDS = 16
HY_PAD = 128
N_EXPERTS = 16
EC_CAPACITY = 2
EXPERT_FF = 1024
EPS = 1e-6

HG_CHUNK = 128
HG_BASE = 8
HG_FAST_MAX_EXP = 80.0
VMEM_LIMIT = 56 * 1024 * 1024

_COL_Q, _COL_FF, _COL_FB, _COL_I, _COL_G = 2, 3, 4, 5, 6


def _params(*sem):
    return pltpu.CompilerParams(dimension_semantics=sem, vmem_limit_bytes=VMEM_LIMIT)


def _dot(a, b):
    return jnp.dot(a, b, preferred_element_type=F32)


def _dot_nt(a, b):
    return lax.dot_general(a, b, (((1,), (1,)), ((), ())), preferred_element_type=F32)


def _dot_tn(a, b):
    return lax.dot_general(a, b, (((0,), (0,)), ((), ())), preferred_element_type=F32)


def _split2(x):
    hi = x.astype(BF16)
    lo = (x - hi.astype(F32)).astype(BF16)
    return hi, lo


def _split3(x):
    hi = x.astype(BF16)
    r = x - hi.astype(F32)
    mid = r.astype(BF16)
    lo = (r - mid.astype(F32)).astype(BF16)
    return hi, mid, lo


def _dot3(a, b, dot=_dot):
    ah, al = _split2(a)
    bh, bl = _split2(b)
    return dot(ah, bh) + dot(ah, bl) + dot(al, bh)


def _silu(x):
    return x * jax.nn.sigmoid(x)


def _norm_mod(x, g, sh, sc):
    ms = jnp.mean(x * x, axis=-1, keepdims=True)
    return (x * lax.rsqrt(ms + EPS) * g) * (1.0 + sc) + sh


def _mod_kernel(a_ref, w_ref, b_ref, o_ref):
    a = _silu(a_ref[...]).astype(BF16)
    o_ref[...] = _dot(a, w_ref[...].astype(BF16)) + b_ref[...]


def _modulation(cc, w_mod, b_mod3, l):
    rows, d = cc.shape
    n = w_mod.shape[-1]
    tn = 1024
    return pl.pallas_call(
        _mod_kernel,
        grid=(n // tn,),
        in_specs=[
            pl.BlockSpec((rows, d), lambda j: (0, 0)),
            pl.BlockSpec((None, d, tn), lambda j: (l, 0, j)),
            pl.BlockSpec((None, 1, tn), lambda j: (l, 0, j)),
        ],
        out_specs=pl.BlockSpec((rows, tn), lambda j: (0, j)),
        out_shape=jax.ShapeDtypeStruct((rows, n), F32),
        compiler_params=_params("arbitrary"),
        name="modulation",
    )(cc, w_mod, b_mod3)


def _in_kernel(x_ref, g_ref, sh_ref, sc_ref, w_ref, o_ref, xm_ref):
    @pl.when(pl.program_id(2) == 0)
    def _():
        xm_ref[...] = _norm_mod(x_ref[...], g_ref[...], sh_ref[...], sc_ref[...]).astype(BF16)

    o_ref[...] = _dot(xm_ref[...], w_ref[...]).astype(o_ref.dtype)


def _in_proj(x, mod4, mrow, gamma3, w_in_bf, l):
    bsz, seq, d = x.shape
    n = w_in_bf.shape[-1]
    tm = min(seq, 1024)
    tn = 1024
    return pl.pallas_call(
        _in_kernel,
        grid=(bsz, seq // tm, n // tn),
        in_specs=[
            pl.BlockSpec((None, tm, d), lambda b, i, j: (b, i, 0)),
            pl.BlockSpec((None, 1, d), lambda b, i, j: (l, 0, 0)),
            pl.BlockSpec((None, None, 1, d), lambda b, i, j: (mrow(b), 0, 0, 0)),
            pl.BlockSpec((None, None, 1, d), lambda b, i, j: (mrow(b), 1, 0, 0)),
            pl.BlockSpec((None, d, tn), lambda b, i, j: (l, 0, j)),
        ],
        out_specs=pl.BlockSpec((None, tm, tn), lambda b, i, j: (b, i, j)),
        out_shape=jax.ShapeDtypeStruct((bsz, seq, n), BF16),
        scratch_shapes=[pltpu.VMEM((tm, d), BF16)],
        compiler_params=_params("arbitrary", "arbitrary", "arbitrary"),
        name="in_proj",
    )(x, gamma3, mod4, mod4, w_in_bf)


def _hg_kernel(rev, fuse_out, *refs):
    if fuse_out:
        (zq_ref, zf_ref, zi_ref, g_ref, of_ref, lb_ref, gain_ref, s0_ref, y_ref, st_ref,
         s_ref, qh_ref, kh_ref, qe_ref, kd_ref, sdec_ref) = refs
    else:
        (zq_ref, zf_ref, zi_ref, lb_ref, s0_ref, y_ref, st_ref,
         s_ref, qh_ref, kh_ref, qe_ref, kd_ref, sdec_ref) = refs
    C = HG_CHUNK
    W = HG_WIDTH
    c = pl.program_id(1)

    @pl.when(c == 0)
    def _():
        s_ref[...] = s0_ref[...]

    row = lax.broadcasted_iota(jnp.int32, (C, C), 0)
    col = lax.broadcasted_iota(jnp.int32, (C, C), 1)
    if rev:
        row, col = C - 1 - row, C - 1 - col
    tri = jnp.where(col <= row, 1.0, 0.0)

    def ref_rows(b, s, r):
        nb = C // s
        b3 = b.reshape(nb, s, W)
        return jnp.broadcast_to(b3[:, r:r + 1, :], (nb, s, W)).reshape(C, W)

    def mid_offset(b, s):
        hh = s // 2
        return b - ref_rows(b, s, hh if rev else hh - 1)

    zq = zq_ref[...].astype(F32)
    zf = zf_ref[...].astype(F32)
    lb = lb_ref[...]
    q = _silu(zq)
    f = jnp.maximum(lb + (1.0 - lb) * jax.nn.sigmoid(zf), HG_F_MIN)
    logf = jnp.log(f)
    kk = 1.0 - f
    hi, mid, lo = _split3(logf)
    tri_bf = tri.astype(BF16)
    b = _dot(tri_bf, hi) + _dot(tri_bf, mid) + _dot(tri_bf, lo)
    btot = b[0:1, :] if rev else b[C - 1:C, :]
    qe_ref[...] = (q * jnp.exp(b)).astype(BF16)
    kd_ref[...] = (kk * jnp.exp(btot - b)).astype(BF16)
    sdec_ref[...] = jnp.exp(btot)

    def intra_chunk(halvings, block):
        for i, s in enumerate(halvings):
            w = jnp.exp(-jnp.abs(mid_offset(b, s)))
            qh_ref[i] = (q * w).astype(BF16)
            kh_ref[i] = (kk * w).astype(BF16)
        d = mid_offset(b, block)
        nh = len(halvings)
        qh_ref[nh] = (q * jnp.exp(d)).astype(BF16)
        kh_ref[nh] = (kk * jnp.exp(-d)).astype(BF16)
        level_mask = [jnp.where(((row // s) == (col // s)) & ((row % s) >= s // 2) & ((col % s) < s // 2), 1.0, 0.0)
                      for s in halvings]
        block_mask = jnp.where((row // block) == (col // block), tri, 0.0)

        for h in range(HG_HEADS):
            hs = slice(h * HG_HEAD, (h + 1) * HG_HEAD)
            v = zi_ref[:, hs]
            a = jnp.where(block_mask > 0.5, _dot_nt(qh_ref[nh, :, hs], kh_ref[nh, :, hs]), 0.0)
            for i in range(nh):
                a = a + _dot_nt(qh_ref[i, :, hs], kh_ref[i, :, hs]) * level_mask[i]
            st = s_ref[h]
            o = _dot_nt(qe_ref[:, hs], st.astype(BF16)) + _dot(a.astype(BF16), v)
            s_ref[h] = st * sdec_ref[:, hs] + _dot_tn(v, kd_ref[:, hs])
            if fuse_out:
                o = o + of_ref[:, hs]
                ms = jnp.mean(o * o, axis=-1, keepdims=True)
                g = g_ref[:, hs].astype(F32)
                y_ref[:, hs] = (o * lax.rsqrt(ms + EPS) * gain_ref[:, hs] * _silu(g)).astype(y_ref.dtype)
            else:
                y_ref[:, hs] = o

    half = C // 2
    in_range = jnp.max(jnp.abs(mid_offset(b, half))) <= HG_FAST_MAX_EXP

    @pl.when(in_range)
    def _():
        intra_chunk([C], half)

    @pl.when(jnp.logical_not(in_range))
    def _():
        sizes = []
        s = C
        while s > HG_BASE:
            sizes.append(s)
            s //= 2
        intra_chunk(sizes, HG_BASE)

    @pl.when(c == pl.num_programs(1) - 1)
    def _():
        st_ref[...] = s_ref[...]


def _hgrn(h, lbs4, gain3, s0, l, rev, o_fwd=None):
    bsz, seq, _ = h.shape
    C = HG_CHUNK
    nc = seq // C
    W = HG_WIDTH
    fuse_out = o_fwd is not None
    n_factor = (C // HG_BASE).bit_length()
    cidx = (lambda c: nc - 1 - c) if rev else (lambda c: c)

    def hcol(k):
        return pl.BlockSpec((None, C, W), lambda b, c: (b, cidx(c), k))

    lb_spec = pl.BlockSpec((None, None, 1, W), lambda b, c: (l, 1 if rev else 0, 0, 0))
    s_spec = pl.BlockSpec((None, HG_HEADS, HG_HEAD, HG_HEAD), lambda b, c: (b, 0, 0, 0))
    o_spec = pl.BlockSpec((None, C, W), lambda b, c: (b, cidx(c), 0))
    if fuse_out:
        in_specs = [hcol(_COL_Q), hcol(_COL_FB if rev else _COL_FF), hcol(_COL_I), hcol(_COL_G), o_spec,
                    lb_spec, pl.BlockSpec((None, 1, W), lambda b, c: (l, 0, 0)), s_spec]
        args = (h, h, h, h, o_fwd, lbs4, gain3, s0)
        out_dtype = BF16
    else:
        in_specs = [hcol(_COL_Q), hcol(_COL_FB if rev else _COL_FF), hcol(_COL_I), lb_spec, s_spec]
        args = (h, h, h, lbs4, s0)
        out_dtype = F32
    return pl.pallas_call(
        functools.partial(_hg_kernel, rev, fuse_out),
        grid=(bsz, nc),
        in_specs=in_specs,
        out_specs=[o_spec, s_spec],
        out_shape=[jax.ShapeDtypeStruct((bsz, seq, W), out_dtype),
                   jax.ShapeDtypeStruct((bsz, HG_HEADS, HG_HEAD, HG_HEAD), F32)],
        scratch_shapes=[pltpu.VMEM((HG_HEADS, HG_HEAD, HG_HEAD), F32),
                        pltpu.VMEM((n_factor, C, W), BF16), pltpu.VMEM((n_factor, C, W), BF16),
                        pltpu.VMEM((C, W), BF16), pltpu.VMEM((C, W), BF16), pltpu.VMEM((1, W), F32)],
        compiler_params=_params("arbitrary", "arbitrary"),
        name="hgrn_bwd" if rev else "hgrn_fwd",
    )(*args)


def _fn_prep_kernel(cs_ref, w_ref, o_ref):
    w = w_ref[...]
    o_ref[:, :FN_WIDTH] = _dot3(cs_ref[0], w).astype(BF16)
    o_ref[:, FN_WIDTH:] = _dot3(cs_ref[1], w).astype(BF16)


def _fn_prep(chan_dft, w_fnet, l):
    return pl.pallas_call(
        _fn_prep_kernel,
        grid=(1,),
        in_specs=[pl.BlockSpec((2, FN_WIDTH, FN_WIDTH), lambda i: (0, 0, 0)),
                  pl.BlockSpec((None, FN_WIDTH, FN_WIDTH), lambda i: (l, 0, 0))],
        out_specs=pl.BlockSpec((FN_WIDTH, 2 * FN_WIDTH), lambda i: (0, 0)),
        out_shape=jax.ShapeDtypeStruct((FN_WIDTH, 2 * FN_WIDTH), BF16),
        compiler_params=_params("arbitrary"),
        name="fnet_prep",
    )(chan_dft, w_fnet)


def _fn_kernel(u_ref, wc_ref, t_ref, o_ref, p_ref):
    seq = u_ref.shape[0]
    rc = min(seq, 512)

    @pl.when(pl.program_id(1) == 0)
    def _():
        def rows(i, carry):
            r0 = pl.multiple_of(i * rc, rc)
            p = _dot(u_ref[pl.ds(r0, rc), :], wc_ref[...])
            p_ref[pl.ds(r0, rc), :] = p[:, :FN_WIDTH].astype(BF16)
            p_ref[pl.ds(pl.multiple_of(seq + r0, rc), rc), :] = p[:, FN_WIDTH:].astype(BF16)
            return carry

        lax.fori_loop(0, seq // rc, rows, 0)

    o_ref[...] = _dot(t_ref[...], p_ref[...]).astype(o_ref.dtype)


def _fourier(h, wc, table):
    bsz, seq, _ = h.shape
    tt = min(seq, 512)
    return pl.pallas_call(
        _fn_kernel,
        grid=(bsz, seq // tt),
        in_specs=[pl.BlockSpec((None, seq, FN_WIDTH), lambda b, t: (b, 0, 0)),
                  pl.BlockSpec((FN_WIDTH, 2 * FN_WIDTH), lambda b, t: (0, 0)),
                  pl.BlockSpec((tt, 2 * seq), lambda b, t: (t, 0))],
        out_specs=pl.BlockSpec((None, tt, FN_WIDTH), lambda b, t: (b, t, 0)),
        out_shape=jax.ShapeDtypeStruct((bsz, seq, FN_WIDTH), BF16),
        scratch_shapes=[pltpu.VMEM((2 * seq, FN_WIDTH), BF16)],
        compiler_params=_params("arbitrary", "arbitrary"),
        name="fourier",
    )(h, wc, table)


def _hy_filter_kernel(seq, wc_ref, ws_ref, w1_ref, b1_ref, w2_ref, b2_ref, w3_ref, b3_ref, wo_ref, fr_ref,
                      dl_ref, kc_ref, ks_ref, h_ref):
    i = pl.program_id(0)
    tf = wc_ref.shape[0]
    nfft = 2 * seq

    @pl.when(i == 0)
    def _():
        pos = lax.broadcasted_iota(jnp.int32, (seq, HY_PAD), 0).astype(F32)
        lane = lax.broadcasted_iota(jnp.int32, (seq, HY_PAD), 1)
        t = pos / float(max(seq - 1, 1))
        w = (2.0 * math.pi) * pos / float(seq)
        band_id = jnp.where(lane <= HY_BANDS, lane - 1, lane - 1 - HY_BANDS).astype(F32)
        band = 1e-4 + band_id * ((HY_BANDS - 1 - 1e-4) / (HY_BANDS - 1))
        arg = band * w
        z = jnp.where(lane == 0, t,
                      jnp.where(lane <= HY_BANDS, jnp.cos(arg),
                                jnp.where(lane <= 2 * HY_BANDS, -jnp.sin(arg), 0.0)))
        fr = fr_ref[...]
        hdn = jnp.sin(fr[0:1] * (_dot3(z, w1_ref[...]) + b1_ref[...]))
        hdn = jnp.sin(fr[1:2] * (_dot3(hdn, w2_ref[...]) + b2_ref[...]))
        hdn = jnp.sin(fr[2:3] * (_dot3(hdn, w3_ref[...]) + b3_ref[...]))
        hf = _dot3(hdn, wo_ref[...])
        decay = jnp.exp(-t[:, 0:1] * jnp.abs(dl_ref[...]))
        first = lax.broadcasted_iota(jnp.int32, (seq, HY_WIDTH), 0) == 0
        h_ref[:, :HY_WIDTH] = (hf[:, :HY_WIDTH] * decay).astype(BF16)
        h_ref[:, HY_WIDTH:] = jnp.where(first, 0.0, hf[:, HY_WIDTH:] * decay).astype(BF16)

    gc = _dot(wc_ref[...], h_ref[...])
    gs = _dot(ws_ref[...], h_ref[...])
    first = (lax.broadcasted_iota(jnp.int32, (tf, HY_WIDTH), 0) + i * tf) == 0
    scale = jnp.where(first, 1.0 / nfft, 2.0 / nfft)
    kc_ref[...] = (gc[:, :HY_WIDTH] + gc[:, HY_WIDTH:]) * scale
    ks_ref[...] = jnp.where(first, gs[:, :HY_WIDTH] + gs[:, HY_WIDTH:], gs[:, :HY_WIDTH] - gs[:, HY_WIDTH:]) * scale


def _hy_filter(seq, dft, mlp, deltas, l):
    w1p, b1p, w2p, b2p, w3p, b3p, wop, frp = mlp
    tf = min(seq, 512)
    nf = seq // tf

    def full(a):
        shp = a.shape[1:]
        return pl.BlockSpec((None,) + shp, lambda i: (l,) + (0,) * len(shp))

    return pl.pallas_call(
        functools.partial(_hy_filter_kernel, seq),
        grid=(nf,),
        in_specs=[pl.BlockSpec((tf, seq), lambda i: (i, 0)),
                  pl.BlockSpec((tf, seq), lambda i: (i + nf, 0)),
                  full(w1p), full(b1p), full(w2p), full(b2p), full(w3p), full(b3p), full(wop), full(frp),
                  pl.BlockSpec((1, HY_WIDTH), lambda i: (0, 0))],
        out_specs=[pl.BlockSpec((tf, HY_WIDTH), lambda i: (i, 0)),
                   pl.BlockSpec((tf, HY_WIDTH), lambda i: (i, 0))],
        out_shape=[jax.ShapeDtypeStruct((seq, HY_WIDTH), F32), jax.ShapeDtypeStruct((seq, HY_WIDTH), F32)],
        scratch_shapes=[pltpu.VMEM((seq, 2 * HY_WIDTH), BF16)],
        compiler_params=_params("arbitrary"),
        name="hyena_filter",
    )(dft, dft, w1p, b1p, w2p, b2p, w3p, b3p, wop, frp, deltas)


def _hy_pre_kernel(uv_ref, u1_ref, u0_ref, cw_ref, cb_ref, z_ref, x0_ref):
    seq = uv_ref.shape[0]
    rowi = lax.broadcasted_iota(jnp.int32, (seq, 128), 0)
    j = pl.program_id(1)

    def conv(u_ref, part):
        u = u_ref[...].astype(F32)
        prev = jnp.where(rowi == 0, 0.0, pltpu.roll(u, 1, 0))
        nxt = jnp.where(rowi == seq - 1, 0.0, pltpu.roll(u, seq - 1, 0))
        cs = pl.ds(pl.multiple_of(part * HY_WIDTH + j * 128, 128), 128)
        return prev * cw_ref[0:1, cs] + u * cw_ref[1:2, cs] + nxt * cw_ref[2:3, cs] + cb_ref[:, cs]

    z_ref[...] = (conv(u1_ref, 1) * conv(uv_ref, 0)).astype(BF16)
    x0_ref[...] = conv(u0_ref, 2).astype(BF16)


def _hy_pre(h, conv_w, conv_b3, l):
    bsz, seq, _ = h.shape
    lanes = 128
    nj = HY_WIDTH // lanes
    off = FN_WIDTH // lanes

    def part(p):
        return pl.BlockSpec((None, seq, lanes), lambda b, j: (b, 0, off + p * nj + j))

    o_spec = pl.BlockSpec((None, seq, lanes), lambda b, j: (b, 0, j))
    return pl.pallas_call(
        _hy_pre_kernel,
        grid=(bsz, nj),
        in_specs=[part(0), part(1), part(2),
                  pl.BlockSpec((None, 3, 3 * HY_WIDTH), lambda b, j: (l, 0, 0)),
                  pl.BlockSpec((None, 1, 3 * HY_WIDTH), lambda b, j: (l, 0, 0))],
        out_specs=[o_spec, o_spec],
        out_shape=[jax.ShapeDtypeStruct((bsz, seq, HY_WIDTH), BF16)] * 2,
        compiler_params=_params("arbitrary", "arbitrary"),
        name="hyena_pre",
    )(h, h, h, conv_w, conv_b3)


def _hy_fwd_kernel(wc_ref, ws_ref, z_ref, kc_ref, ks_ref, yc_ref, ys_ref):
    tf = wc_ref.shape[0]
    z = z_ref[...]
    uc = _dot(wc_ref[...], z)
    us = _dot(ws_ref[...], z)
    kc = kc_ref[...]
    ks = ks_ref[...]
    first = (lax.broadcasted_iota(jnp.int32, (tf, HY_WIDTH), 0) + pl.program_id(0) * tf) == 0
    ss = us * ks
    yc_ref[...] = (uc * kc - jnp.where(first, 0.0, ss)).astype(BF16)
    ys_ref[...] = jnp.where(first, ss, uc * ks + us * kc).astype(BF16)


def _hy_fwd(z, dft, kc, ks):
    bsz, seq, _ = z.shape
    tf = min(seq, 512)
    nf = seq // tf
    k_spec = pl.BlockSpec((tf, HY_WIDTH), lambda i, b: (i, 0))
    y_spec = pl.BlockSpec((None, tf, HY_WIDTH), lambda i, b: (b, i, 0))
    return pl.pallas_call(
        _hy_fwd_kernel,
        grid=(nf, bsz),
        in_specs=[pl.BlockSpec((tf, seq), lambda i, b: (i, 0)),
                  pl.BlockSpec((tf, seq), lambda i, b: (i + nf, 0)),
                  pl.BlockSpec((None, seq, HY_WIDTH), lambda i, b: (b, 0, 0)),
                  k_spec, k_spec],
        out_specs=[y_spec, y_spec],
        out_shape=[jax.ShapeDtypeStruct((bsz, seq, HY_WIDTH), BF16)] * 2,
        compiler_params=_params("arbitrary", "arbitrary"),
        name="hyena_dft",
    )(dft, dft, z, kc, ks)


def _hy_inv_kernel(tc_ref, ts_ref, yc_ref, ys_ref, z_ref, x0_ref, db_ref, o_ref):
    y = _dot(tc_ref[...], yc_ref[...]) + _dot(ts_ref[...], ys_ref[...])
    z = z_ref[...].astype(F32)
    o_ref[...] = (x0_ref[...].astype(F32) * (y + z * db_ref[...])).astype(BF16)


def _hy_inv(yc, ys, dft_t, z, x0, hy_bias3, l):
    bsz, seq, _ = z.shape
    tt = min(seq, 512)
    y_spec = pl.BlockSpec((None, seq, HY_WIDTH), lambda t, b: (b, 0, 0))
    r_spec = pl.BlockSpec((None, tt, HY_WIDTH), lambda t, b: (b, t, 0))
    return pl.pallas_call(
        _hy_inv_kernel,
        grid=(seq // tt, bsz),
        in_specs=[pl.BlockSpec((tt, seq), lambda t, b: (t, 0)),
                  pl.BlockSpec((tt, seq), lambda t, b: (t, 1)),
                  y_spec, y_spec, r_spec, r_spec,
                  pl.BlockSpec((None, 1, HY_WIDTH), lambda t, b: (l, 0, 0))],
        out_specs=r_spec,
        out_shape=jax.ShapeDtypeStruct((bsz, seq, HY_WIDTH), BF16),
        compiler_params=_params("arbitrary", "arbitrary"),
        name="hyena_idft",
    )(dft_t, dft_t, yc, ys, z, x0, hy_bias3)


def _out_route_kernel(yf_ref, yh_ref, yg_ref, wf_ref, wh_ref, wg_ref, x_ref, g1_ref, gam_ref, sh_ref, sc_ref,
                      wr_ref, xo_ref, xm_ref, aff_ref):
    mix = _dot(yf_ref[...], wf_ref[...]) + _dot(yh_ref[...], wh_ref[...]) + _dot(yg_ref[...], wg_ref[...])
    x = x_ref[...] + g1_ref[...] * mix
    xo_ref[...] = x
    xm = _norm_mod(x, gam_ref[...], sh_ref[...], sc_ref[...])
    xm_ref[...] = xm.astype(BF16)
    logits = _dot3(wr_ref[...], xm, dot=_dot_nt)
    mx = jnp.max(logits, axis=0, keepdims=True)
    ex = jnp.exp(logits - mx)
    aff_ref[...] = ex / jnp.sum(ex, axis=0, keepdims=True)


def _out_route(x, y_fn, y_hy, y_hg, w_out_bf, mod4, mrow, gamma3, w_router_t, l):
    bsz, seq, d = x.shape
    tm = min(seq, 512)
    half = FN_WIDTH

    def mod_row(k):
        return pl.BlockSpec((None, None, 1, d), lambda b, i: (mrow(b), k, 0, 0))

    return pl.pallas_call(
        _out_route_kernel,
        grid=(bsz, seq // tm),
        in_specs=[pl.BlockSpec((None, tm, half), lambda b, i: (b, i, 0)),
                  pl.BlockSpec((None, tm, half), lambda b, i: (b, i, 0)),
                  pl.BlockSpec((None, tm, HG_WIDTH), lambda b, i: (b, i, 0)),
                  pl.BlockSpec((None, half, d), lambda b, i: (l, 0, 0)),
                  pl.BlockSpec((None, half, d), lambda b, i: (l, 1, 0)),
                  pl.BlockSpec((None, HG_WIDTH, d), lambda b, i: (l, 1, 0)),
                  pl.BlockSpec((None, tm, d), lambda b, i: (b, i, 0)),
                  mod_row(2),
                  pl.BlockSpec((None, 1, d), lambda b, i: (l, 0, 0)),
                  mod_row(3), mod_row(4),
                  pl.BlockSpec((None, N_EXPERTS, d), lambda b, i: (l, 0, 0))],
        out_specs=[pl.BlockSpec((None, tm, d), lambda b, i: (b, i, 0)),
                   pl.BlockSpec((None, tm, d), lambda b, i: (b, i, 0)),
                   pl.BlockSpec((None, N_EXPERTS, tm), lambda b, i: (b, 0, i))],
        out_shape=[jax.ShapeDtypeStruct((bsz, seq, d), F32),
                   jax.ShapeDtypeStruct((bsz, seq, d), BF16),
                   jax.ShapeDtypeStruct((bsz, N_EXPERTS, seq), F32)],
        compiler_params=_params("arbitrary", "arbitrary"),
        name="out_route",
    )(y_fn, y_hy, y_hg, w_out_bf, w_out_bf, w_out_bf, x, mod4, gamma3, mod4, mod4, w_router_t)


def _topk_kernel(cap, aff_ref, tri_ref, pos_ref):
    a = aff_ref[...]

    def count(mask):
        return jnp.sum(jnp.where(mask, 1.0, 0.0), axis=1, keepdims=True)

    def as_float(bits):
        return pltpu.bitcast(jnp.broadcast_to(bits, a.shape), F32)

    def step(i, thr_bits):
        cand = thr_bits | jnp.left_shift(jnp.int32(1), 30 - i)
        return jnp.where(count(a >= as_float(cand)) >= cap, cand, thr_bits)

    thr = as_float(lax.fori_loop(0, 31, step, jnp.zeros((a.shape[0], 1), jnp.int32)))
    above = a > thr
    tie = a == thr
    room = cap - count(above)
    tie_rank = _dot(jnp.where(tie, 1.0, 0.0).astype(BF16), tri_ref[...])
    sel = jnp.where(above, 1.0, jnp.where(tie, jnp.where(tie_rank <= room, 1.0, 0.0), 0.0))
    slot = _dot(sel.astype(BF16), tri_ref[...]) - 1.0
    pos_ref[...] = jnp.where(sel > 0.5, slot, -1.0).astype(jnp.int32)


def _topk(aff, tri_incl, cap):
    bsz, ne, seq = aff.shape
    return pl.pallas_call(
        functools.partial(_topk_kernel, cap),
        grid=(bsz,),
        in_specs=[pl.BlockSpec((None, ne, seq), lambda b: (b, 0, 0)),
                  pl.BlockSpec((seq, seq), lambda b: (0, 0))],
        out_specs=pl.BlockSpec((None, ne, seq), lambda b: (b, 0, 0)),
        out_shape=jax.ShapeDtypeStruct((bsz, ne, seq), jnp.int32),
        compiler_params=_params("arbitrary"),
        name="moe_topk",
    )(aff, tri_incl)


def _gather_kernel(cap, xm_ref, pos_ref, o_ref):
    seq = xm_ref.shape[0]
    pos = pos_ref[pl.ds(pl.program_id(1), 1), :]
    slot = lax.broadcasted_iota(jnp.int32, (cap, seq), 0)
    onehot = jnp.where(slot == pos, 1.0, 0.0).astype(BF16)
    o_ref[...] = _dot(onehot, xm_ref[...]).astype(BF16)


def _gather(xm, pos, cap):
    bsz, seq, d = xm.shape
    return pl.pallas_call(
        functools.partial(_gather_kernel, cap),
        grid=(bsz, N_EXPERTS),
        in_specs=[pl.BlockSpec((None, seq, d), lambda b, e: (b, 0, 0)),
                  pl.BlockSpec((None, N_EXPERTS, seq), lambda b, e: (b, 0, 0))],
        out_specs=pl.BlockSpec((None, None, cap, d), lambda b, e: (e, b, 0, 0)),
        out_shape=jax.ShapeDtypeStruct((N_EXPERTS, bsz, cap, d), BF16),
        compiler_params=_params("arbitrary", "arbitrary"),
        name="moe_gather",
    )(xm, pos)


def _ffn_kernel(xs_ref, wg_ref, wu_ref, wd_ref, o_ref, acc_ref):
    j = pl.program_id(2)
    xs = xs_ref[...]
    hid = _silu(_dot(xs, wg_ref[...].astype(BF16))) * _dot(xs, wu_ref[...].astype(BF16))
    part = _dot(hid.astype(BF16), wd_ref[...].astype(BF16))

    @pl.when(j == 0)
    def _():
        acc_ref[...] = part

    @pl.when(j > 0)
    def _():
        acc_ref[...] += part

    @pl.when(j == pl.num_programs(2) - 1)
    def _():
        o_ref[...] = acc_ref[...].astype(BF16)


def _ffn(xs, w_gate, w_up, w_down, l):
    ne, rows, d = xs.shape
    ff = w_gate.shape[-1]
    tm = min(rows, 1024)
    tj = 512
    return pl.pallas_call(
        _ffn_kernel,
        grid=(ne, rows // tm, ff // tj),
        in_specs=[pl.BlockSpec((None, tm, d), lambda e, m, j: (e, m, 0)),
                  pl.BlockSpec((None, None, d, tj), lambda e, m, j: (l, e, 0, j)),
                  pl.BlockSpec((None, None, d, tj), lambda e, m, j: (l, e, 0, j)),
                  pl.BlockSpec((None, None, tj, d), lambda e, m, j: (l, e, j, 0))],
        out_specs=pl.BlockSpec((None, tm, d), lambda e, m, j: (e, m, 0)),
        out_shape=jax.ShapeDtypeStruct((ne, rows, d), BF16),
        scratch_shapes=[pltpu.VMEM((tm, d), F32)],
        compiler_params=_params("arbitrary", "arbitrary", "arbitrary"),
        name="moe_ffn",
    )(xs, w_gate, w_up, w_down)


def _combine_kernel(cap, final_norm, *refs):
    if final_norm:
        ys_ref, pos_ref, aff_ref, x_ref, g_ref, fg_ref, o_ref, pos_t_ref, gate_t_ref, w_ref = refs
    else:
        ys_ref, pos_ref, aff_ref, x_ref, g_ref, o_ref, pos_t_ref, gate_t_ref, w_ref = refs
    i = pl.program_id(1)
    tm = x_ref.shape[0]
    ne = pos_ref.shape[0]

    @pl.when(i == 0)
    def _():
        pos_t_ref[...] = pos_ref[...].astype(F32).T
        gate_t_ref[...] = aff_ref[...].T

    r0 = pl.multiple_of(i * tm, tm)
    pos = pos_t_ref[pl.ds(r0, tm), :]
    gate = gate_t_ref[pl.ds(r0, tm), :]
    if cap % 128 == 0:
        lane = lax.broadcasted_iota(jnp.int32, (tm, cap), 1).astype(F32)
        for e in range(ne):
            w_ref[:, e * cap:(e + 1) * cap] = jnp.where(lane == pos[:, e:e + 1], gate[:, e:e + 1], 0.0).astype(BF16)
    else:
        lane = lax.broadcasted_iota(jnp.int32, (tm, ne * cap), 1).astype(F32)
        w = jnp.zeros((tm, ne * cap), F32)
        for e in range(ne):
            hit = jnp.logical_and(lane == pos[:, e:e + 1] + float(e * cap), pos[:, e:e + 1] >= 0.0)
            w = w + jnp.where(hit, gate[:, e:e + 1], 0.0)
        w_ref[...] = w.astype(BF16)
    ys = ys_ref[...].reshape(ne * cap, ys_ref.shape[-1])
    x = x_ref[...] + g_ref[...] * _dot(w_ref[...], ys)
    if final_norm:
        ms = jnp.mean(x * x, axis=-1, keepdims=True)
        x = x * lax.rsqrt(ms + EPS) * fg_ref[...]
    o_ref[...] = x


def _combine(ys, pos, aff, x, mod4, mrow, cap, final_g=None):
    bsz, seq, d = x.shape
    ne = pos.shape[1]
    tm = min(seq, 256)
    final_norm = final_g is not None
    in_specs = [pl.BlockSpec((ne, None, cap, d), lambda b, i: (0, b, 0, 0)),
                pl.BlockSpec((None, ne, seq), lambda b, i: (b, 0, 0)),
                pl.BlockSpec((None, ne, seq), lambda b, i: (b, 0, 0)),
                pl.BlockSpec((None, tm, d), lambda b, i: (b, i, 0)),
                pl.BlockSpec((None, None, 1, d), lambda b, i: (mrow(b), 5, 0, 0))]
    args = (ys, pos, aff, x, mod4)
    if final_norm:
        in_specs.append(pl.BlockSpec((1, d), lambda b, i: (0, 0)))
        args += (final_g,)
    return pl.pallas_call(
        functools.partial(_combine_kernel, cap, final_norm),
        grid=(bsz, seq // tm),
        in_specs=in_specs,
        out_specs=pl.BlockSpec((None, tm, d), lambda b, i: (b, i, 0)),
        out_shape=jax.ShapeDtypeStruct((bsz, seq, d), F32),
        scratch_shapes=[pltpu.VMEM((seq, ne), F32), pltpu.VMEM((seq, ne), F32), pltpu.VMEM((tm, ne * cap), BF16)],
        compiler_params=_params("arbitrary", "arbitrary"),
        name="moe_combine",
    )(*args)


def _angles(row_ids, cols, n):
    c = lax.broadcasted_iota(jnp.int32, (row_ids.shape[0], cols), 1)
    return ((row_ids * c) % n).astype(F32) * (2.0 * math.pi / n)


def _cos_sin(rows, cols, n):
    step = min(rows, 32)
    hi = _angles(jnp.arange(0, rows, step, dtype=jnp.int32)[:, None], cols, n)[:, None, :]
    lo = _angles(jnp.arange(step, dtype=jnp.int32)[:, None], cols, n)[None, :, :]
    cos = jnp.cos(hi) * jnp.cos(lo) - jnp.sin(hi) * jnp.sin(lo)
    sin = jnp.sin(hi) * jnp.cos(lo) + jnp.cos(hi) * jnp.sin(lo)
    return cos.reshape(rows, cols), sin.reshape(rows, cols)


def _fourier_table(seq):
    cos, sin = _cos_sin(seq, seq, seq)
    return jnp.concatenate([cos, -sin], axis=1).astype(BF16)


def _channel_dft(seq):
    ang = _angles(jnp.arange(FN_GROUP, dtype=jnp.int32)[:, None], FN_GROUP, FN_GROUP)
    scale = 1.0 / math.sqrt(seq * FN_GROUP)
    eye = jnp.eye(FN_WIDTH // FN_GROUP, dtype=F32)
    return jnp.stack([jnp.kron(eye, jnp.cos(ang) * scale), jnp.kron(eye, jnp.sin(ang) * scale)])


def _hyena_dft(seq):
    cos, sin = _cos_sin(seq, seq, 2 * seq)
    r = lax.broadcasted_iota(jnp.int32, (seq, seq), 0)
    c = lax.broadcasted_iota(jnp.int32, (seq, seq), 1)
    dft = jnp.concatenate([cos, jnp.where(r == 0, (1 - 2 * (c % 2)).astype(F32), sin)], axis=0).astype(BF16)
    dft_t = jnp.concatenate([cos, jnp.where(c == 0, (1 - 2 * (r % 2)).astype(F32), sin)], axis=1).astype(BF16)
    return dft, dft_t


def _tri_incl(seq):
    r = lax.broadcasted_iota(jnp.int32, (seq, seq), 0)
    c = lax.broadcasted_iota(jnp.int32, (seq, seq), 1)
    return (r <= c).astype(BF16)


def _pad_to(a, shape):
    return jnp.pad(a, [(0, t - s) for s, t in zip(a.shape, shape)])


def kernel(x, c, ctx, c_ctx, norm_mix_g, norm_ffn_g, final_norm_g, w_mod, b_mod, w_in, w_out, w_fnet,
           hy_conv_w, hy_conv_b, hy_w1, hy_b1, hy_w2, hy_b2, hy_w3, hy_b3, hy_w_out, hy_freq, hy_bias,
           hg_lb, hg_norm_g, w_router, w_gate, w_up, w_down):
    bsz, seq, d = x.shape
    ctx_len = ctx.shape[1]
    depth = w_in.shape[0]

    p = jax.nn.softmax(hg_lb.astype(F32), axis=0)
    lbs4 = (jnp.cumsum(p, axis=0) - p[0:1]).reshape(depth, 2, 1, HG_WIDTH)
    w_in_bf = w_in.astype(BF16)
    w_out_bf = w_out.astype(BF16)
    w_router_t = jnp.swapaxes(w_router, 1, 2)
    g_mix3 = norm_mix_g.reshape(depth, 1, d)
    g_ffn3 = norm_ffn_g.reshape(depth, 1, d)
    gain3 = hg_norm_g.reshape(depth, 1, HG_WIDTH)
    b_mod3 = b_mod.reshape(depth, 1, 6 * d)
    conv_b3 = hy_conv_b.reshape(depth, 1, 3 * HY_WIDTH)
    hy_bias3 = hy_bias.reshape(depth, 1, HY_WIDTH)
    mlp = (_pad_to(hy_w1, (depth, HY_PAD, HY_PAD)), _pad_to(hy_b1.reshape(depth, 1, -1), (depth, 1, HY_PAD)),
           _pad_to(hy_w2, (depth, HY_PAD, HY_PAD)), _pad_to(hy_b2.reshape(depth, 1, -1), (depth, 1, HY_PAD)),
           _pad_to(hy_w3, (depth, HY_PAD, HY_PAD)), _pad_to(hy_b3.reshape(depth, 1, -1), (depth, 1, HY_PAD)),
           _pad_to(hy_w_out, (depth, HY_PAD, 2 * HY_WIDTH)), _pad_to(hy_freq, (depth, 8, HY_PAD)))
    max_decay = math.log(1e-2) / 0.3
    min_decay = math.log(1e-2) / 1.5
    deltas = jnp.linspace(min_decay, max_decay, HY_WIDTH, dtype=F32).reshape(1, HY_WIDTH)
    rows = 16
    cc = jnp.zeros((rows, d), F32).at[:bsz].set(c).at[bsz].set(c_ctx)

    tables = {}
    for n in {seq, ctx_len}:
        dft, dft_t = _hyena_dft(n)
        tables[n] = dict(fourier=_fourier_table(n), chan=_channel_dft(n), dft=dft, dft_t=dft_t, tri=_tri_incl(n))

    x_row = lambda b: b
    ctx_row = lambda b: bsz
    zero_state = jnp.zeros((bsz, HG_HEADS, HG_HEAD, HG_HEAD), F32)

    def mixers(hh, y_hg, n, l):
        t = tables[n]
        y_fn = _fourier(hh, _fn_prep(t["chan"], w_fnet, l), t["fourier"])
        kc, ks = _hy_filter(n, t["dft"], mlp, deltas, l)
        z, x0 = _hy_pre(hh, hy_conv_w, conv_b3, l)
        yc, ys = _hy_fwd(z, t["dft"], kc, ks)
        y_hy = _hy_inv(yc, ys, t["dft_t"], z, x0, hy_bias3, l)
        return y_fn, y_hy, y_hg

    def sublayers(xx, hh, y_hg, mod4, mrow, n, l, final_g=None):
        cap = EC_CAPACITY * n // N_EXPERTS
        xx, xm, aff = _out_route(xx, *mixers(hh, y_hg, n, l), w_out_bf, mod4, mrow, g_ffn3, w_router_t, l)
        pos = _topk(aff, tables[n]["tri"], cap)
        xs = _gather(xm, pos, cap)
        ys = _ffn(xs.reshape(N_EXPERTS, bsz * cap, d), w_gate, w_up, w_down, l)
        return _combine(ys.reshape(N_EXPERTS, bsz, cap, d), pos, aff, xx, mod4, mrow, cap, final_g)

    xc = ctx
    for l in range(depth):
        last = l == depth - 1
        mod4 = _modulation(cc, w_mod, b_mod3, l).reshape(rows, 6, 1, d)
        h = _in_proj(x, mod4, x_row, g_mix3, w_in_bf, l)
        hc = _in_proj(xc, mod4, ctx_row, g_mix3, w_in_bf, l)
        o_cf, s_f = _hgrn(hc, lbs4, gain3, zero_state, l, False)
        y_hg_c, s_b = _hgrn(hc, lbs4, gain3, zero_state, l, True, o_fwd=o_cf)
        o_xf, _ = _hgrn(h, lbs4, gain3, s_f, l, False)
        y_hg_x, _ = _hgrn(h, lbs4, gain3, s_b, l, True, o_fwd=o_xf)
        x = sublayers(x, h, y_hg_x, mod4, x_row, seq, l, final_norm_g.reshape(1, d) if last else None)
        if not last:
            xc = sublayers(xc, hc, y_hg_c, mod4, ctx_row, ctx_len, l)
    return x
```

```python
import functools
import math

import jax
import jax.numpy as jnp
from jax import lax
from jax.experimental import pallas as pl
from jax.experimental.pallas import tpu as pltpu

F32 = jnp.float32
BF16 = jnp.bfloat16

D_MODEL = 2048
FN_WIDTH = 512
FN_GROUP = 128
HY_WIDTH = 512
HG_WIDTH = 1024
HG_HEAD = 128
HG_HEADS = HG_WIDTH // HG_HEAD
HG_F_MIN = 1e-6
IN_WIDTH = FN_WIDTH + 3 * HY_WIDTH + 5 * HG_WIDTH
HY_BANDS = 16
HY_PAD = 128
N_EXPERTS = 16
EC_CAPACITY = 2
EXPERT_FF = 1024
EPS = 1e-6

HG_CHUNK = 128
HG_BASE = 8
HG_FAST_MAX_EXP = 80.0
VMEM_LIMIT = 56 * 1024 * 1024

_COL_Q, _COL_FF, _COL_FB, _COL_I, _COL_G = 2, 3, 4, 5, 6


def _params(*sem):
    return pltpu.CompilerParams(dimension_semantics=sem, vmem_limit_bytes=VMEM_LIMIT)


def _dot(a, b):
    return jnp.dot(a, b, preferred_element_type=F32)


def _dot_nt(a, b):
    return lax.dot_general(a, b, (((1,), (1,)), ((), ())), preferred_element_type=F32)


def _dot_tn(a, b):
    return lax.dot_general(a, b, (((0,), (0,)), ((), ())), preferred_element_type=F32)


def _split2(x):
    hi = x.astype(BF16)
    lo = (x - hi.astype(F32)).astype(BF16)
    return hi, lo


def _split3(x):
    hi = x.astype(BF16)
    r = x - hi.astype(F32)
    mid = r.astype(BF16)
    lo = (r - mid.astype(F32)).astype(BF16)
    return hi, mid, lo


def _dot3(a, b, dot=_dot):
    ah, al = _split2(a)
    bh, bl = _split2(b)
    return dot(ah, bh) + dot(ah, bl) + dot(al, bh)


def _silu(x):
    return x * jax.nn.sigmoid(x)


def _norm_mod(x, g, sh, sc):
    ms = jnp.mean(x * x, axis=-1, keepdims=True)
    return (x * lax.rsqrt(ms + EPS) * g) * (1.0 + sc) + sh


def _mod_kernel(a_ref, w_ref, b_ref, o_ref):
    a = _silu(a_ref[...]).astype(BF16)
    o_ref[...] = _dot(a, w_ref[...].astype(BF16)) + b_ref[...]


def _modulation(cc, w_mod, b_mod3, l):
    rows, d = cc.shape
    n = w_mod.shape[-1]
    tn = 1024
    return pl.pallas_call(
        _mod_kernel,
        grid=(n // tn,),
        in_specs=[
            pl.BlockSpec((rows, d), lambda j: (0, 0)),
            pl.BlockSpec((None, d, tn), lambda j: (l, 0, j)),
            pl.BlockSpec((None, 1, tn), lambda j: (l, 0, j)),
        ],
        out_specs=pl.BlockSpec((rows, tn), lambda j: (0, j)),
        out_shape=jax.ShapeDtypeStruct((rows, n), F32),
        compiler_params=_params("arbitrary"),
        name="modulation",
    )(cc, w_mod, b_mod3)


def _in_kernel(x_ref, g_ref, sh_ref, sc_ref, w_ref, o_ref, xm_ref):
    @pl.when(pl.program_id(2) == 0)
    def _():
        xm_ref[...] = _norm_mod(x_ref[...], g_ref[...], sh_ref[...], sc_ref[...]).astype(BF16)

    o_ref[...] = _dot(xm_ref[...], w_ref[...]).astype(o_ref.dtype)


def _in_proj(x, mod4, mrow, gamma3, w_in_bf, l):
    bsz, seq, d = x.shape
    n = w_in_bf.shape[-1]
    tm = min(seq, 1024)
    tn = 1024
    return pl.pallas_call(
        _in_kernel,
        grid=(bsz, seq // tm, n // tn),
        in_specs=[
            pl.BlockSpec((None, tm, d), lambda b, i, j: (b, i, 0)),
            pl.BlockSpec((None, 1, d), lambda b, i, j: (l, 0, 0)),
            pl.BlockSpec((None, None, 1, d), lambda b, i, j: (mrow(b), 0, 0, 0)),
            pl.BlockSpec((None, None, 1, d), lambda b, i, j: (mrow(b), 1, 0, 0)),
            pl.BlockSpec((None, d, tn), lambda b, i, j: (l, 0, j)),
        ],
        out_specs=pl.BlockSpec((None, tm, tn), lambda b, i, j: (b, i, j)),
        out_shape=jax.ShapeDtypeStruct((bsz, seq, n), BF16),
        scratch_shapes=[pltpu.VMEM((tm, d), BF16)],
        compiler_params=_params("arbitrary", "arbitrary", "arbitrary"),
        name="in_proj",
    )(x, gamma3, mod4, mod4, w_in_bf)


def _hg_kernel(rev, fuse_out, *refs):
    if fuse_out:
        (zq_ref, zf_ref, zi_ref, g_ref, of_ref, lb_ref, gain_ref, s0_ref, y_ref, st_ref,
         s_ref, qh_ref, kh_ref, oi_ref) = refs
    else:
        (zq_ref, zf_ref, zi_ref, lb_ref, s0_ref, y_ref, st_ref,
         s_ref, qh_ref, kh_ref, oi_ref) = refs
    C = HG_CHUNK
    W = HG_WIDTH
    c = pl.program_id(1)

    @pl.when(c == 0)
    def _():
        s_ref[...] = s0_ref[...]

    row = lax.broadcasted_iota(jnp.int32, (C, C), 0)
    col = lax.broadcasted_iota(jnp.int32, (C, C), 1)
    if rev:
        row, col = C - 1 - row, C - 1 - col
    tri = jnp.where(col <= row, 1.0, 0.0)

    def ref_rows(b, s, r):
        nb = C // s
        b3 = b.reshape(nb, s, W)
        return jnp.broadcast_to(b3[:, r:r + 1, :], (nb, s, W)).reshape(C, W)

    def mid_offset(b, s):
        hh = s // 2
        return b - ref_rows(b, s, hh if rev else hh - 1)

    zq = zq_ref[...].astype(F32)
    zf = zf_ref[...].astype(F32)
    lb = lb_ref[...]
    q = _silu(zq)
    f = jnp.maximum(lb + (1.0 - lb) * jax.nn.sigmoid(zf), HG_F_MIN)
    logf = jnp.log(f)
    kk = 1.0 - f
    hi, mid, lo = _split3(logf)
    tri_bf = tri.astype(BF16)
    b = _dot(tri_bf, hi) + _dot(tri_bf, mid) + _dot(tri_bf, lo)
    btot = b[0:1, :] if rev else b[C - 1:C, :]
    qe = (q * jnp.exp(b)).astype(BF16)
    kd = (kk * jnp.exp(btot - b)).astype(BF16)
    sdec = jnp.exp(btot)
    v_t = zi_ref[...].T
    for h in range(HG_HEADS):
        hs = slice(h * HG_HEAD, (h + 1) * HG_HEAD)
        st = s_ref[h]
        oi_ref[:, hs] = _dot_nt(qe[:, hs], st.astype(BF16))
        s_ref[h] = st * sdec[:, hs] + _dot(v_t[hs, :], kd[:, hs])

    def intra_chunk(halvings, block):
        for i, s in enumerate(halvings):
            w = jnp.exp(-jnp.abs(mid_offset(b, s)))
            qh_ref[i] = (q * w).astype(BF16)
            kh_ref[i] = (kk * w).astype(BF16)
        d = mid_offset(b, block)
        nh = len(halvings)
        qh_ref[nh] = (q * jnp.exp(d)).astype(BF16)
        kh_ref[nh] = (kk * jnp.exp(-d)).astype(BF16)
        level_mask = [jnp.where(((row // s) == (col // s)) & ((row % s) >= s // 2) & ((col % s) < s // 2), 1.0, 0.0)
                      for s in halvings]
        block_mask = jnp.where((row // block) == (col // block), tri, 0.0)

        heads = [slice(h * HG_HEAD, (h + 1) * HG_HEAD) for h in range(HG_HEADS)]
        scores = [[_dot_nt(qh_ref[i, :, hs], kh_ref[i, :, hs]) for i in range(nh + 1)] for hs in heads]
        for hs, m in zip(heads, scores):
            a = jnp.where(block_mask > 0.5, m[nh], 0.0)
            for i in range(nh):
                a = a + m[i] * level_mask[i]
            o = oi_ref[:, hs] + _dot(a.astype(BF16), zi_ref[:, hs])
            if fuse_out:
                o = o + of_ref[:, hs]
                ms = jnp.mean(o * o, axis=-1, keepdims=True)
                g = g_ref[:, hs].astype(F32)
                y_ref[:, hs] = (o * lax.rsqrt(ms + EPS) * gain_ref[:, hs] * _silu(g)).astype(y_ref.dtype)
            else:
                y_ref[:, hs] = o

    half = C // 2
    in_range = jnp.max(jnp.abs(mid_offset(b, half))) <= HG_FAST_MAX_EXP

    @pl.when(in_range)
    def _():
        intra_chunk([C], half)

    @pl.when(jnp.logical_not(in_range))
    def _():
        sizes = []
        s = C
        while s > HG_BASE:
            sizes.append(s)
            s //= 2
        intra_chunk(sizes, HG_BASE)

    @pl.when(c == pl.num_programs(1) - 1)
    def _():
        st_ref[...] = s_ref[...]


def _hgrn(h, lbs4, gain3, s0, l, rev, o_fwd=None):
    bsz, seq, _ = h.shape
    C = HG_CHUNK
    nc = seq // C
    W = HG_WIDTH
    fuse_out = o_fwd is not None
    n_factor = (C // HG_BASE).bit_length()
    cidx = (lambda c: nc - 1 - c) if rev else (lambda c: c)

    def hcol(k):
        return pl.BlockSpec((None, C, W), lambda b, c: (b, cidx(c), k))

    lb_spec = pl.BlockSpec((None, None, 1, W), lambda b, c: (l, 1 if rev else 0, 0, 0))
    s_spec = pl.BlockSpec((None, HG_HEADS, HG_HEAD, HG_HEAD), lambda b, c: (b, 0, 0, 0))
    o_spec = pl.BlockSpec((None, C, W), lambda b, c: (b, cidx(c), 0))
    if fuse_out:
        in_specs = [hcol(_COL_Q), hcol(_COL_FB if rev else _COL_FF), hcol(_COL_I), hcol(_COL_G), o_spec,
                    lb_spec, pl.BlockSpec((None, 1, W), lambda b, c: (l, 0, 0)), s_spec]
        args = (h, h, h, h, o_fwd, lbs4, gain3, s0)
        out_dtype = BF16
    else:
        in_specs = [hcol(_COL_Q), hcol(_COL_FB if rev else _COL_FF), hcol(_COL_I), lb_spec, s_spec]
        args = (h, h, h, lbs4, s0)
        out_dtype = F32
    return pl.pallas_call(
        functools.partial(_hg_kernel, rev, fuse_out),
        grid=(bsz, nc),
        in_specs=in_specs,
        out_specs=[o_spec, s_spec],
        out_shape=[jax.ShapeDtypeStruct((bsz, seq, W), out_dtype),
                   jax.ShapeDtypeStruct((bsz, HG_HEADS, HG_HEAD, HG_HEAD), F32)],
        scratch_shapes=[pltpu.VMEM((HG_HEADS, HG_HEAD, HG_HEAD), F32),
                        pltpu.VMEM((n_factor, C, W), BF16), pltpu.VMEM((n_factor, C, W), BF16),
                        pltpu.VMEM((C, W), F32)],
        compiler_params=_params("arbitrary", "arbitrary"),
        name="hgrn_bwd" if rev else "hgrn_fwd",
    )(*args)


def _fn_prep_kernel(cs_ref, w_ref, o_ref):
    w = w_ref[...]
    o_ref[:, :FN_WIDTH] = _dot3(cs_ref[0], w).astype(BF16)
    o_ref[:, FN_WIDTH:] = _dot3(cs_ref[1], w).astype(BF16)


def _fn_prep(chan_dft, w_fnet, l):
    return pl.pallas_call(
        _fn_prep_kernel,
        grid=(1,),
        in_specs=[pl.BlockSpec((2, FN_WIDTH, FN_WIDTH), lambda i: (0, 0, 0)),
                  pl.BlockSpec((None, FN_WIDTH, FN_WIDTH), lambda i: (l, 0, 0))],
        out_specs=pl.BlockSpec((FN_WIDTH, 2 * FN_WIDTH), lambda i: (0, 0)),
        out_shape=jax.ShapeDtypeStruct((FN_WIDTH, 2 * FN_WIDTH), BF16),
        compiler_params=_params("arbitrary"),
        name="fnet_prep",
    )(chan_dft, w_fnet)


def _fn_kernel(u_ref, wc_ref, t_ref, o_ref, p_ref):
    seq = u_ref.shape[0]
    rc = min(seq, 512)

    @pl.when(pl.program_id(1) == 0)
    def _():
        def rows(i, carry):
            r0 = pl.multiple_of(i * rc, rc)
            p = _dot(u_ref[pl.ds(r0, rc), :], wc_ref[...])
            p_ref[pl.ds(r0, rc), :] = p[:, :FN_WIDTH].astype(BF16)
            p_ref[pl.ds(pl.multiple_of(seq + r0, rc), rc), :] = p[:, FN_WIDTH:].astype(BF16)
            return carry

        lax.fori_loop(0, seq // rc, rows, 0)

    o_ref[...] = _dot(t_ref[...], p_ref[...]).astype(o_ref.dtype)


def _fourier(h, wc, table):
    bsz, seq, _ = h.shape
    tt = min(seq, 512)
    return pl.pallas_call(
        _fn_kernel,
        grid=(bsz, seq // tt),
        in_specs=[pl.BlockSpec((None, seq, FN_WIDTH), lambda b, t: (b, 0, 0)),
                  pl.BlockSpec((FN_WIDTH, 2 * FN_WIDTH), lambda b, t: (0, 0)),
                  pl.BlockSpec((tt, 2 * seq), lambda b, t: (t, 0))],
        out_specs=pl.BlockSpec((None, tt, FN_WIDTH), lambda b, t: (b, t, 0)),
        out_shape=jax.ShapeDtypeStruct((bsz, seq, FN_WIDTH), BF16),
        scratch_shapes=[pltpu.VMEM((2 * seq, FN_WIDTH), BF16)],
        compiler_params=_params("arbitrary", "arbitrary"),
        name="fourier",
    )(h, wc, table)


def _hy_filter_kernel(seq, wc_ref, ws_ref, w1_ref, b1_ref, w2_ref, b2_ref, w3_ref, b3_ref, wo_ref, fr_ref,
                      dl_ref, kc_ref, ks_ref, h_ref):
    i = pl.program_id(0)
    tf = wc_ref.shape[0]
    nfft = 2 * seq

    @pl.when(i == 0)
    def _():
        pos = lax.broadcasted_iota(jnp.int32, (seq, HY_PAD), 0).astype(F32)
        lane = lax.broadcasted_iota(jnp.int32, (seq, HY_PAD), 1)
        t = pos / float(max(seq - 1, 1))
        w = (2.0 * math.pi) * pos / float(seq)
        band_id = jnp.where(lane <= HY_BANDS, lane - 1, lane - 1 - HY_BANDS).astype(F32)
        band = 1e-4 + band_id * ((HY_BANDS - 1 - 1e-4) / (HY_BANDS - 1))
        arg = band * w
        z = jnp.where(lane == 0, t,
                      jnp.where(lane <= HY_BANDS, jnp.cos(arg),
                                jnp.where(lane <= 2 * HY_BANDS, -jnp.sin(arg), 0.0)))
        fr = fr_ref[...]
        hdn = jnp.sin(fr[0:1] * (_dot3(z, w1_ref[...]) + b1_ref[...]))
        hdn = jnp.sin(fr[1:2] * (_dot3(hdn, w2_ref[...]) + b2_ref[...]))
        hdn = jnp.sin(fr[2:3] * (_dot3(hdn, w3_ref[...]) + b3_ref[...]))
        hf = _dot3(hdn, wo_ref[...])
        decay = jnp.exp(-t[:, 0:1] * jnp.abs(dl_ref[...]))
        first = lax.broadcasted_iota(jnp.int32, (seq, HY_WIDTH), 0) == 0
        h_ref[:, :HY_WIDTH] = (hf[:, :HY_WIDTH] * decay).astype(BF16)
        h_ref[:, HY_WIDTH:] = jnp.where(first, 0.0, hf[:, HY_WIDTH:] * decay).astype(BF16)

    gc = _dot(wc_ref[...], h_ref[...])
    gs = _dot(ws_ref[...], h_ref[...])
    first = (lax.broadcasted_iota(jnp.int32, (tf, HY_WIDTH), 0) + i * tf) == 0
    scale = jnp.where(first, 1.0 / nfft, 2.0 / nfft)
    kc_ref[...] = (gc[:, :HY_WIDTH] + gc[:, HY_WIDTH:]) * scale
    ks_ref[...] = jnp.where(first, gs[:, :HY_WIDTH] + gs[:, HY_WIDTH:], gs[:, :HY_WIDTH] - gs[:, HY_WIDTH:]) * scale


def _hy_filter(seq, dft, mlp, deltas, l):
    w1p, b1p, w2p, b2p, w3p, b3p, wop, frp = mlp
    tf = min(seq, 512)
    nf = seq // tf

    def full(a):
        shp = a.shape[1:]
        return pl.BlockSpec((None,) + shp, lambda i: (l,) + (0,) * len(shp))

    return pl.pallas_call(
        functools.partial(_hy_filter_kernel, seq),
        grid=(nf,),
        in_specs=[pl.BlockSpec((tf, seq), lambda i: (i, 0)),
                  pl.BlockSpec((tf, seq), lambda i: (i + nf, 0)),
                  full(w1p), full(b1p), full(w2p), full(b2p), full(w3p), full(b3p), full(wop), full(frp),
                  pl.BlockSpec((1, HY_WIDTH), lambda i: (0, 0))],
        out_specs=[pl.BlockSpec((tf, HY_WIDTH), lambda i: (i, 0)),
                   pl.BlockSpec((tf, HY_WIDTH), lambda i: (i, 0))],
        out_shape=[jax.ShapeDtypeStruct((seq, HY_WIDTH), F32), jax.ShapeDtypeStruct((seq, HY_WIDTH), F32)],
        scratch_shapes=[pltpu.VMEM((seq, 2 * HY_WIDTH), BF16)],
        compiler_params=_params("arbitrary"),
        name="hyena_filter",
    )(dft, dft, w1p, b1p, w2p, b2p, w3p, b3p, wop, frp, deltas)


def _hy_pre_kernel(uv_ref, u1_ref, u0_ref, cw_ref, cb_ref, z_ref, x0_ref):
    seq = uv_ref.shape[0]
    rowi = lax.broadcasted_iota(jnp.int32, (seq, 128), 0)
    j = pl.program_id(1)

    def conv(u_ref, part):
        u = u_ref[...].astype(F32)
        prev = jnp.where(rowi == 0, 0.0, pltpu.roll(u, 1, 0))
        nxt = jnp.where(rowi == seq - 1, 0.0, pltpu.roll(u, seq - 1, 0))
        cs = pl.ds(pl.multiple_of(part * HY_WIDTH + j * 128, 128), 128)
        return prev * cw_ref[0:1, cs] + u * cw_ref[1:2, cs] + nxt * cw_ref[2:3, cs] + cb_ref[:, cs]

    z_ref[...] = (conv(u1_ref, 1) * conv(uv_ref, 0)).astype(BF16)
    x0_ref[...] = conv(u0_ref, 2).astype(BF16)


def _hy_pre(h, conv_w, conv_b3, l):
    bsz, seq, _ = h.shape
    lanes = 128
    nj = HY_WIDTH // lanes
    off = FN_WIDTH // lanes

    def part(p):
        return pl.BlockSpec((None, seq, lanes), lambda b, j: (b, 0, off + p * nj + j))

    o_spec = pl.BlockSpec((None, seq, lanes), lambda b, j: (b, 0, j))
    return pl.pallas_call(
        _hy_pre_kernel,
        grid=(bsz, nj),
        in_specs=[part(0), part(1), part(2),
                  pl.BlockSpec((None, 3, 3 * HY_WIDTH), lambda b, j: (l, 0, 0)),
                  pl.BlockSpec((None, 1, 3 * HY_WIDTH), lambda b, j: (l, 0, 0))],
        out_specs=[o_spec, o_spec],
        out_shape=[jax.ShapeDtypeStruct((bsz, seq, HY_WIDTH), BF16)] * 2,
        compiler_params=_params("arbitrary", "arbitrary"),
        name="hyena_pre",
    )(h, h, h, conv_w, conv_b3)


def _hy_fwd_kernel(wc_ref, ws_ref, z_ref, kc_ref, ks_ref, yc_ref, ys_ref):
    tf = wc_ref.shape[0]
    z = z_ref[...]
    uc = _dot(wc_ref[...], z)
    us = _dot(ws_ref[...], z)
    kc = kc_ref[...]
    ks = ks_ref[...]
    first = (lax.broadcasted_iota(jnp.int32, (tf, HY_WIDTH), 0) + pl.program_id(0) * tf) == 0
    ss = us * ks
    yc_ref[...] = (uc * kc - jnp.where(first, 0.0, ss)).astype(BF16)
    ys_ref[...] = jnp.where(first, ss, uc * ks + us * kc).astype(BF16)


def _hy_fwd(z, dft, kc, ks):
    bsz, seq, _ = z.shape
    tf = min(seq, 512)
    nf = seq // tf
    k_spec = pl.BlockSpec((tf, HY_WIDTH), lambda i, b: (i, 0))
    y_spec = pl.BlockSpec((None, tf, HY_WIDTH), lambda i, b: (b, i, 0))
    return pl.pallas_call(
        _hy_fwd_kernel,
        grid=(nf, bsz),
        in_specs=[pl.BlockSpec((tf, seq), lambda i, b: (i, 0)),
                  pl.BlockSpec((tf, seq), lambda i, b: (i + nf, 0)),
                  pl.BlockSpec((None, seq, HY_WIDTH), lambda i, b: (b, 0, 0)),
                  k_spec, k_spec],
        out_specs=[y_spec, y_spec],
        out_shape=[jax.ShapeDtypeStruct((bsz, seq, HY_WIDTH), BF16)] * 2,
        compiler_params=_params("arbitrary", "arbitrary"),
        name="hyena_dft",
    )(dft, dft, z, kc, ks)


def _hy_inv_kernel(tc_ref, ts_ref, yc_ref, ys_ref, z_ref, x0_ref, db_ref, o_ref):
    y = _dot(tc_ref[...], yc_ref[...]) + _dot(ts_ref[...], ys_ref[...])
    z = z_ref[...].astype(F32)
    o_ref[...] = (x0_ref[...].astype(F32) * (y + z * db_ref[...])).astype(BF16)


def _hy_inv(yc, ys, dft_t, z, x0, hy_bias3, l):
    bsz, seq, _ = z.shape
    tt = min(seq, 512)
    y_spec = pl.BlockSpec((None, seq, HY_WIDTH), lambda t, b: (b, 0, 0))
    r_spec = pl.BlockSpec((None, tt, HY_WIDTH), lambda t, b: (b, t, 0))
    return pl.pallas_call(
        _hy_inv_kernel,
        grid=(seq // tt, bsz),
        in_specs=[pl.BlockSpec((tt, seq), lambda t, b: (t, 0)),
                  pl.BlockSpec((tt, seq), lambda t, b: (t, 1)),
                  y_spec, y_spec, r_spec, r_spec,
                  pl.BlockSpec((None, 1, HY_WIDTH), lambda t, b: (l, 0, 0))],
        out_specs=r_spec,
        out_shape=jax.ShapeDtypeStruct((bsz, seq, HY_WIDTH), BF16),
        compiler_params=_params("arbitrary", "arbitrary"),
        name="hyena_idft",
    )(dft_t, dft_t, yc, ys, z, x0, hy_bias3)


def _out_route_kernel(yf_ref, yh_ref, yg_ref, wf_ref, wh_ref, wg_ref, x_ref, g1_ref, gam_ref, sh_ref, sc_ref,
                      wr_ref, xo_ref, xm_ref, aff_ref):
    mix = _dot(yf_ref[...], wf_ref[...]) + _dot(yh_ref[...], wh_ref[...]) + _dot(yg_ref[...], wg_ref[...])
    x = x_ref[...] + g1_ref[...] * mix
    xo_ref[...] = x
    xm = _norm_mod(x, gam_ref[...], sh_ref[...], sc_ref[...])
    xm_ref[...] = xm.astype(BF16)
    logits = _dot3(wr_ref[...], xm, dot=_dot_nt)
    mx = jnp.max(logits, axis=0, keepdims=True)
    ex = jnp.exp(logits - mx)
    aff_ref[...] = ex / jnp.sum(ex, axis=0, keepdims=True)


def _out_route(x, y_fn, y_hy, y_hg, w_out_bf, mod4, mrow, gamma3, w_router_t, l):
    bsz, seq, d = x.shape
    tm = min(seq, 512)
    half = FN_WIDTH

    def mod_row(k):
        return pl.BlockSpec((None, None, 1, d), lambda b, i: (mrow(b), k, 0, 0))

    return pl.pallas_call(
        _out_route_kernel,
        grid=(bsz, seq // tm),
        in_specs=[pl.BlockSpec((None, tm, half), lambda b, i: (b, i, 0)),
                  pl.BlockSpec((None, tm, half), lambda b, i: (b, i, 0)),
                  pl.BlockSpec((None, tm, HG_WIDTH), lambda b, i: (b, i, 0)),
                  pl.BlockSpec((None, half, d), lambda b, i: (l, 0, 0)),
                  pl.BlockSpec((None, half, d), lambda b, i: (l, 1, 0)),
                  pl.BlockSpec((None, HG_WIDTH, d), lambda b, i: (l, 1, 0)),
                  pl.BlockSpec((None, tm, d), lambda b, i: (b, i, 0)),
                  mod_row(2),
                  pl.BlockSpec((None, 1, d), lambda b, i: (l, 0, 0)),
                  mod_row(3), mod_row(4),
                  pl.BlockSpec((None, N_EXPERTS, d), lambda b, i: (l, 0, 0))],
        out_specs=[pl.BlockSpec((None, tm, d), lambda b, i: (b, i, 0)),
                   pl.BlockSpec((None, tm, d), lambda b, i: (b, i, 0)),
                   pl.BlockSpec((None, N_EXPERTS, tm), lambda b, i: (b, 0, i))],
        out_shape=[jax.ShapeDtypeStruct((bsz, seq, d), F32),
                   jax.ShapeDtypeStruct((bsz, seq, d), BF16),
                   jax.ShapeDtypeStruct((bsz, N_EXPERTS, seq), F32)],
        compiler_params=_params("arbitrary", "arbitrary"),
        name="out_route",
    )(y_fn, y_hy, y_hg, w_out_bf, w_out_bf, w_out_bf, x, mod4, gamma3, mod4, mod4, w_router_t)


def _topk_kernel(cap, aff_ref, tri_ref, pos_ref):
    a = aff_ref[...]

    def count(mask):
        return jnp.sum(jnp.where(mask, 1.0, 0.0), axis=1, keepdims=True)

    def as_float(bits):
        return pltpu.bitcast(jnp.broadcast_to(bits, a.shape), F32)

    def step(i, thr_bits):
        cand = thr_bits | jnp.left_shift(jnp.int32(1), 30 - i)
        return jnp.where(count(a >= as_float(cand)) >= cap, cand, thr_bits)

    thr = as_float(lax.fori_loop(0, 31, step, jnp.zeros((a.shape[0], 1), jnp.int32)))
    above = a > thr
    tie = a == thr
    room = cap - count(above)
    tie_rank = _dot(jnp.where(tie, 1.0, 0.0).astype(BF16), tri_ref[...])
    sel = jnp.where(above, 1.0, jnp.where(tie, jnp.where(tie_rank <= room, 1.0, 0.0), 0.0))
    slot = _dot(sel.astype(BF16), tri_ref[...]) - 1.0
    pos_ref[...] = jnp.where(sel > 0.5, slot, -1.0).astype(jnp.int32)


def _topk(aff, tri_incl, cap):
    bsz, ne, seq = aff.shape
    return pl.pallas_call(
        functools.partial(_topk_kernel, cap),
        grid=(bsz,),
        in_specs=[pl.BlockSpec((None, ne, seq), lambda b: (b, 0, 0)),
                  pl.BlockSpec((seq, seq), lambda b: (0, 0))],
        out_specs=pl.BlockSpec((None, ne, seq), lambda b: (b, 0, 0)),
        out_shape=jax.ShapeDtypeStruct((bsz, ne, seq), jnp.int32),
        compiler_params=_params("arbitrary"),
        name="moe_topk",
    )(aff, tri_incl)


def _gather_kernel(cap, xm_ref, pos_ref, o_ref):
    seq = xm_ref.shape[0]
    pos = pos_ref[pl.ds(pl.program_id(1), 1), :]
    slot = lax.broadcasted_iota(jnp.int32, (cap, seq), 0)
    onehot = jnp.where(slot == pos, 1.0, 0.0).astype(BF16)
    o_ref[...] = _dot(onehot, xm_ref[...]).astype(BF16)


def _gather(xm, pos, cap):
    bsz, seq, d = xm.shape
    return pl.pallas_call(
        functools.partial(_gather_kernel, cap),
        grid=(bsz, N_EXPERTS),
        in_specs=[pl.BlockSpec((None, seq, d), lambda b, e: (b, 0, 0)),
                  pl.BlockSpec((None, N_EXPERTS, seq), lambda b, e: (b, 0, 0))],
        out_specs=pl.BlockSpec((None, None, cap, d), lambda b, e: (e, b, 0, 0)),
        out_shape=jax.ShapeDtypeStruct((N_EXPERTS, bsz, cap, d), BF16),
        compiler_params=_params("arbitrary", "arbitrary"),
        name="moe_gather",
    )(xm, pos)


def _ffn_kernel(xs_ref, wg_ref, wu_ref, wd_ref, o_ref, acc_ref):
    j = pl.program_id(2)

    @pl.when((pl.program_id(0) == 0) & (pl.program_id(1) == 0) & (j == 0))
    def _():
        acc_ref[...] = jnp.zeros_like(acc_ref)

    xs = xs_ref[...]
    hid = _silu(_dot(xs, wg_ref[...].astype(BF16))) * _dot(xs, wu_ref[...].astype(BF16))
    part = _dot(hid.astype(BF16), wd_ref[...].astype(BF16))
    total = part + jnp.where(j > 0, acc_ref[...], 0.0)
    acc_ref[...] = total
    o_ref[...] = total.astype(BF16)


def _ffn(xs, w_gate, w_up, w_down, l):
    ne, rows, d = xs.shape
    ff = w_gate.shape[-1]
    tm = min(rows, 1024)
    tj = 512
    return pl.pallas_call(
        _ffn_kernel,
        grid=(ne, rows // tm, ff // tj),
        in_specs=[pl.BlockSpec((None, tm, d), lambda e, m, j: (e, m, 0)),
                  pl.BlockSpec((None, None, d, tj), lambda e, m, j: (l, e, 0, j)),
                  pl.BlockSpec((None, None, d, tj), lambda e, m, j: (l, e, 0, j)),
                  pl.BlockSpec((None, None, tj, d), lambda e, m, j: (l, e, j, 0))],
        out_specs=pl.BlockSpec((None, tm, d), lambda e, m, j: (e, m, 0)),
        out_shape=jax.ShapeDtypeStruct((ne, rows, d), BF16),
        scratch_shapes=[pltpu.VMEM((tm, d), F32)],
        compiler_params=_params("arbitrary", "arbitrary", "arbitrary"),
        name="moe_ffn",
    )(xs, w_gate, w_up, w_down)


def _combine_kernel(cap, final_norm, *refs):
    if final_norm:
        ys_ref, pos_ref, aff_ref, x_ref, g_ref, fg_ref, o_ref, pos_t_ref, gate_t_ref, w_ref = refs
    else:
        ys_ref, pos_ref, aff_ref, x_ref, g_ref, o_ref, pos_t_ref, gate_t_ref, w_ref = refs
    i = pl.program_id(1)
    tm = x_ref.shape[0]
    ne = pos_ref.shape[0]

    @pl.when(i == 0)
    def _():
        pos_t_ref[...] = pos_ref[...].astype(F32).T
        gate_t_ref[...] = aff_ref[...].T

    r0 = pl.multiple_of(i * tm, tm)
    pos = pos_t_ref[pl.ds(r0, tm), :]
    gate = gate_t_ref[pl.ds(r0, tm), :]
    if cap % 128 == 0:
        lane = lax.broadcasted_iota(jnp.int32, (tm, cap), 1).astype(F32)
        for e in range(ne):
            w_ref[:, e * cap:(e + 1) * cap] = jnp.where(lane == pos[:, e:e + 1], gate[:, e:e + 1], 0.0).astype(BF16)
    else:
        lane = lax.broadcasted_iota(jnp.int32, (tm, ne * cap), 1).astype(F32)
        w = jnp.zeros((tm, ne * cap), F32)
        for e in range(ne):
            hit = jnp.logical_and(lane == pos[:, e:e + 1] + float(e * cap), pos[:, e:e + 1] >= 0.0)
            w = w + jnp.where(hit, gate[:, e:e + 1], 0.0)
        w_ref[...] = w.astype(BF16)
    ys = ys_ref[...].reshape(ne * cap, ys_ref.shape[-1])
    x = x_ref[...] + g_ref[...] * _dot(w_ref[...], ys)
    if final_norm:
        ms = jnp.mean(x * x, axis=-1, keepdims=True)
        x = x * lax.rsqrt(ms + EPS) * fg_ref[...]
    o_ref[...] = x


def _combine(ys, pos, aff, x, mod4, mrow, cap, final_g=None):
    bsz, seq, d = x.shape
    ne = pos.shape[1]
    tm = min(seq, 256)
    final_norm = final_g is not None
    in_specs = [pl.BlockSpec((ne, None, cap, d), lambda b, i: (0, b, 0, 0)),
                pl.BlockSpec((None, ne, seq), lambda b, i: (b, 0, 0)),
                pl.BlockSpec((None, ne, seq), lambda b, i: (b, 0, 0)),
                pl.BlockSpec((None, tm, d), lambda b, i: (b, i, 0)),
                pl.BlockSpec((None, None, 1, d), lambda b, i: (mrow(b), 5, 0, 0))]
    args = (ys, pos, aff, x, mod4)
    if final_norm:
        in_specs.append(pl.BlockSpec((1, d), lambda b, i: (0, 0)))
        args += (final_g,)
    return pl.pallas_call(
        functools.partial(_combine_kernel, cap, final_norm),
        grid=(bsz, seq // tm),
        in_specs=in_specs,
        out_specs=pl.BlockSpec((None, tm, d), lambda b, i: (b, i, 0)),
        out_shape=jax.ShapeDtypeStruct((bsz, seq, d), F32),
        scratch_shapes=[pltpu.VMEM((seq, ne), F32), pltpu.VMEM((seq, ne), F32), pltpu.VMEM((tm, ne * cap), BF16)],
        compiler_params=_params("arbitrary", "arbitrary"),
        name="moe_combine",
    )(*args)


def _angles(row_ids, cols, n):
    c = lax.broadcasted_iota(jnp.int32, (row_ids.shape[0], cols), 1)
    return ((row_ids * c) % n).astype(F32) * (2.0 * math.pi / n)


def _cos_sin(rows, cols, n):
    step = min(rows, 32)
    hi = _angles(jnp.arange(0, rows, step, dtype=jnp.int32)[:, None], cols, n)[:, None, :]
    lo = _angles(jnp.arange(step, dtype=jnp.int32)[:, None], cols, n)[None, :, :]
    cos = jnp.cos(hi) * jnp.cos(lo) - jnp.sin(hi) * jnp.sin(lo)
    sin = jnp.sin(hi) * jnp.cos(lo) + jnp.cos(hi) * jnp.sin(lo)
    return cos.reshape(rows, cols), sin.reshape(rows, cols)


def _fourier_table(seq):
    cos, sin = _cos_sin(seq, seq, seq)
    return jnp.concatenate([cos, -sin], axis=1).astype(BF16)


def _channel_dft(seq):
    ang = _angles(jnp.arange(FN_GROUP, dtype=jnp.int32)[:, None], FN_GROUP, FN_GROUP)
    scale = 1.0 / math.sqrt(seq * FN_GROUP)
    eye = jnp.eye(FN_WIDTH // FN_GROUP, dtype=F32)
    return jnp.stack([jnp.kron(eye, jnp.cos(ang) * scale), jnp.kron(eye, jnp.sin(ang) * scale)])


def _hyena_dft(seq):
    cos, sin = _cos_sin(seq, seq, 2 * seq)
    r = lax.broadcasted_iota(jnp.int32, (seq, seq), 0)
    c = lax.broadcasted_iota(jnp.int32, (seq, seq), 1)
    dft = jnp.concatenate([cos, jnp.where(r == 0, (1 - 2 * (c % 2)).astype(F32), sin)], axis=0).astype(BF16)
    dft_t = jnp.concatenate([cos, jnp.where(c == 0, (1 - 2 * (r % 2)).astype(F32), sin)], axis=1).astype(BF16)
    return dft, dft_t


def _tri_incl(seq):
    r = lax.broadcasted_iota(jnp.int32, (seq, seq), 0)
    c = lax.broadcasted_iota(jnp.int32, (seq, seq), 1)
    return (r <= c).astype(BF16)


def _pad_to(a, shape):
    return jnp.pad(a, [(0, t - s) for s, t in zip(a.shape, shape)])


def kernel(x, c, ctx, c_ctx, norm_mix_g, norm_ffn_g, final_norm_g, w_mod, b_mod, w_in, w_out, w_fnet,
           hy_conv_w, hy_conv_b, hy_w1, hy_b1, hy_w2, hy_b2, hy_w3, hy_b3, hy_w_out, hy_freq, hy_bias,
           hg_lb, hg_norm_g, w_router, w_gate, w_up, w_down):
    bsz, seq, d = x.shape
    ctx_len = ctx.shape[1]
    depth = w_in.shape[0]

    p = jax.nn.softmax(hg_lb.astype(F32), axis=0)
    lbs4 = (jnp.cumsum(p, axis=0) - p[0:1]).reshape(depth, 2, 1, HG_WIDTH)
    w_in_bf = w_in.astype(BF16)
    w_out_bf = w_out.astype(BF16)
    w_router_t = jnp.swapaxes(w_router, 1, 2)
    g_mix3 = norm_mix_g.reshape(depth, 1, d)
    g_ffn3 = norm_ffn_g.reshape(depth, 1, d)
    gain3 = hg_norm_g.reshape(depth, 1, HG_WIDTH)
    b_mod3 = b_mod.reshape(depth, 1, 6 * d)
    conv_b3 = hy_conv_b.reshape(depth, 1, 3 * HY_WIDTH)
    hy_bias3 = hy_bias.reshape(depth, 1, HY_WIDTH)
    mlp = (_pad_to(hy_w1, (depth, HY_PAD, HY_PAD)), _pad_to(hy_b1.reshape(depth, 1, -1), (depth, 1, HY_PAD)),
           _pad_to(hy_w2, (depth, HY_PAD, HY_PAD)), _pad_to(hy_b2.reshape(depth, 1, -1), (depth, 1, HY_PAD)),
           _pad_to(hy_w3, (depth, HY_PAD, HY_PAD)), _pad_to(hy_b3.reshape(depth, 1, -1), (depth, 1, HY_PAD)),
           _pad_to(hy_w_out, (depth, HY_PAD, 2 * HY_WIDTH)), _pad_to(hy_freq, (depth, 8, HY_PAD)))
    max_decay = math.log(1e-2) / 0.3
    min_decay = math.log(1e-2) / 1.5
    deltas = jnp.linspace(min_decay, max_decay, HY_WIDTH, dtype=F32).reshape(1, HY_WIDTH)
    rows = 16
    cc = jnp.zeros((rows, d), F32).at[:bsz].set(c).at[bsz].set(c_ctx)

    tables = {}
    for n in {seq, ctx_len}:
        dft, dft_t = _hyena_dft(n)
        tables[n] = dict(fourier=_fourier_table(n), chan=_channel_dft(n), dft=dft, dft_t=dft_t, tri=_tri_incl(n))

    x_row = lambda b: b
    ctx_row = lambda b: bsz
    zero_state = jnp.zeros((bsz, HG_HEADS, HG_HEAD, HG_HEAD), F32)

    def mixers(hh, y_hg, n, l):
        t = tables[n]
        y_fn = _fourier(hh, _fn_prep(t["chan"], w_fnet, l), t["fourier"])
        kc, ks = _hy_filter(n, t["dft"], mlp, deltas, l)
        z, x0 = _hy_pre(hh, hy_conv_w, conv_b3, l)
        yc, ys = _hy_fwd(z, t["dft"], kc, ks)
        y_hy = _hy_inv(yc, ys, t["dft_t"], z, x0, hy_bias3, l)
        return y_fn, y_hy, y_hg

    def sublayers(xx, hh, y_hg, mod4, mrow, n, l, final_g=None):
        cap = EC_CAPACITY * n // N_EXPERTS
        xx, xm, aff = _out_route(xx, *mixers(hh, y_hg, n, l), w_out_bf, mod4, mrow, g_ffn3, w_router_t, l)
        pos = _topk(aff, tables[n]["tri"], cap)
        xs = _gather(xm, pos, cap)
        ys = _ffn(xs.reshape(N_EXPERTS, bsz * cap, d), w_gate, w_up, w_down, l)
        return _combine(ys.reshape(N_EXPERTS, bsz, cap, d), pos, aff, xx, mod4, mrow, cap, final_g)

    xc = ctx
    for l in range(depth):
        last = l == depth - 1
        mod4 = _modulation(cc, w_mod, b_mod3, l).reshape(rows, 6, 1, d)
        h = _in_proj(x, mod4, x_row, g_mix3, w_in_bf, l)
        hc = _in_proj(xc, mod4, ctx_row, g_mix3, w_in_bf, l)
        o_cf, s_f = _hgrn(hc, lbs4, gain3, zero_state, l, False)
        y_hg_c, s_b = _hgrn(hc, lbs4, gain3, zero_state, l, True, o_fwd=o_cf)
        o_xf, _ = _hgrn(h, lbs4, gain3, s_f, l, False)
        y_hg_x, _ = _hgrn(h, lbs4, gain3, s_b, l, True, o_fwd=o_xf)
        x = sublayers(x, h, y_hg_x, mod4, x_row, seq, l, final_norm_g.reshape(1, d) if last else None)
        if not last:
            xc = sublayers(xc, hc, y_hg_c, mod4, ctx_row, ctx_len, l)
    return x
```

```python
import functools
import math

import jax
import jax.numpy as jnp
from jax import lax
from jax.experimental import pallas as pl
from jax.experimental.pallas import tpu as pltpu

F32 = jnp.float32
BF16 = jnp.bfloat16

D_MODEL = 2048
FN_WIDTH = 512
FN_GROUP = 128
HY_WIDTH = 512
HG_WIDTH = 1024
HG_HEAD = 128
HG_HEADS = HG_WIDTH // HG_HEAD
HG_F_MIN = 1e-6
IN_WIDTH = FN_WIDTH + 3 * HY_WIDTH + 5 * HG_WIDTH
HY_BANDS = 16
HY_PAD = 128
N_EXPERTS = 16
EC_CAPACITY = 2
EXPERT_FF = 1024
EPS = 1e-6

HG_CHUNK = 128
HG_BASE = 8
HG_FAST_MAX_LOG2 = 115.0
LOG2_E = math.log2(math.e)
VMEM_LIMIT = 56 * 1024 * 1024

_COL_Q, _COL_FF, _COL_FB, _COL_I, _COL_G = 2, 3, 4, 5, 6


def _params(*sem):
    return pltpu.CompilerParams(dimension_semantics=sem, vmem_limit_bytes=VMEM_LIMIT)


def _dot(a, b):
    return jnp.dot(a, b, preferred_element_type=F32)


def _dot_nt(a, b):
    return lax.dot_general(a, b, (((1,), (1,)), ((), ())), preferred_element_type=F32)


def _split2(x):
    hi = x.astype(BF16)
    lo = (x - hi.astype(F32)).astype(BF16)
    return hi, lo


def _dot3(a, b, dot=_dot):
    ah, al = _split2(a)
    bh, bl = _split2(b)
    return dot(ah, bh) + dot(ah, bl) + dot(al, bh)


def _silu(x):
    return x * jax.nn.sigmoid(x)


def _norm_mod(x, g, sh, sc):
    ms = jnp.mean(x * x, axis=-1, keepdims=True)
    return (x * lax.rsqrt(ms + EPS) * g) * (1.0 + sc) + sh


def _mod_kernel(a_ref, w_ref, b_ref, o_ref):
    a = _silu(a_ref[...]).astype(BF16)
    o_ref[...] = _dot(a, w_ref[...].astype(BF16)) + b_ref[...]


def _modulation(cc, w_mod, b_mod3, l):
    rows, d = cc.shape
    n = w_mod.shape[-1]
    tn = 1024
    return pl.pallas_call(
        _mod_kernel,
        grid=(n // tn,),
        in_specs=[
            pl.BlockSpec((rows, d), lambda j: (0, 0)),
            pl.BlockSpec((None, d, tn), lambda j: (l, 0, j)),
            pl.BlockSpec((None, 1, tn), lambda j: (l, 0, j)),
        ],
        out_specs=pl.BlockSpec((rows, tn), lambda j: (0, j)),
        out_shape=jax.ShapeDtypeStruct((rows, n), F32),
        compiler_params=_params("arbitrary"),
        name="modulation",
    )(cc, w_mod, b_mod3)


def _in_kernel(x_ref, g_ref, sh_ref, sc_ref, w_ref, o_ref, xm_ref):
    @pl.when(pl.program_id(2) == 0)
    def _():
        xm_ref[...] = _norm_mod(x_ref[...], g_ref[...], sh_ref[...], sc_ref[...]).astype(BF16)

    o_ref[...] = _dot(xm_ref[...], w_ref[...]).astype(o_ref.dtype)


def _in_proj(x, mod4, mrow, gamma3, w_in_bf, l):
    bsz, seq, d = x.shape
    n = w_in_bf.shape[-1]
    tm = min(seq, 1024)
    tn = 1024
    return pl.pallas_call(
        _in_kernel,
        grid=(bsz, seq // tm, n // tn),
        in_specs=[
            pl.BlockSpec((None, tm, d), lambda b, i, j: (b, i, 0)),
            pl.BlockSpec((None, 1, d), lambda b, i, j: (l, 0, 0)),
            pl.BlockSpec((None, None, 1, d), lambda b, i, j: (mrow(b), 0, 0, 0)),
            pl.BlockSpec((None, None, 1, d), lambda b, i, j: (mrow(b), 1, 0, 0)),
            pl.BlockSpec((None, d, tn), lambda b, i, j: (l, 0, j)),
        ],
        out_specs=pl.BlockSpec((None, tm, tn), lambda b, i, j: (b, i, j)),
        out_shape=jax.ShapeDtypeStruct((bsz, seq, n), BF16),
        scratch_shapes=[pltpu.VMEM((tm, d), BF16)],
        compiler_params=_params("arbitrary", "arbitrary", "arbitrary"),
        name="in_proj",
    )(x, gamma3, mod4, mod4, w_in_bf)


def _hg_kernel(rev, fuse_out, *refs):
    if fuse_out:
        (zq_ref, zf_ref, zi_ref, g_ref, of_ref, lb_ref, gain_ref, s0_ref, y_ref, st_ref,
         s_ref, qh_ref, kh_ref, oi_ref) = refs
    else:
        (zq_ref, zf_ref, zi_ref, lb_ref, s0_ref, y_ref, st_ref,
         s_ref, qh_ref, kh_ref, oi_ref) = refs
    C = HG_CHUNK
    W = HG_WIDTH
    c = pl.program_id(1)

    @pl.when(c == 0)
    def _():
        s_ref[...] = s0_ref[...]

    row = lax.broadcasted_iota(jnp.int32, (C, C), 0)
    col = lax.broadcasted_iota(jnp.int32, (C, C), 1)
    if rev:
        row, col = C - 1 - row, C - 1 - col
    tri = jnp.where(col <= row, 1.0, 0.0)

    def ref_rows(b, s, r):
        nb = C // s
        b3 = b.reshape(nb, s, W)
        return jnp.broadcast_to(b3[:, r:r + 1, :], (nb, s, W)).reshape(C, W)

    def mid_offset(b, s):
        hh = s // 2
        return b - ref_rows(b, s, hh if rev else hh - 1)

    zf = zf_ref[...].astype(F32)
    lb = lb_ref[...]
    q = _silu(zq_ref[...])
    f = jnp.maximum(lb + (1.0 - lb) * jax.nn.sigmoid(zf), HG_F_MIN)
    kk = (1.0 - f).astype(BF16)
    hi, lo = _split2(jnp.log(f) * LOG2_E)
    tri_bf = tri.astype(BF16)
    b = _dot(tri_bf, hi) + _dot(tri_bf, lo)
    btot = b[0:1, :] if rev else b[C - 1:C, :]
    qe = q * jnp.exp2(b).astype(BF16)
    kd = kk * jnp.exp2(btot - b).astype(BF16)
    sdec = jnp.exp2(btot)
    v_t = zi_ref[...].T
    for h in range(HG_HEADS):
        hs = slice(h * HG_HEAD, (h + 1) * HG_HEAD)
        st = s_ref[h]
        oi_ref[:, hs] = _dot_nt(qe[:, hs], st.astype(BF16))
        s_ref[h] = st * sdec[:, hs] + _dot(v_t[hs, :], kd[:, hs])

    def intra_chunk(halvings, block):
        for i, s in enumerate(halvings):
            w = jnp.exp2(-jnp.abs(mid_offset(b, s))).astype(BF16)
            qh_ref[i] = q * w
            kh_ref[i] = kk * w
        d = mid_offset(b, block)
        nh = len(halvings)
        qh_ref[nh] = q * jnp.exp2(d).astype(BF16)
        kh_ref[nh] = kk * jnp.exp2(-d).astype(BF16)
        level_mask = [jnp.where(((row // s) == (col // s)) & ((row % s) >= s // 2) & ((col % s) < s // 2), 1.0, 0.0)
                      for s in halvings]
        block_mask = jnp.where((row // block) == (col // block), tri, 0.0)

        heads = [slice(h * HG_HEAD, (h + 1) * HG_HEAD) for h in range(HG_HEADS)]
        scores = [[_dot_nt(qh_ref[i, :, hs], kh_ref[i, :, hs]) for i in range(nh + 1)] for hs in heads]
        for hs, m in zip(heads, scores):
            a = jnp.where(block_mask > 0.5, m[nh], 0.0)
            for i in range(nh):
                a = a + m[i] * level_mask[i]
            o = oi_ref[:, hs] + _dot(a.astype(BF16), zi_ref[:, hs])
            if fuse_out:
                o = o + of_ref[:, hs]
                ms = jnp.mean(o * o, axis=-1, keepdims=True)
                g = g_ref[:, hs].astype(F32)
                y_ref[:, hs] = (o * lax.rsqrt(ms + EPS) * gain_ref[:, hs] * _silu(g)).astype(y_ref.dtype)
            else:
                y_ref[:, hs] = o

    half = C // 2
    in_range = jnp.max(jnp.abs(mid_offset(b, half))) <= HG_FAST_MAX_LOG2

    @pl.when(in_range)
    def _():
        intra_chunk([C], half)

    @pl.when(jnp.logical_not(in_range))
    def _():
        sizes = []
        s = C
        while s > HG_BASE:
            sizes.append(s)
            s //= 2
        intra_chunk(sizes, HG_BASE)

    @pl.when(c == pl.num_programs(1) - 1)
    def _():
        st_ref[...] = s_ref[...]


def _hgrn(h, lbs4, gain3, s0, l, rev, o_fwd=None):
    bsz, seq, _ = h.shape
    C = HG_CHUNK
    nc = seq // C
    W = HG_WIDTH
    fuse_out = o_fwd is not None
    n_factor = (C // HG_BASE).bit_length()
    cidx = (lambda c: nc - 1 - c) if rev else (lambda c: c)

    def hcol(k):
        return pl.BlockSpec((None, C, W), lambda b, c: (b, cidx(c), k))

    lb_spec = pl.BlockSpec((None, None, 1, W), lambda b, c: (l, 1 if rev else 0, 0, 0))
    s_spec = pl.BlockSpec((None, HG_HEADS, HG_HEAD, HG_HEAD), lambda b, c: (b, 0, 0, 0))
    o_spec = pl.BlockSpec((None, C, W), lambda b, c: (b, cidx(c), 0))
    if fuse_out:
        in_specs = [hcol(_COL_Q), hcol(_COL_FB if rev else _COL_FF), hcol(_COL_I), hcol(_COL_G), o_spec,
                    lb_spec, pl.BlockSpec((None, 1, W), lambda b, c: (l, 0, 0)), s_spec]
        args = (h, h, h, h, o_fwd, lbs4, gain3, s0)
        out_dtype = BF16
    else:
        in_specs = [hcol(_COL_Q), hcol(_COL_FB if rev else _COL_FF), hcol(_COL_I), lb_spec, s_spec]
        args = (h, h, h, lbs4, s0)
        out_dtype = F32
    return pl.pallas_call(
        functools.partial(_hg_kernel, rev, fuse_out),
        grid=(bsz, nc),
        in_specs=in_specs,
        out_specs=[o_spec, s_spec],
        out_shape=[jax.ShapeDtypeStruct((bsz, seq, W), out_dtype),
                   jax.ShapeDtypeStruct((bsz, HG_HEADS, HG_HEAD, HG_HEAD), F32)],
        scratch_shapes=[pltpu.VMEM((HG_HEADS, HG_HEAD, HG_HEAD), F32),
                        pltpu.VMEM((n_factor, C, W), BF16), pltpu.VMEM((n_factor, C, W), BF16),
                        pltpu.VMEM((C, W), F32)],
        compiler_params=_params("arbitrary", "arbitrary"),
        name="hgrn_bwd" if rev else "hgrn_fwd",
    )(*args)


def _fn_prep_kernel(cs_ref, w_ref, o_ref):
    w = w_ref[...]
    o_ref[:, :FN_WIDTH] = _dot3(cs_ref[0], w).astype(BF16)
    o_ref[:, FN_WIDTH:] = _dot3(cs_ref[1], w).astype(BF16)


def _fn_prep(chan_dft, w_fnet, l):
    return pl.pallas_call(
        _fn_prep_kernel,
        grid=(1,),
        in_specs=[pl.BlockSpec((2, FN_WIDTH, FN_WIDTH), lambda i: (0, 0, 0)),
                  pl.BlockSpec((None, FN_WIDTH, FN_WIDTH), lambda i: (l, 0, 0))],
        out_specs=pl.BlockSpec((FN_WIDTH, 2 * FN_WIDTH), lambda i: (0, 0)),
        out_shape=jax.ShapeDtypeStruct((FN_WIDTH, 2 * FN_WIDTH), BF16),
        compiler_params=_params("arbitrary"),
        name="fnet_prep",
    )(chan_dft, w_fnet)


def _fn_kernel(u_ref, wc_ref, t_ref, o_ref, p_ref):
    seq = u_ref.shape[0]
    rc = min(seq, 512)

    @pl.when(pl.program_id(1) == 0)
    def _():
        def rows(i, carry):
            r0 = pl.multiple_of(i * rc, rc)
            p = _dot(u_ref[pl.ds(r0, rc), :], wc_ref[...])
            p_ref[pl.ds(r0, rc), :] = p[:, :FN_WIDTH].astype(BF16)
            p_ref[pl.ds(pl.multiple_of(seq + r0, rc), rc), :] = p[:, FN_WIDTH:].astype(BF16)
            return carry

        lax.fori_loop(0, seq // rc, rows, 0)

    o_ref[...] = _dot(t_ref[...], p_ref[...]).astype(o_ref.dtype)


def _fourier(h, wc, table):
    bsz, seq, _ = h.shape
    tt = min(seq, 512)
    return pl.pallas_call(
        _fn_kernel,
        grid=(bsz, seq // tt),
        in_specs=[pl.BlockSpec((None, seq, FN_WIDTH), lambda b, t: (b, 0, 0)),
                  pl.BlockSpec((FN_WIDTH, 2 * FN_WIDTH), lambda b, t: (0, 0)),
                  pl.BlockSpec((tt, 2 * seq), lambda b, t: (t, 0))],
        out_specs=pl.BlockSpec((None, tt, FN_WIDTH), lambda b, t: (b, t, 0)),
        out_shape=jax.ShapeDtypeStruct((bsz, seq, FN_WIDTH), BF16),
        scratch_shapes=[pltpu.VMEM((2 * seq, FN_WIDTH), BF16)],
        compiler_params=_params("arbitrary", "arbitrary"),
        name="fourier",
    )(h, wc, table)


def _hy_filter_kernel(seq, wc_ref, ws_ref, w1_ref, b1_ref, w2_ref, b2_ref, w3_ref, b3_ref, wo_ref, fr_ref,
                      dl_ref, kc_ref, ks_ref, h_ref):
    i = pl.program_id(0)
    tf = wc_ref.shape[0]
    nfft = 2 * seq

    @pl.when(i == 0)
    def _():
        pos = lax.broadcasted_iota(jnp.int32, (seq, HY_PAD), 0).astype(F32)
        lane = lax.broadcasted_iota(jnp.int32, (seq, HY_PAD), 1)
        t = pos / float(max(seq - 1, 1))
        w = (2.0 * math.pi) * pos / float(seq)
        band_id = jnp.where(lane <= HY_BANDS, lane - 1, lane - 1 - HY_BANDS).astype(F32)
        band = 1e-4 + band_id * ((HY_BANDS - 1 - 1e-4) / (HY_BANDS - 1))
        arg = band * w
        z = jnp.where(lane == 0, t,
                      jnp.where(lane <= HY_BANDS, jnp.cos(arg),
                                jnp.where(lane <= 2 * HY_BANDS, -jnp.sin(arg), 0.0)))
        fr = fr_ref[...]
        hdn = jnp.sin(fr[0:1] * (_dot3(z, w1_ref[...]) + b1_ref[...]))
        hdn = jnp.sin(fr[1:2] * (_dot3(hdn, w2_ref[...]) + b2_ref[...]))
        hdn = jnp.sin(fr[2:3] * (_dot3(hdn, w3_ref[...]) + b3_ref[...]))
        hf = _dot3(hdn, wo_ref[...])
        decay = jnp.exp(-t[:, 0:1] * jnp.abs(dl_ref[...]))
        first = lax.broadcasted_iota(jnp.int32, (seq, HY_WIDTH), 0) == 0
        h_ref[:, :HY_WIDTH] = (hf[:, :HY_WIDTH] * decay).astype(BF16)
        h_ref[:, HY_WIDTH:] = jnp.where(first, 0.0, hf[:, HY_WIDTH:] * decay).astype(BF16)

    gc = _dot(wc_ref[...], h_ref[...])
    gs = _dot(ws_ref[...], h_ref[...])
    first = (lax.broadcasted_iota(jnp.int32, (tf, HY_WIDTH), 0) + i * tf) == 0
    scale = jnp.where(first, 1.0 / nfft, 2.0 / nfft)
    kc_ref[...] = (gc[:, :HY_WIDTH] + gc[:, HY_WIDTH:]) * scale
    ks_ref[...] = jnp.where(first, gs[:, :HY_WIDTH] + gs[:, HY_WIDTH:], gs[:, :HY_WIDTH] - gs[:, HY_WIDTH:]) * scale


def _hy_filter(seq, dft, mlp, deltas, l):
    w1p, b1p, w2p, b2p, w3p, b3p, wop, frp = mlp
    tf = min(seq, 512)
    nf = seq // tf

    def full(a):
        shp = a.shape[1:]
        return pl.BlockSpec((None,) + shp, lambda i: (l,) + (0,) * len(shp))

    return pl.pallas_call(
        functools.partial(_hy_filter_kernel, seq),
        grid=(nf,),
        in_specs=[pl.BlockSpec((tf, seq), lambda i: (i, 0)),
                  pl.BlockSpec((tf, seq), lambda i: (i + nf, 0)),
                  full(w1p), full(b1p), full(w2p), full(b2p), full(w3p), full(b3p), full(wop), full(frp),
                  pl.BlockSpec((1, HY_WIDTH), lambda i: (0, 0))],
        out_specs=[pl.BlockSpec((tf, HY_WIDTH), lambda i: (i, 0)),
                   pl.BlockSpec((tf, HY_WIDTH), lambda i: (i, 0))],
        out_shape=[jax.ShapeDtypeStruct((seq, HY_WIDTH), F32), jax.ShapeDtypeStruct((seq, HY_WIDTH), F32)],
        scratch_shapes=[pltpu.VMEM((seq, 2 * HY_WIDTH), BF16)],
        compiler_params=_params("arbitrary"),
        name="hyena_filter",
    )(dft, dft, w1p, b1p, w2p, b2p, w3p, b3p, wop, frp, deltas)


def _hy_pre_kernel(uv_ref, u1_ref, u0_ref, cw_ref, cb_ref, z_ref, x0_ref):
    seq = uv_ref.shape[0]
    rowi = lax.broadcasted_iota(jnp.int32, (seq, 128), 0)
    j = pl.program_id(1)

    def conv(u_ref, part):
        u = u_ref[...].astype(F32)
        prev = jnp.where(rowi == 0, 0.0, pltpu.roll(u, 1, 0))
        nxt = jnp.where(rowi == seq - 1, 0.0, pltpu.roll(u, seq - 1, 0))
        cs = pl.ds(pl.multiple_of(part * HY_WIDTH + j * 128, 128), 128)
        return prev * cw_ref[0:1, cs] + u * cw_ref[1:2, cs] + nxt * cw_ref[2:3, cs] + cb_ref[:, cs]

    z_ref[...] = (conv(u1_ref, 1) * conv(uv_ref, 0)).astype(BF16)
    x0_ref[...] = conv(u0_ref, 2).astype(BF16)


def _hy_pre(h, conv_w, conv_b3, l):
    bsz, seq, _ = h.shape
    lanes = 128
    nj = HY_WIDTH // lanes
    off = FN_WIDTH // lanes

    def part(p):
        return pl.BlockSpec((None, seq, lanes), lambda b, j: (b, 0, off + p * nj + j))

    o_spec = pl.BlockSpec((None, seq, lanes), lambda b, j: (b, 0, j))
    return pl.pallas_call(
        _hy_pre_kernel,
        grid=(bsz, nj),
        in_specs=[part(0), part(1), part(2),
                  pl.BlockSpec((None, 3, 3 * HY_WIDTH), lambda b, j: (l, 0, 0)),
                  pl.BlockSpec((None, 1, 3 * HY_WIDTH), lambda b, j: (l, 0, 0))],
        out_specs=[o_spec, o_spec],
        out_shape=[jax.ShapeDtypeStruct((bsz, seq, HY_WIDTH), BF16)] * 2,
        compiler_params=_params("arbitrary", "arbitrary"),
        name="hyena_pre",
    )(h, h, h, conv_w, conv_b3)


def _hy_fwd_kernel(wc_ref, ws_ref, z_ref, kc_ref, ks_ref, yc_ref, ys_ref):
    tf = wc_ref.shape[0]
    z = z_ref[...]
    uc = _dot(wc_ref[...], z)
    us = _dot(ws_ref[...], z)
    kc = kc_ref[...]
    ks = ks_ref[...]
    first = (lax.broadcasted_iota(jnp.int32, (tf, HY_WIDTH), 0) + pl.program_id(0) * tf) == 0
    ss = us * ks
    yc_ref[...] = (uc * kc - jnp.where(first, 0.0, ss)).astype(BF16)
    ys_ref[...] = jnp.where(first, ss, uc * ks + us * kc).astype(BF16)


def _hy_fwd(z, dft, kc, ks):
    bsz, seq, _ = z.shape
    tf = min(seq, 512)
    nf = seq // tf
    k_spec = pl.BlockSpec((tf, HY_WIDTH), lambda i, b: (i, 0))
    y_spec = pl.BlockSpec((None, tf, HY_WIDTH), lambda i, b: (b, i, 0))
    return pl.pallas_call(
        _hy_fwd_kernel,
        grid=(nf, bsz),
        in_specs=[pl.BlockSpec((tf, seq), lambda i, b: (i, 0)),
                  pl.BlockSpec((tf, seq), lambda i, b: (i + nf, 0)),
                  pl.BlockSpec((None, seq, HY_WIDTH), lambda i, b: (b, 0, 0)),
                  k_spec, k_spec],
        out_specs=[y_spec, y_spec],
        out_shape=[jax.ShapeDtypeStruct((bsz, seq, HY_WIDTH), BF16)] * 2,
        compiler_params=_params("arbitrary", "arbitrary"),
        name="hyena_dft",
    )(dft, dft, z, kc, ks)


def _hy_inv_kernel(tc_ref, ts_ref, yc_ref, ys_ref, z_ref, x0_ref, db_ref, o_ref):
    y = _dot(tc_ref[...], yc_ref[...]) + _dot(ts_ref[...], ys_ref[...])
    z = z_ref[...].astype(F32)
    o_ref[...] = (x0_ref[...].astype(F32) * (y + z * db_ref[...])).astype(BF16)


def _hy_inv(yc, ys, dft_t, z, x0, hy_bias3, l):
    bsz, seq, _ = z.shape
    tt = min(seq, 512)
    y_spec = pl.BlockSpec((None, seq, HY_WIDTH), lambda t, b: (b, 0, 0))
    r_spec = pl.BlockSpec((None, tt, HY_WIDTH), lambda t, b: (b, t, 0))
    return pl.pallas_call(
        _hy_inv_kernel,
        grid=(seq // tt, bsz),
        in_specs=[pl.BlockSpec((tt, seq), lambda t, b: (t, 0)),
                  pl.BlockSpec((tt, seq), lambda t, b: (t, 1)),
                  y_spec, y_spec, r_spec, r_spec,
                  pl.BlockSpec((None, 1, HY_WIDTH), lambda t, b: (l, 0, 0))],
        out_specs=r_spec,
        out_shape=jax.ShapeDtypeStruct((bsz, seq, HY_WIDTH), BF16),
        compiler_params=_params("arbitrary", "arbitrary"),
        name="hyena_idft",
    )(dft_t, dft_t, yc, ys, z, x0, hy_bias3)


def _out_route_kernel(yf_ref, yh_ref, yg_ref, wf_ref, wh_ref, wg_ref, x_ref, g1_ref, gam_ref, sh_ref, sc_ref,
                      wr_ref, xo_ref, xm_ref, aff_ref):
    mix = _dot(yf_ref[...], wf_ref[...]) + _dot(yh_ref[...], wh_ref[...]) + _dot(yg_ref[...], wg_ref[...])
    x = x_ref[...] + g1_ref[...] * mix
    xo_ref[...] = x
    xm = _norm_mod(x, gam_ref[...], sh_ref[...], sc_ref[...])
    xm_ref[...] = xm.astype(BF16)
    logits = _dot3(wr_ref[...], xm, dot=_dot_nt)
    mx = jnp.max(logits, axis=0, keepdims=True)
    ex = jnp.exp(logits - mx)
    aff_ref[...] = ex / jnp.sum(ex, axis=0, keepdims=True)


def _out_route(x, y_fn, y_hy, y_hg, w_out_bf, mod4, mrow, gamma3, w_router_t, l):
    bsz, seq, d = x.shape
    tm = min(seq, 512)
    half = FN_WIDTH

    def mod_row(k):
        return pl.BlockSpec((None, None, 1, d), lambda b, i: (mrow(b), k, 0, 0))

    return pl.pallas_call(
        _out_route_kernel,
        grid=(bsz, seq // tm),
        in_specs=[pl.BlockSpec((None, tm, half), lambda b, i: (b, i, 0)),
                  pl.BlockSpec((None, tm, half), lambda b, i: (b, i, 0)),
                  pl.BlockSpec((None, tm, HG_WIDTH), lambda b, i: (b, i, 0)),
                  pl.BlockSpec((None, half, d), lambda b, i: (l, 0, 0)),
                  pl.BlockSpec((None, half, d), lambda b, i: (l, 1, 0)),
                  pl.BlockSpec((None, HG_WIDTH, d), lambda b, i: (l, 1, 0)),
                  pl.BlockSpec((None, tm, d), lambda b, i: (b, i, 0)),
                  mod_row(2),
                  pl.BlockSpec((None, 1, d), lambda b, i: (l, 0, 0)),
                  mod_row(3), mod_row(4),
                  pl.BlockSpec((None, N_EXPERTS, d), lambda b, i: (l, 0, 0))],
        out_specs=[pl.BlockSpec((None, tm, d), lambda b, i: (b, i, 0)),
                   pl.BlockSpec((None, tm, d), lambda b, i: (b, i, 0)),
                   pl.BlockSpec((None, N_EXPERTS, tm), lambda b, i: (b, 0, i))],
        out_shape=[jax.ShapeDtypeStruct((bsz, seq, d), F32),
                   jax.ShapeDtypeStruct((bsz, seq, d), BF16),
                   jax.ShapeDtypeStruct((bsz, N_EXPERTS, seq), F32)],
        compiler_params=_params("arbitrary", "arbitrary"),
        name="out_route",
    )(y_fn, y_hy, y_hg, w_out_bf, w_out_bf, w_out_bf, x, mod4, gamma3, mod4, mod4, w_router_t)


def _topk_kernel(cap, aff_ref, tri_ref, pos_ref):
    a = aff_ref[...]

    def count(mask):
        return jnp.sum(jnp.where(mask, 1.0, 0.0), axis=1, keepdims=True)

    def as_float(bits):
        return pltpu.bitcast(jnp.broadcast_to(bits, a.shape), F32)

    def step(i, thr_bits):
        cand = thr_bits | jnp.left_shift(jnp.int32(1), 30 - i)
        return jnp.where(count(a >= as_float(cand)) >= cap, cand, thr_bits)

    thr = as_float(lax.fori_loop(0, 31, step, jnp.zeros((a.shape[0], 1), jnp.int32)))
    above = a > thr
    tie = a == thr
    room = cap - count(above)
    tie_rank = _dot(jnp.where(tie, 1.0, 0.0).astype(BF16), tri_ref[...])
    sel = jnp.where(above, 1.0, jnp.where(tie, jnp.where(tie_rank <= room, 1.0, 0.0), 0.0))
    slot = _dot(sel.astype(BF16), tri_ref[...]) - 1.0
    pos_ref[...] = jnp.where(sel > 0.5, slot, -1.0).astype(jnp.int32)


def _topk(aff, tri_incl, cap):
    bsz, ne, seq = aff.shape
    return pl.pallas_call(
        functools.partial(_topk_kernel, cap),
        grid=(bsz,),
        in_specs=[pl.BlockSpec((None, ne, seq), lambda b: (b, 0, 0)),
                  pl.BlockSpec((seq, seq), lambda b: (0, 0))],
        out_specs=pl.BlockSpec((None, ne, seq), lambda b: (b, 0, 0)),
        out_shape=jax.ShapeDtypeStruct((bsz, ne, seq), jnp.int32),
        compiler_params=_params("arbitrary"),
        name="moe_topk",
    )(aff, tri_incl)


def _gather_kernel(cap, xm_ref, pos_ref, o_ref):
    seq = xm_ref.shape[0]
    pos = pos_ref[pl.ds(pl.program_id(1), 1), :]
    slot = lax.broadcasted_iota(jnp.int32, (cap, seq), 0)
    onehot = jnp.where(slot == pos, 1.0, 0.0).astype(BF16)
    o_ref[...] = _dot(onehot, xm_ref[...]).astype(BF16)


def _gather(xm, pos, cap):
    bsz, seq, d = xm.shape
    return pl.pallas_call(
        functools.partial(_gather_kernel, cap),
        grid=(bsz, N_EXPERTS),
        in_specs=[pl.BlockSpec((None, seq, d), lambda b, e: (b, 0, 0)),
                  pl.BlockSpec((None, N_EXPERTS, seq), lambda b, e: (b, 0, 0))],
        out_specs=pl.BlockSpec((None, None, cap, d), lambda b, e: (e, b, 0, 0)),
        out_shape=jax.ShapeDtypeStruct((N_EXPERTS, bsz, cap, d), BF16),
        compiler_params=_params("arbitrary", "arbitrary"),
        name="moe_gather",
    )(xm, pos)


def _ffn_kernel(xs_ref, wg_ref, wu_ref, wd_ref, o_ref, acc_ref):
    j = pl.program_id(2)

    @pl.when((pl.program_id(0) == 0) & (pl.program_id(1) == 0) & (j == 0))
    def _():
        acc_ref[...] = jnp.zeros_like(acc_ref)

    xs = xs_ref[...]
    hid = _silu(_dot(xs, wg_ref[...].astype(BF16))) * _dot(xs, wu_ref[...].astype(BF16))
    part = _dot(hid.astype(BF16), wd_ref[...].astype(BF16))
    total = part + jnp.where(j > 0, acc_ref[...], 0.0)
    acc_ref[...] = total
    o_ref[...] = total.astype(BF16)


def _ffn(xs, w_gate, w_up, w_down, l):
    ne, rows, d = xs.shape
    ff = w_gate.shape[-1]
    tm = min(rows, 1024)
    tj = 512
    return pl.pallas_call(
        _ffn_kernel,
        grid=(ne, rows // tm, ff // tj),
        in_specs=[pl.BlockSpec((None, tm, d), lambda e, m, j: (e, m, 0)),
                  pl.BlockSpec((None, None, d, tj), lambda e, m, j: (l, e, 0, j)),
                  pl.BlockSpec((None, None, d, tj), lambda e, m, j: (l, e, 0, j)),
                  pl.BlockSpec((None, None, tj, d), lambda e, m, j: (l, e, j, 0))],
        out_specs=pl.BlockSpec((None, tm, d), lambda e, m, j: (e, m, 0)),
        out_shape=jax.ShapeDtypeStruct((ne, rows, d), BF16),
        scratch_shapes=[pltpu.VMEM((tm, d), F32)],
        compiler_params=_params("arbitrary", "arbitrary", "arbitrary"),
        name="moe_ffn",
    )(xs, w_gate, w_up, w_down)


def _combine_kernel(cap, final_norm, *refs):
    if final_norm:
        ys_ref, pos_ref, aff_ref, x_ref, g_ref, fg_ref, o_ref, pos_t_ref, gate_t_ref, w_ref = refs
    else:
        ys_ref, pos_ref, aff_ref, x_ref, g_ref, o_ref, pos_t_ref, gate_t_ref, w_ref = refs
    i = pl.program_id(1)
    tm = x_ref.shape[0]
    ne = pos_ref.shape[0]

    @pl.when(i == 0)
    def _():
        pos_t_ref[...] = pos_ref[...].astype(F32).T
        gate_t_ref[...] = aff_ref[...].T

    r0 = pl.multiple_of(i * tm, tm)
    pos = pos_t_ref[pl.ds(r0, tm), :]
    gate = gate_t_ref[pl.ds(r0, tm), :]
    if cap % 128 == 0:
        lane = lax.broadcasted_iota(jnp.int32, (tm, cap), 1).astype(F32)
        for e in range(ne):
            w_ref[:, e * cap:(e + 1) * cap] = jnp.where(lane == pos[:, e:e + 1], gate[:, e:e + 1], 0.0).astype(BF16)
    else:
        lane = lax.broadcasted_iota(jnp.int32, (tm, ne * cap), 1).astype(F32)
        w = jnp.zeros((tm, ne * cap), F32)
        for e in range(ne):
            hit = jnp.logical_and(lane == pos[:, e:e + 1] + float(e * cap), pos[:, e:e + 1] >= 0.0)
            w = w + jnp.where(hit, gate[:, e:e + 1], 0.0)
        w_ref[...] = w.astype(BF16)
    ys = ys_ref[...].reshape(ne * cap, ys_ref.shape[-1])
    x = x_ref[...] + g_ref[...] * _dot(w_ref[...], ys)
    if final_norm:
        ms = jnp.mean(x * x, axis=-1, keepdims=True)
        x = x * lax.rsqrt(ms + EPS) * fg_ref[...]
    o_ref[...] = x


def _combine(ys, pos, aff, x, mod4, mrow, cap, final_g=None):
    bsz, seq, d = x.shape
    ne = pos.shape[1]
    tm = min(seq, 256)
    final_norm = final_g is not None
    in_specs = [pl.BlockSpec((ne, None, cap, d), lambda b, i: (0, b, 0, 0)),
                pl.BlockSpec((None, ne, seq), lambda b, i: (b, 0, 0)),
                pl.BlockSpec((None, ne, seq), lambda b, i: (b, 0, 0)),
                pl.BlockSpec((None, tm, d), lambda b, i: (b, i, 0)),
                pl.BlockSpec((None, None, 1, d), lambda b, i: (mrow(b), 5, 0, 0))]
    args = (ys, pos, aff, x, mod4)
    if final_norm:
        in_specs.append(pl.BlockSpec((1, d), lambda b, i: (0, 0)))
        args += (final_g,)
    return pl.pallas_call(
        functools.partial(_combine_kernel, cap, final_norm),
        grid=(bsz, seq // tm),
        in_specs=in_specs,
        out_specs=pl.BlockSpec((None, tm, d), lambda b, i: (b, i, 0)),
        out_shape=jax.ShapeDtypeStruct((bsz, seq, d), F32),
        scratch_shapes=[pltpu.VMEM((seq, ne), F32), pltpu.VMEM((seq, ne), F32), pltpu.VMEM((tm, ne * cap), BF16)],
        compiler_params=_params("arbitrary", "arbitrary"),
        name="moe_combine",
    )(*args)


def _angles(row_ids, cols, n):
    c = lax.broadcasted_iota(jnp.int32, (row_ids.shape[0], cols), 1)
    return ((row_ids * c) % n).astype(F32) * (2.0 * math.pi / n)


def _cos_sin(rows, cols, n):
    step = min(rows, 32)
    hi = _angles(jnp.arange(0, rows, step, dtype=jnp.int32)[:, None], cols, n)[:, None, :]
    lo = _angles(jnp.arange(step, dtype=jnp.int32)[:, None], cols, n)[None, :, :]
    cos = jnp.cos(hi) * jnp.cos(lo) - jnp.sin(hi) * jnp.sin(lo)
    sin = jnp.sin(hi) * jnp.cos(lo) + jnp.cos(hi) * jnp.sin(lo)
    return cos.reshape(rows, cols), sin.reshape(rows, cols)


def _fourier_table(seq):
    cos, sin = _cos_sin(seq, seq, seq)
    return jnp.concatenate([cos, -sin], axis=1).astype(BF16)


def _channel_dft(seq):
    ang = _angles(jnp.arange(FN_GROUP, dtype=jnp.int32)[:, None], FN_GROUP, FN_GROUP)
    scale = 1.0 / math.sqrt(seq * FN_GROUP)
    eye = jnp.eye(FN_WIDTH // FN_GROUP, dtype=F32)
    return jnp.stack([jnp.kron(eye, jnp.cos(ang) * scale), jnp.kron(eye, jnp.sin(ang) * scale)])


def _hyena_dft(seq):
    cos, sin = _cos_sin(seq, seq, 2 * seq)
    r = lax.broadcasted_iota(jnp.int32, (seq, seq), 0)
    c = lax.broadcasted_iota(jnp.int32, (seq, seq), 1)
    dft = jnp.concatenate([cos, jnp.where(r == 0, (1 - 2 * (c % 2)).astype(F32), sin)], axis=0).astype(BF16)
    dft_t = jnp.concatenate([cos, jnp.where(c == 0, (1 - 2 * (r % 2)).astype(F32), sin)], axis=1).astype(BF16)
    return dft, dft_t


def _tri_incl(seq):
    r = lax.broadcasted_iota(jnp.int32, (seq, seq), 0)
    c = lax.broadcasted_iota(jnp.int32, (seq, seq), 1)
    return (r <= c).astype(BF16)


def _pad_to(a, shape):
    return jnp.pad(a, [(0, t - s) for s, t in zip(a.shape, shape)])


def kernel(x, c, ctx, c_ctx, norm_mix_g, norm_ffn_g, final_norm_g, w_mod, b_mod, w_in, w_out, w_fnet,
           hy_conv_w, hy_conv_b, hy_w1, hy_b1, hy_w2, hy_b2, hy_w3, hy_b3, hy_w_out, hy_freq, hy_bias,
           hg_lb, hg_norm_g, w_router, w_gate, w_up, w_down):
    bsz, seq, d = x.shape
    ctx_len = ctx.shape[1]
    depth = w_in.shape[0]

    p = jax.nn.softmax(hg_lb.astype(F32), axis=0)
    lbs4 = (jnp.cumsum(p, axis=0) - p[0:1]).reshape(depth, 2, 1, HG_WIDTH)
    w_in_bf = w_in.astype(BF16)
    w_out_bf = w_out.astype(BF16)
    w_router_t = jnp.swapaxes(w_router, 1, 2)
    g_mix3 = norm_mix_g.reshape(depth, 1, d)
    g_ffn3 = norm_ffn_g.reshape(depth, 1, d)
    gain3 = hg_norm_g.reshape(depth, 1, HG_WIDTH)
    b_mod3 = b_mod.reshape(depth, 1, 6 * d)
    conv_b3 = hy_conv_b.reshape(depth, 1, 3 * HY_WIDTH)
    hy_bias3 = hy_bias.reshape(depth, 1, HY_WIDTH)
    mlp = (_pad_to(hy_w1, (depth, HY_PAD, HY_PAD)), _pad_to(hy_b1.reshape(depth, 1, -1), (depth, 1, HY_PAD)),
           _pad_to(hy_w2, (depth, HY_PAD, HY_PAD)), _pad_to(hy_b2.reshape(depth, 1, -1), (depth, 1, HY_PAD)),
           _pad_to(hy_w3, (depth, HY_PAD, HY_PAD)), _pad_to(hy_b3.reshape(depth, 1, -1), (depth, 1, HY_PAD)),
           _pad_to(hy_w_out, (depth, HY_PAD, 2 * HY_WIDTH)), _pad_to(hy_freq, (depth, 8, HY_PAD)))
    max_decay = math.log(1e-2) / 0.3
    min_decay = math.log(1e-2) / 1.5
    deltas = jnp.linspace(min_decay, max_decay, HY_WIDTH, dtype=F32).reshape(1, HY_WIDTH)
    rows = 16
    cc = jnp.zeros((rows, d), F32).at[:bsz].set(c).at[bsz].set(c_ctx)

    tables = {}
    for n in {seq, ctx_len}:
        dft, dft_t = _hyena_dft(n)
        tables[n] = dict(fourier=_fourier_table(n), chan=_channel_dft(n), dft=dft, dft_t=dft_t, tri=_tri_incl(n))

    x_row = lambda b: b
    ctx_row = lambda b: bsz
    zero_state = jnp.zeros((bsz, HG_HEADS, HG_HEAD, HG_HEAD), F32)

    def mixers(hh, y_hg, n, l):
        t = tables[n]
        y_fn = _fourier(hh, _fn_prep(t["chan"], w_fnet, l), t["fourier"])
        kc, ks = _hy_filter(n, t["dft"], mlp, deltas, l)
        z, x0 = _hy_pre(hh, hy_conv_w, conv_b3, l)
        yc, ys = _hy_fwd(z, t["dft"], kc, ks)
        y_hy = _hy_inv(yc, ys, t["dft_t"], z, x0, hy_bias3, l)
        return y_fn, y_hy, y_hg

    def sublayers(xx, hh, y_hg, mod4, mrow, n, l, final_g=None):
        cap = EC_CAPACITY * n // N_EXPERTS
        xx, xm, aff = _out_route(xx, *mixers(hh, y_hg, n, l), w_out_bf, mod4, mrow, g_ffn3, w_router_t, l)
        pos = _topk(aff, tables[n]["tri"], cap)
        xs = _gather(xm, pos, cap)
        ys = _ffn(xs.reshape(N_EXPERTS, bsz * cap, d), w_gate, w_up, w_down, l)
        return _combine(ys.reshape(N_EXPERTS, bsz, cap, d), pos, aff, xx, mod4, mrow, cap, final_g)

    xc = ctx
    for l in range(depth):
        last = l == depth - 1
        mod4 = _modulation(cc, w_mod, b_mod3, l).reshape(rows, 6, 1, d)
        h = _in_proj(x, mod4, x_row, g_mix3, w_in_bf, l)
        hc = _in_proj(xc.reshape(1, bsz * ctx_len, d), mod4, ctx_row, g_mix3, w_in_bf, l).reshape(bsz, ctx_len, -1)
        o_cf, s_f = _hgrn(hc, lbs4, gain3, zero_state, l, False)
        y_hg_c, s_b = _hgrn(hc, lbs4, gain3, zero_state, l, True, o_fwd=o_cf)
        o_xf, _ = _hgrn(h, lbs4, gain3, s_f, l, False)
        y_hg_x, _ = _hgrn(h, lbs4, gain3, s_b, l, True, o_fwd=o_xf)
        x = sublayers(x, h, y_hg_x, mod4, x_row, seq, l, final_norm_g.reshape(1, d) if last else None)
        if not last:
            xc = sublayers(xc, hc, y_hg_c, mod4, ctx_row, ctx_len, l)
    return x
```

```python
import functools
import math

import jax
import jax.numpy as jnp
from jax import lax
from jax.experimental import pallas as pl
from jax.experimental.pallas import tpu as pltpu

F32 = jnp.float32
BF16 = jnp.bfloat16

D_MODEL = 2048
FN_WIDTH = 512
FN_GROUP = 128
HY_WIDTH = 512
HG_WIDTH = 1024
HG_HEAD = 128
HG_HEADS = HG_WIDTH // HG_HEAD
HG_F_MIN = 1e-6
IN_WIDTH = FN_WIDTH + 3 * HY_WIDTH + 5 * HG_WIDTH
HY_BANDS = 16
HY_PAD = 128
N_EXPERTS = 16
EC_CAPACITY = 2
EXPERT_FF = 1024
EPS = 1e-6

HG_CHUNK = 128
HG_CHUNKS_PER_STEP = 2
HG_BASE = 8
HG_FAST_MAX_LOG2 = 115.0
LOG2_E = math.log2(math.e)
VMEM_LIMIT = 56 * 1024 * 1024

_COL_Q, _COL_FF, _COL_FB, _COL_I, _COL_G = 2, 3, 4, 5, 6


def _params(*sem):
    return pltpu.CompilerParams(dimension_semantics=sem, vmem_limit_bytes=VMEM_LIMIT)


def _dot(a, b):
    return jnp.dot(a, b, preferred_element_type=F32)


def _dot_nt(a, b):
    return lax.dot_general(a, b, (((1,), (1,)), ((), ())), preferred_element_type=F32)


def _split2(x):
    hi = x.astype(BF16)
    lo = (x - hi.astype(F32)).astype(BF16)
    return hi, lo


def _dot3(a, b, dot=_dot):
    ah, al = _split2(a)
    bh, bl = _split2(b)
    return dot(ah, bh) + dot(ah, bl) + dot(al, bh)


def _silu(x):
    return x * jax.nn.sigmoid(x)


def _norm_mod(x, g, sh, sc):
    ms = jnp.mean(x * x, axis=-1, keepdims=True)
    return (x * lax.rsqrt(ms + EPS) * g) * (1.0 + sc) + sh


def _mod_kernel(a_ref, w_ref, b_ref, o_ref):
    a = _silu(a_ref[...]).astype(BF16)
    o_ref[...] = _dot(a, w_ref[...].astype(BF16)) + b_ref[...]


def _modulation(cc, w_mod, b_mod3, l):
    rows, d = cc.shape
    n = w_mod.shape[-1]
    tn = 1024
    return pl.pallas_call(
        _mod_kernel,
        grid=(n // tn,),
        in_specs=[
            pl.BlockSpec((rows, d), lambda j: (0, 0)),
            pl.BlockSpec((None, d, tn), lambda j: (l, 0, j)),
            pl.BlockSpec((None, 1, tn), lambda j: (l, 0, j)),
        ],
        out_specs=pl.BlockSpec((rows, tn), lambda j: (0, j)),
        out_shape=jax.ShapeDtypeStruct((rows, n), F32),
        compiler_params=_params("arbitrary"),
        name="modulation",
    )(cc, w_mod, b_mod3)


def _in_kernel(x_ref, g_ref, sh_ref, sc_ref, w_ref, o_ref, xm_ref):
    @pl.when(pl.program_id(2) == 0)
    def _():
        xm_ref[...] = _norm_mod(x_ref[...], g_ref[...], sh_ref[...], sc_ref[...]).astype(BF16)

    o_ref[...] = _dot(xm_ref[...], w_ref[...]).astype(o_ref.dtype)


def _in_proj(x, mod4, mrow, gamma3, w_in_bf, l):
    bsz, seq, d = x.shape
    n = w_in_bf.shape[-1]
    tm = min(seq, 1024)
    tn = 1024
    return pl.pallas_call(
        _in_kernel,
        grid=(bsz, seq // tm, n // tn),
        in_specs=[
            pl.BlockSpec((None, tm, d), lambda b, i, j: (b, i, 0)),
            pl.BlockSpec((None, 1, d), lambda b, i, j: (l, 0, 0)),
            pl.BlockSpec((None, None, 1, d), lambda b, i, j: (mrow(b), 0, 0, 0)),
            pl.BlockSpec((None, None, 1, d), lambda b, i, j: (mrow(b), 1, 0, 0)),
            pl.BlockSpec((None, d, tn), lambda b, i, j: (l, 0, j)),
        ],
        out_specs=pl.BlockSpec((None, tm, tn), lambda b, i, j: (b, i, j)),
        out_shape=jax.ShapeDtypeStruct((bsz, seq, n), BF16),
        scratch_shapes=[pltpu.VMEM((tm, d), BF16)],
        compiler_params=_params("arbitrary", "arbitrary", "arbitrary"),
        name="in_proj",
    )(x, gamma3, mod4, mod4, w_in_bf)


def _hg_kernel(rev, fuse_out, *refs):
    if fuse_out:
        (zq_ref, zf_ref, zi_ref, g_ref, of_ref, lb_ref, gain_ref, s0_ref, y_ref, st_ref,
         s_ref, qh_ref, kh_ref, oi_ref, q_ref, kk_ref, b_ref) = refs
    else:
        (zq_ref, zf_ref, zi_ref, lb_ref, s0_ref, y_ref, st_ref,
         s_ref, qh_ref, kh_ref, oi_ref, q_ref, kk_ref, b_ref) = refs
    C = HG_CHUNK
    W = HG_WIDTH
    R = zq_ref.shape[0]
    chunks = [slice(k * C, (k + 1) * C) for k in range(R // C)]
    if rev:
        chunks = chunks[::-1]
    heads = [slice(h * HG_HEAD, (h + 1) * HG_HEAD) for h in range(HG_HEADS)]
    c = pl.program_id(1)

    @pl.when(c == 0)
    def _():
        s_ref[...] = s0_ref[...]

    def scan_order_iota(n):
        row = lax.broadcasted_iota(jnp.int32, (n, n), 0)
        col = lax.broadcasted_iota(jnp.int32, (n, n), 1)
        return (n - 1 - row, n - 1 - col) if rev else (row, col)

    row, col = scan_order_iota(C)
    tri = jnp.where(col <= row, 1.0, 0.0)
    tri_bf = tri.astype(BF16)

    def ref_rows(b, s, r):
        n = b.shape[0]
        b3 = b.reshape(n // s, s, W)
        return jnp.broadcast_to(b3[:, r:r + 1, :], (n // s, s, W)).reshape(n, W)

    def mid_offset(b, s):
        hh = s // 2
        return b - ref_rows(b, s, hh if rev else hh - 1)

    lb = lb_ref[...]
    half = C // 2
    spread = None
    for rs in chunks:
        zf = zf_ref[rs, :].astype(F32)
        q = _silu(zq_ref[rs, :])
        f = jnp.maximum(lb + (1.0 - lb) * jax.nn.sigmoid(zf), HG_F_MIN)
        kk = (1.0 - f).astype(BF16)
        hi, lo = _split2(jnp.log(f) * LOG2_E)
        b = _dot(tri_bf, hi) + _dot(tri_bf, lo)
        btot = b[0:1, :] if rev else b[C - 1:C, :]
        qe = q * jnp.exp2(b).astype(BF16)
        kd = kk * jnp.exp2(btot - b).astype(BF16)
        sdec = jnp.exp2(btot)
        v_t = zi_ref[rs, :].T
        for h, hs in enumerate(heads):
            st = s_ref[h]
            oi_ref[rs, hs] = _dot_nt(qe[:, hs], st.astype(BF16))
            s_ref[h] = st * sdec[:, hs] + _dot(v_t[hs, :], kd[:, hs])
        q_ref[rs, :] = q
        kk_ref[rs, :] = kk
        b_ref[rs, :] = b
        m = jnp.max(jnp.abs(mid_offset(b, half)))
        spread = m if spread is None else jnp.maximum(spread, m)

    def intra_chunk(halvings, block):
        q = q_ref[...]
        kk = kk_ref[...]
        b = b_ref[...]
        for i, s in enumerate(halvings):
            w = jnp.exp2(-jnp.abs(mid_offset(b, s))).astype(BF16)
            qh_ref[i] = q * w
            kh_ref[i] = kk * w
        d = mid_offset(b, block)
        nh = len(halvings)
        qh_ref[nh] = q * jnp.exp2(d).astype(BF16)
        kh_ref[nh] = kk * jnp.exp2(-d).astype(BF16)
        level_mask = [jnp.where(((row // s) == (col // s)) & ((row % s) >= s // 2) & ((col % s) < s // 2), 1.0, 0.0)
                      for s in halvings]
        block_mask = jnp.where((row // block) == (col // block), tri, 0.0)

        pairs = [(rs, hs) for rs in chunks for hs in heads]
        scores = [[_dot_nt(qh_ref[i, rs, hs], kh_ref[i, rs, hs]) for i in range(nh + 1)] for rs, hs in pairs]
        for (rs, hs), m in zip(pairs, scores):
            a = jnp.where(block_mask > 0.5, m[nh], 0.0)
            for i in range(nh):
                a = a + m[i] * level_mask[i]
            o = oi_ref[rs, hs] + _dot(a.astype(BF16), zi_ref[rs, hs])
            if fuse_out:
                o = o + of_ref[rs, hs]
                ms = jnp.mean(o * o, axis=-1, keepdims=True)
                g = g_ref[rs, hs].astype(F32)
                y_ref[rs, hs] = (o * lax.rsqrt(ms + EPS) * gain_ref[:, hs] * _silu(g)).astype(y_ref.dtype)
            else:
                y_ref[rs, hs] = o

    in_range = spread <= HG_FAST_MAX_LOG2

    @pl.when(in_range)
    def _():
        intra_chunk([C], half)

    @pl.when(jnp.logical_not(in_range))
    def _():
        sizes = []
        s = C
        while s > HG_BASE:
            sizes.append(s)
            s //= 2
        intra_chunk(sizes, HG_BASE)

    @pl.when(c == pl.num_programs(1) - 1)
    def _():
        st_ref[...] = s_ref[...]


def _hgrn(h, lbs4, gain3, s0, l, rev, o_fwd=None):
    bsz, seq, _ = h.shape
    R = HG_CHUNKS_PER_STEP * HG_CHUNK
    nc = seq // R
    W = HG_WIDTH
    fuse_out = o_fwd is not None
    n_factor = (HG_CHUNK // HG_BASE).bit_length()
    cidx = (lambda c: nc - 1 - c) if rev else (lambda c: c)

    def hcol(k):
        return pl.BlockSpec((None, R, W), lambda b, c: (b, cidx(c), k))

    lb_spec = pl.BlockSpec((None, None, 1, W), lambda b, c: (l, 1 if rev else 0, 0, 0))
    s_spec = pl.BlockSpec((None, HG_HEADS, HG_HEAD, HG_HEAD), lambda b, c: (b, 0, 0, 0))
    o_spec = pl.BlockSpec((None, R, W), lambda b, c: (b, cidx(c), 0))
    if fuse_out:
        in_specs = [hcol(_COL_Q), hcol(_COL_FB if rev else _COL_FF), hcol(_COL_I), hcol(_COL_G), o_spec,
                    lb_spec, pl.BlockSpec((None, 1, W), lambda b, c: (l, 0, 0)), s_spec]
        args = (h, h, h, h, o_fwd, lbs4, gain3, s0)
        out_dtype = BF16
    else:
        in_specs = [hcol(_COL_Q), hcol(_COL_FB if rev else _COL_FF), hcol(_COL_I), lb_spec, s_spec]
        args = (h, h, h, lbs4, s0)
        out_dtype = F32
    return pl.pallas_call(
        functools.partial(_hg_kernel, rev, fuse_out),
        grid=(bsz, nc),
        in_specs=in_specs,
        out_specs=[o_spec, s_spec],
        out_shape=[jax.ShapeDtypeStruct((bsz, seq, W), out_dtype),
                   jax.ShapeDtypeStruct((bsz, HG_HEADS, HG_HEAD, HG_HEAD), F32)],
        scratch_shapes=[pltpu.VMEM((HG_HEADS, HG_HEAD, HG_HEAD), F32),
                        pltpu.VMEM((n_factor, R, W), BF16), pltpu.VMEM((n_factor, R, W), BF16),
                        pltpu.VMEM((R, W), F32),
                        pltpu.VMEM((R, W), BF16), pltpu.VMEM((R, W), BF16), pltpu.VMEM((R, W), F32)],
        compiler_params=_params("arbitrary", "arbitrary"),
        name="hgrn_bwd" if rev else "hgrn_fwd",
    )(*args)


def _fn_prep_kernel(cs_ref, w_ref, o_ref):
    w = w_ref[...]
    o_ref[:, :FN_WIDTH] = _dot3(cs_ref[0], w).astype(BF16)
    o_ref[:, FN_WIDTH:] = _dot3(cs_ref[1], w).astype(BF16)


def _fn_prep(chan_dft, w_fnet, l):
    return pl.pallas_call(
        _fn_prep_kernel,
        grid=(1,),
        in_specs=[pl.BlockSpec((2, FN_WIDTH, FN_WIDTH), lambda i: (0, 0, 0)),
                  pl.BlockSpec((None, FN_WIDTH, FN_WIDTH), lambda i: (l, 0, 0))],
        out_specs=pl.BlockSpec((FN_WIDTH, 2 * FN_WIDTH), lambda i: (0, 0)),
        out_shape=jax.ShapeDtypeStruct((FN_WIDTH, 2 * FN_WIDTH), BF16),
        compiler_params=_params("arbitrary"),
        name="fnet_prep",
    )(chan_dft, w_fnet)


def _fn_kernel(u_ref, wc_ref, t_ref, o_ref, p_ref):
    seq = u_ref.shape[0]
    rc = min(seq, 512)

    @pl.when(pl.program_id(1) == 0)
    def _():
        def rows(i, carry):
            r0 = pl.multiple_of(i * rc, rc)
            p = _dot(u_ref[pl.ds(r0, rc), :], wc_ref[...])
            p_ref[pl.ds(r0, rc), :] = p[:, :FN_WIDTH].astype(BF16)
            p_ref[pl.ds(pl.multiple_of(seq + r0, rc), rc), :] = p[:, FN_WIDTH:].astype(BF16)
            return carry

        lax.fori_loop(0, seq // rc, rows, 0)

    o_ref[...] = _dot(t_ref[...], p_ref[...]).astype(o_ref.dtype)


def _fourier(h, wc, table):
    bsz, seq, _ = h.shape
    tt = min(seq, 1024)
    return pl.pallas_call(
        _fn_kernel,
        grid=(bsz, seq // tt),
        in_specs=[pl.BlockSpec((None, seq, FN_WIDTH), lambda b, t: (b, 0, 0)),
                  pl.BlockSpec((FN_WIDTH, 2 * FN_WIDTH), lambda b, t: (0, 0)),
                  pl.BlockSpec((tt, 2 * seq), lambda b, t: (t, 0))],
        out_specs=pl.BlockSpec((None, tt, FN_WIDTH), lambda b, t: (b, t, 0)),
        out_shape=jax.ShapeDtypeStruct((bsz, seq, FN_WIDTH), BF16),
        scratch_shapes=[pltpu.VMEM((2 * seq, FN_WIDTH), BF16)],
        compiler_params=_params("arbitrary", "arbitrary"),
        name="fourier",
    )(h, wc, table)


def _hy_filter_kernel(seq, wc_ref, ws_ref, w1_ref, b1_ref, w2_ref, b2_ref, w3_ref, b3_ref, wo_ref, fr_ref,
                      dl_ref, kc_ref, ks_ref, h_ref):
    i = pl.program_id(0)
    tf = wc_ref.shape[0]
    nfft = 2 * seq

    @pl.when(i == 0)
    def _():
        pos = lax.broadcasted_iota(jnp.int32, (seq, HY_PAD), 0).astype(F32)
        lane = lax.broadcasted_iota(jnp.int32, (seq, HY_PAD), 1)
        t = pos / float(max(seq - 1, 1))
        w = (2.0 * math.pi) * pos / float(seq)
        band_id = jnp.where(lane <= HY_BANDS, lane - 1, lane - 1 - HY_BANDS).astype(F32)
        band = 1e-4 + band_id * ((HY_BANDS - 1 - 1e-4) / (HY_BANDS - 1))
        arg = band * w
        z = jnp.where(lane == 0, t,
                      jnp.where(lane <= HY_BANDS, jnp.cos(arg),
                                jnp.where(lane <= 2 * HY_BANDS, -jnp.sin(arg), 0.0)))
        fr = fr_ref[...]
        hdn = jnp.sin(fr[0:1] * (_dot3(z, w1_ref[...]) + b1_ref[...]))
        hdn = jnp.sin(fr[1:2] * (_dot3(hdn, w2_ref[...]) + b2_ref[...]))
        hdn = jnp.sin(fr[2:3] * (_dot3(hdn, w3_ref[...]) + b3_ref[...]))
        hf = _dot3(hdn, wo_ref[...])
        decay = jnp.exp(-t[:, 0:1] * jnp.abs(dl_ref[...]))
        first = lax.broadcasted_iota(jnp.int32, (seq, HY_WIDTH), 0) == 0
        h_ref[:, :HY_WIDTH] = (hf[:, :HY_WIDTH] * decay).astype(BF16)
        h_ref[:, HY_WIDTH:] = jnp.where(first, 0.0, hf[:, HY_WIDTH:] * decay).astype(BF16)

    gc = _dot(wc_ref[...], h_ref[...])
    gs = _dot(ws_ref[...], h_ref[...])
    first = (lax.broadcasted_iota(jnp.int32, (tf, HY_WIDTH), 0) + i * tf) == 0
    scale = jnp.where(first, 1.0 / nfft, 2.0 / nfft)
    kc_ref[...] = (gc[:, :HY_WIDTH] + gc[:, HY_WIDTH:]) * scale
    ks_ref[...] = jnp.where(first, gs[:, :HY_WIDTH] + gs[:, HY_WIDTH:], gs[:, :HY_WIDTH] - gs[:, HY_WIDTH:]) * scale


def _hy_filter(seq, dft, mlp, deltas, l):
    w1p, b1p, w2p, b2p, w3p, b3p, wop, frp = mlp
    tf = min(seq, 512)
    nf = seq // tf

    def full(a):
        shp = a.shape[1:]
        return pl.BlockSpec((None,) + shp, lambda i: (l,) + (0,) * len(shp))

    return pl.pallas_call(
        functools.partial(_hy_filter_kernel, seq),
        grid=(nf,),
        in_specs=[pl.BlockSpec((tf, seq), lambda i: (i, 0)),
                  pl.BlockSpec((tf, seq), lambda i: (i + nf, 0)),
                  full(w1p), full(b1p), full(w2p), full(b2p), full(w3p), full(b3p), full(wop), full(frp),
                  pl.BlockSpec((1, HY_WIDTH), lambda i: (0, 0))],
        out_specs=[pl.BlockSpec((tf, HY_WIDTH), lambda i: (i, 0)),
                   pl.BlockSpec((tf, HY_WIDTH), lambda i: (i, 0))],
        out_shape=[jax.ShapeDtypeStruct((seq, HY_WIDTH), F32), jax.ShapeDtypeStruct((seq, HY_WIDTH), F32)],
        scratch_shapes=[pltpu.VMEM((seq, 2 * HY_WIDTH), BF16)],
        compiler_params=_params("arbitrary"),
        name="hyena_filter",
    )(dft, dft, w1p, b1p, w2p, b2p, w3p, b3p, wop, frp, deltas)


def _hy_pre_kernel(uv_ref, u1_ref, u0_ref, cw_ref, cb_ref, z_ref, x0_ref):
    seq = uv_ref.shape[0]
    rowi = lax.broadcasted_iota(jnp.int32, (seq, 128), 0)
    j = pl.program_id(1)

    def conv(u_ref, part):
        u = u_ref[...].astype(F32)
        prev = jnp.where(rowi == 0, 0.0, pltpu.roll(u, 1, 0))
        nxt = jnp.where(rowi == seq - 1, 0.0, pltpu.roll(u, seq - 1, 0))
        cs = pl.ds(pl.multiple_of(part * HY_WIDTH + j * 128, 128), 128)
        return prev * cw_ref[0:1, cs] + u * cw_ref[1:2, cs] + nxt * cw_ref[2:3, cs] + cb_ref[:, cs]

    z_ref[...] = (conv(u1_ref, 1) * conv(uv_ref, 0)).astype(BF16)
    x0_ref[...] = conv(u0_ref, 2).astype(BF16)


def _hy_pre(h, conv_w, conv_b3, l):
    bsz, seq, _ = h.shape
    lanes = 128
    nj = HY_WIDTH // lanes
    off = FN_WIDTH // lanes

    def part(p):
        return pl.BlockSpec((None, seq, lanes), lambda b, j: (b, 0, off + p * nj + j))

    o_spec = pl.BlockSpec((None, seq, lanes), lambda b, j: (b, 0, j))
    return pl.pallas_call(
        _hy_pre_kernel,
        grid=(bsz, nj),
        in_specs=[part(0), part(1), part(2),
                  pl.BlockSpec((None, 3, 3 * HY_WIDTH), lambda b, j: (l, 0, 0)),
                  pl.BlockSpec((None, 1, 3 * HY_WIDTH), lambda b, j: (l, 0, 0))],
        out_specs=[o_spec, o_spec],
        out_shape=[jax.ShapeDtypeStruct((bsz, seq, HY_WIDTH), BF16)] * 2,
        compiler_params=_params("arbitrary", "arbitrary"),
        name="hyena_pre",
    )(h, h, h, conv_w, conv_b3)


def _hy_fwd_kernel(wc_ref, ws_ref, z_ref, kc_ref, ks_ref, yc_ref, ys_ref):
    tf = wc_ref.shape[0]
    z = z_ref[...]
    uc = _dot(wc_ref[...], z)
    us = _dot(ws_ref[...], z)
    kc = kc_ref[...]
    ks = ks_ref[...]
    first = (lax.broadcasted_iota(jnp.int32, (tf, HY_WIDTH), 0) + pl.program_id(0) * tf) == 0
    ss = us * ks
    yc_ref[...] = (uc * kc - jnp.where(first, 0.0, ss)).astype(BF16)
    ys_ref[...] = jnp.where(first, ss, uc * ks + us * kc).astype(BF16)


def _hy_fwd(z, dft, kc, ks):
    bsz, seq, _ = z.shape
    tf = min(seq, 1024)
    nf = seq // tf
    k_spec = pl.BlockSpec((tf, HY_WIDTH), lambda i, b: (i, 0))
    y_spec = pl.BlockSpec((None, tf, HY_WIDTH), lambda i, b: (b, i, 0))
    return pl.pallas_call(
        _hy_fwd_kernel,
        grid=(nf, bsz),
        in_specs=[pl.BlockSpec((tf, seq), lambda i, b: (i, 0)),
                  pl.BlockSpec((tf, seq), lambda i, b: (i + nf, 0)),
                  pl.BlockSpec((None, seq, HY_WIDTH), lambda i, b: (b, 0, 0)),
                  k_spec, k_spec],
        out_specs=[y_spec, y_spec],
        out_shape=[jax.ShapeDtypeStruct((bsz, seq, HY_WIDTH), BF16)] * 2,
        compiler_params=_params("arbitrary", "arbitrary"),
        name="hyena_dft",
    )(dft, dft, z, kc, ks)


def _hy_inv_kernel(tc_ref, ts_ref, yc_ref, ys_ref, z_ref, x0_ref, db_ref, o_ref):
    y = _dot(tc_ref[...], yc_ref[...]) + _dot(ts_ref[...], ys_ref[...])
    z = z_ref[...].astype(F32)
    o_ref[...] = (x0_ref[...].astype(F32) * (y + z * db_ref[...])).astype(BF16)


def _hy_inv(yc, ys, dft_t, z, x0, hy_bias3, l):
    bsz, seq, _ = z.shape
    tt = min(seq, 1024)
    y_spec = pl.BlockSpec((None, seq, HY_WIDTH), lambda t, b: (b, 0, 0))
    r_spec = pl.BlockSpec((None, tt, HY_WIDTH), lambda t, b: (b, t, 0))
    return pl.pallas_call(
        _hy_inv_kernel,
        grid=(seq // tt, bsz),
        in_specs=[pl.BlockSpec((tt, seq), lambda t, b: (t, 0)),
                  pl.BlockSpec((tt, seq), lambda t, b: (t, 1)),
                  y_spec, y_spec, r_spec, r_spec,
                  pl.BlockSpec((None, 1, HY_WIDTH), lambda t, b: (l, 0, 0))],
        out_specs=r_spec,
        out_shape=jax.ShapeDtypeStruct((bsz, seq, HY_WIDTH), BF16),
        compiler_params=_params("arbitrary", "arbitrary"),
        name="hyena_idft",
    )(dft_t, dft_t, yc, ys, z, x0, hy_bias3)


def _out_route_kernel(yf_ref, yh_ref, yg_ref, wf_ref, wh_ref, wg_ref, x_ref, g1_ref, gam_ref, sh_ref, sc_ref,
                      wr_ref, xo_ref, xm_ref, aff_ref):
    mix = _dot(yf_ref[...], wf_ref[...]) + _dot(yh_ref[...], wh_ref[...]) + _dot(yg_ref[...], wg_ref[...])
    x = x_ref[...] + g1_ref[...] * mix
    xo_ref[...] = x
    xm = _norm_mod(x, gam_ref[...], sh_ref[...], sc_ref[...])
    xm_ref[...] = xm.astype(BF16)
    logits = _dot3(wr_ref[...], xm, dot=_dot_nt)
    mx = jnp.max(logits, axis=0, keepdims=True)
    ex = jnp.exp(logits - mx)
    aff_ref[...] = ex / jnp.sum(ex, axis=0, keepdims=True)


def _out_route(x, y_fn, y_hy, y_hg, w_out_bf, mod4, mrow, gamma3, w_router_t, l):
    bsz, seq, d = x.shape
    tm = min(seq, 512)
    half = FN_WIDTH

    def mod_row(k):
        return pl.BlockSpec((None, None, 1, d), lambda b, i: (mrow(b), k, 0, 0))

    return pl.pallas_call(
        _out_route_kernel,
        grid=(bsz, seq // tm),
        in_specs=[pl.BlockSpec((None, tm, half), lambda b, i: (b, i, 0)),
                  pl.BlockSpec((None, tm, half), lambda b, i: (b, i, 0)),
                  pl.BlockSpec((None, tm, HG_WIDTH), lambda b, i: (b, i, 0)),
                  pl.BlockSpec((None, half, d), lambda b, i: (l, 0, 0)),
                  pl.BlockSpec((None, half, d), lambda b, i: (l, 1, 0)),
                  pl.BlockSpec((None, HG_WIDTH, d), lambda b, i: (l, 1, 0)),
                  pl.BlockSpec((None, tm, d), lambda b, i: (b, i, 0)),
                  mod_row(2),
                  pl.BlockSpec((None, 1, d), lambda b, i: (l, 0, 0)),
                  mod_row(3), mod_row(4),
                  pl.BlockSpec((None, N_EXPERTS, d), lambda b, i: (l, 0, 0))],
        out_specs=[pl.BlockSpec((None, tm, d), lambda b, i: (b, i, 0)),
                   pl.BlockSpec((None, tm, d), lambda b, i: (b, i, 0)),
                   pl.BlockSpec((None, N_EXPERTS, tm), lambda b, i: (b, 0, i))],
        out_shape=[jax.ShapeDtypeStruct((bsz, seq, d), F32),
                   jax.ShapeDtypeStruct((bsz, seq, d), BF16),
                   jax.ShapeDtypeStruct((bsz, N_EXPERTS, seq), F32)],
        compiler_params=_params("arbitrary", "arbitrary"),
        name="out_route",
    )(y_fn, y_hy, y_hg, w_out_bf, w_out_bf, w_out_bf, x, mod4, gamma3, mod4, mod4, w_router_t)


def _topk_kernel(cap, aff_ref, tri_ref, pos_ref):
    a = aff_ref[...]

    def count(mask):
        return jnp.sum(jnp.where(mask, 1.0, 0.0), axis=1, keepdims=True)

    def as_float(bits):
        return pltpu.bitcast(jnp.broadcast_to(bits, a.shape), F32)

    def step(i, thr_bits):
        cand = thr_bits | jnp.left_shift(jnp.int32(1), 30 - i)
        return jnp.where(count(a >= as_float(cand)) >= cap, cand, thr_bits)

    thr = as_float(lax.fori_loop(0, 31, step, jnp.zeros((a.shape[0], 1), jnp.int32)))
    above = a > thr
    tie = a == thr
    room = cap - count(above)
    tie_rank = _dot(jnp.where(tie, 1.0, 0.0).astype(BF16), tri_ref[...])
    sel = jnp.where(above, 1.0, jnp.where(tie, jnp.where(tie_rank <= room, 1.0, 0.0), 0.0))
    slot = _dot(sel.astype(BF16), tri_ref[...]) - 1.0
    pos_ref[...] = jnp.where(sel > 0.5, slot, -1.0).astype(jnp.int32)


def _topk(aff, tri_incl, cap):
    bsz, ne, seq = aff.shape
    return pl.pallas_call(
        functools.partial(_topk_kernel, cap),
        grid=(bsz,),
        in_specs=[pl.BlockSpec((None, ne, seq), lambda b: (b, 0, 0)),
                  pl.BlockSpec((seq, seq), lambda b: (0, 0))],
        out_specs=pl.BlockSpec((None, ne, seq), lambda b: (b, 0, 0)),
        out_shape=jax.ShapeDtypeStruct((bsz, ne, seq), jnp.int32),
        compiler_params=_params("arbitrary"),
        name="moe_topk",
    )(aff, tri_incl)


def _gather_kernel(cap, xm_ref, pos_ref, o_ref):
    seq = xm_ref.shape[0]
    pos = pos_ref[pl.ds(pl.program_id(1), 1), :]
    slot = lax.broadcasted_iota(jnp.int32, (cap, seq), 0)
    onehot = jnp.where(slot == pos, 1.0, 0.0).astype(BF16)
    o_ref[...] = _dot(onehot, xm_ref[...]).astype(BF16)


def _gather(xm, pos, cap):
    bsz, seq, d = xm.shape
    return pl.pallas_call(
        functools.partial(_gather_kernel, cap),
        grid=(bsz, N_EXPERTS),
        in_specs=[pl.BlockSpec((None, seq, d), lambda b, e: (b, 0, 0)),
                  pl.BlockSpec((None, N_EXPERTS, seq), lambda b, e: (b, 0, 0))],
        out_specs=pl.BlockSpec((None, None, cap, d), lambda b, e: (e, b, 0, 0)),
        out_shape=jax.ShapeDtypeStruct((N_EXPERTS, bsz, cap, d), BF16),
        compiler_params=_params("arbitrary", "arbitrary"),
        name="moe_gather",
    )(xm, pos)


def _ffn_kernel(xs_ref, wg_ref, wu_ref, wd_ref, o_ref, acc_ref):
    j = pl.program_id(2)

    @pl.when((pl.program_id(0) == 0) & (pl.program_id(1) == 0) & (j == 0))
    def _():
        acc_ref[...] = jnp.zeros_like(acc_ref)

    xs = xs_ref[...]
    hid = _silu(_dot(xs, wg_ref[...].astype(BF16))) * _dot(xs, wu_ref[...].astype(BF16))
    part = _dot(hid.astype(BF16), wd_ref[...].astype(BF16))
    total = part + jnp.where(j > 0, acc_ref[...], 0.0)
    acc_ref[...] = total
    o_ref[...] = total.astype(BF16)


def _ffn(xs, w_gate, w_up, w_down, l):
    ne, rows, d = xs.shape
    ff = w_gate.shape[-1]
    tm = min(rows, 1024)
    tj = 512
    return pl.pallas_call(
        _ffn_kernel,
        grid=(ne, rows // tm, ff // tj),
        in_specs=[pl.BlockSpec((None, tm, d), lambda e, m, j: (e, m, 0)),
                  pl.BlockSpec((None, None, d, tj), lambda e, m, j: (l, e, 0, j)),
                  pl.BlockSpec((None, None, d, tj), lambda e, m, j: (l, e, 0, j)),
                  pl.BlockSpec((None, None, tj, d), lambda e, m, j: (l, e, j, 0))],
        out_specs=pl.BlockSpec((None, tm, d), lambda e, m, j: (e, m, 0)),
        out_shape=jax.ShapeDtypeStruct((ne, rows, d), BF16),
        scratch_shapes=[pltpu.VMEM((tm, d), F32)],
        compiler_params=_params("arbitrary", "arbitrary", "arbitrary"),
        name="moe_ffn",
    )(xs, w_gate, w_up, w_down)


def _combine_kernel(cap, final_norm, *refs):
    if final_norm:
        ys_ref, pos_ref, aff_ref, x_ref, g_ref, fg_ref, o_ref, pos_t_ref, gate_t_ref, w_ref = refs
    else:
        ys_ref, pos_ref, aff_ref, x_ref, g_ref, o_ref, pos_t_ref, gate_t_ref, w_ref = refs
    i = pl.program_id(1)
    tm = x_ref.shape[0]
    ne = pos_ref.shape[0]

    @pl.when(i == 0)
    def _():
        pos_t_ref[...] = pos_ref[...].astype(F32).T
        gate_t_ref[...] = aff_ref[...].T

    r0 = pl.multiple_of(i * tm, tm)
    pos = pos_t_ref[pl.ds(r0, tm), :]
    gate = gate_t_ref[pl.ds(r0, tm), :]
    if cap % 128 == 0:
        lane = lax.broadcasted_iota(jnp.int32, (tm, cap), 1).astype(F32)
        for e in range(ne):
            w_ref[:, e * cap:(e + 1) * cap] = jnp.where(lane == pos[:, e:e + 1], gate[:, e:e + 1], 0.0).astype(BF16)
    else:
        lane = lax.broadcasted_iota(jnp.int32, (tm, ne * cap), 1).astype(F32)
        w = jnp.zeros((tm, ne * cap), F32)
        for e in range(ne):
            hit = jnp.logical_and(lane == pos[:, e:e + 1] + float(e * cap), pos[:, e:e + 1] >= 0.0)
            w = w + jnp.where(hit, gate[:, e:e + 1], 0.0)
        w_ref[...] = w.astype(BF16)
    ys = ys_ref[...].reshape(ne * cap, ys_ref.shape[-1])
    x = x_ref[...] + g_ref[...] * _dot(w_ref[...], ys)
    if final_norm:
        ms = jnp.mean(x * x, axis=-1, keepdims=True)
        x = x * lax.rsqrt(ms + EPS) * fg_ref[...]
    o_ref[...] = x


def _combine(ys, pos, aff, x, mod4, mrow, cap, final_g=None):
    bsz, seq, d = x.shape
    ne = pos.shape[1]
    tm = min(seq, 256)
    final_norm = final_g is not None
    in_specs = [pl.BlockSpec((ne, None, cap, d), lambda b, i: (0, b, 0, 0)),
                pl.BlockSpec((None, ne, seq), lambda b, i: (b, 0, 0)),
                pl.BlockSpec((None, ne, seq), lambda b, i: (b, 0, 0)),
                pl.BlockSpec((None, tm, d), lambda b, i: (b, i, 0)),
                pl.BlockSpec((None, None, 1, d), lambda b, i: (mrow(b), 5, 0, 0))]
    args = (ys, pos, aff, x, mod4)
    if final_norm:
        in_specs.append(pl.BlockSpec((1, d), lambda b, i: (0, 0)))
        args += (final_g,)
    return pl.pallas_call(
        functools.partial(_combine_kernel, cap, final_norm),
        grid=(bsz, seq // tm),
        in_specs=in_specs,
        out_specs=pl.BlockSpec((None, tm, d), lambda b, i: (b, i, 0)),
        out_shape=jax.ShapeDtypeStruct((bsz, seq, d), F32),
        scratch_shapes=[pltpu.VMEM((seq, ne), F32), pltpu.VMEM((seq, ne), F32), pltpu.VMEM((tm, ne * cap), BF16)],
        compiler_params=_params("arbitrary", "arbitrary"),
        name="moe_combine",
    )(*args)


def _angles(row_ids, cols, n):
    c = lax.broadcasted_iota(jnp.int32, (row_ids.shape[0], cols), 1)
    return ((row_ids * c) % n).astype(F32) * (2.0 * math.pi / n)


def _cos_sin(rows, cols, n):
    step = min(rows, 32)
    hi = _angles(jnp.arange(0, rows, step, dtype=jnp.int32)[:, None], cols, n)[:, None, :]
    lo = _angles(jnp.arange(step, dtype=jnp.int32)[:, None], cols, n)[None, :, :]
    cos = jnp.cos(hi) * jnp.cos(lo) - jnp.sin(hi) * jnp.sin(lo)
    sin = jnp.sin(hi) * jnp.cos(lo) + jnp.cos(hi) * jnp.sin(lo)
    return cos.reshape(rows, cols), sin.reshape(rows, cols)


def _fourier_table(seq):
    cos, sin = _cos_sin(seq, seq, seq)
    return jnp.concatenate([cos, -sin], axis=1).astype(BF16)


def _channel_dft(seq):
    ang = _angles(jnp.arange(FN_GROUP, dtype=jnp.int32)[:, None], FN_GROUP, FN_GROUP)
    scale = 1.0 / math.sqrt(seq * FN_GROUP)
    eye = jnp.eye(FN_WIDTH // FN_GROUP, dtype=F32)
    return jnp.stack([jnp.kron(eye, jnp.cos(ang) * scale), jnp.kron(eye, jnp.sin(ang) * scale)])


def _hyena_dft(seq):
    cos, sin = _cos_sin(seq, seq, 2 * seq)
    r = lax.broadcasted_iota(jnp.int32, (seq, seq), 0)
    c = lax.broadcasted_iota(jnp.int32, (seq, seq), 1)
    dft = jnp.concatenate([cos, jnp.where(r == 0, (1 - 2 * (c % 2)).astype(F32), sin)], axis=0).astype(BF16)
    dft_t = jnp.concatenate([cos, jnp.where(c == 0, (1 - 2 * (r % 2)).astype(F32), sin)], axis=1).astype(BF16)
    return dft, dft_t


def _tri_incl(seq):
    r = lax.broadcasted_iota(jnp.int32, (seq, seq), 0)
    c = lax.broadcasted_iota(jnp.int32, (seq, seq), 1)
    return (r <= c).astype(BF16)


def _pad_to(a, shape):
    return jnp.pad(a, [(0, t - s) for s, t in zip(a.shape, shape)])


def kernel(x, c, ctx, c_ctx, norm_mix_g, norm_ffn_g, final_norm_g, w_mod, b_mod, w_in, w_out, w_fnet,
           hy_conv_w, hy_conv_b, hy_w1, hy_b1, hy_w2, hy_b2, hy_w3, hy_b3, hy_w_out, hy_freq, hy_bias,
           hg_lb, hg_norm_g, w_router, w_gate, w_up, w_down):
    bsz, seq, d = x.shape
    ctx_len = ctx.shape[1]
    depth = w_in.shape[0]

    p = jax.nn.softmax(hg_lb.astype(F32), axis=0)
    lbs4 = (jnp.cumsum(p, axis=0) - p[0:1]).reshape(depth, 2, 1, HG_WIDTH)
    w_in_bf = w_in.astype(BF16)
    w_out_bf = w_out.astype(BF16)
    w_router_t = jnp.swapaxes(w_router, 1, 2)
    g_mix3 = norm_mix_g.reshape(depth, 1, d)
    g_ffn3 = norm_ffn_g.reshape(depth, 1, d)
    gain3 = hg_norm_g.reshape(depth, 1, HG_WIDTH)
    b_mod3 = b_mod.reshape(depth, 1, 6 * d)
    conv_b3 = hy_conv_b.reshape(depth, 1, 3 * HY_WIDTH)
    hy_bias3 = hy_bias.reshape(depth, 1, HY_WIDTH)
    mlp = (_pad_to(hy_w1, (depth, HY_PAD, HY_PAD)), _pad_to(hy_b1.reshape(depth, 1, -1), (depth, 1, HY_PAD)),
           _pad_to(hy_w2, (depth, HY_PAD, HY_PAD)), _pad_to(hy_b2.reshape(depth, 1, -1), (depth, 1, HY_PAD)),
           _pad_to(hy_w3, (depth, HY_PAD, HY_PAD)), _pad_to(hy_b3.reshape(depth, 1, -1), (depth, 1, HY_PAD)),
           _pad_to(hy_w_out, (depth, HY_PAD, 2 * HY_WIDTH)), _pad_to(hy_freq, (depth, 8, HY_PAD)))
    max_decay = math.log(1e-2) / 0.3
    min_decay = math.log(1e-2) / 1.5
    deltas = jnp.linspace(min_decay, max_decay, HY_WIDTH, dtype=F32).reshape(1, HY_WIDTH)
    rows = 16
    cc = jnp.zeros((rows, d), F32).at[:bsz].set(c).at[bsz].set(c_ctx)

    tables = {}
    for n in {seq, ctx_len}:
        dft, dft_t = _hyena_dft(n)
        tables[n] = dict(fourier=_fourier_table(n), chan=_channel_dft(n), dft=dft, dft_t=dft_t, tri=_tri_incl(n))

    x_row = lambda b: b
    ctx_row = lambda b: bsz
    zero_state = jnp.zeros((bsz, HG_HEADS, HG_HEAD, HG_HEAD), F32)

    def mixers(hh, y_hg, n, l):
        t = tables[n]
        y_fn = _fourier(hh, _fn_prep(t["chan"], w_fnet, l), t["fourier"])
        kc, ks = _hy_filter(n, t["dft"], mlp, deltas, l)
        z, x0 = _hy_pre(hh, hy_conv_w, conv_b3, l)
        yc, ys = _hy_fwd(z, t["dft"], kc, ks)
        y_hy = _hy_inv(yc, ys, t["dft_t"], z, x0, hy_bias3, l)
        return y_fn, y_hy, y_hg

    def sublayers(xx, hh, y_hg, mod4, mrow, n, l, final_g=None):
        cap = EC_CAPACITY * n // N_EXPERTS
        xx, xm, aff = _out_route(xx, *mixers(hh, y_hg, n, l), w_out_bf, mod4, mrow, g_ffn3, w_router_t, l)
        pos = _topk(aff, tables[n]["tri"], cap)
        xs = _gather(xm, pos, cap)
        ys = _ffn(xs.reshape(N_EXPERTS, bsz * cap, d), w_gate, w_up, w_down, l)
        return _combine(ys.reshape(N_EXPERTS, bsz, cap, d), pos, aff, xx, mod4, mrow, cap, final_g)

    xc = ctx
    for l in range(depth):
        last = l == depth - 1
        mod4 = _modulation(cc, w_mod, b_mod3, l).reshape(rows, 6, 1, d)
        h = _in_proj(x, mod4, x_row, g_mix3, w_in_bf, l)
        hc = _in_proj(xc.reshape(1, bsz * ctx_len, d), mod4, ctx_row, g_mix3, w_in_bf, l).reshape(bsz, ctx_len, -1)
        o_cf, s_f = _hgrn(hc, lbs4, gain3, zero_state, l, False)
        y_hg_c, s_b = _hgrn(hc, lbs4, gain3, zero_state, l, True, o_fwd=o_cf)
        o_xf, _ = _hgrn(h, lbs4, gain3, s_f, l, False)
        y_hg_x, _ = _hgrn(h, lbs4, gain3, s_b, l, True, o_fwd=o_xf)
        x = sublayers(x, h, y_hg_x, mod4, x_row, seq, l, final_norm_g.reshape(1, d) if last else None)
        if not last:
            xc = sublayers(xc, hc, y_hg_c, mod4, ctx_row, ctx_len, l)
    return x
```

```python
import functools
import math

import jax
import jax.numpy as jnp
from jax import lax
from jax.experimental import pallas as pl
from jax.experimental.pallas import tpu as pltpu

F32 = jnp.float32
BF16 = jnp.bfloat16

D_MODEL = 2048
FN_WIDTH = 512
FN_GROUP = 128
HY_WIDTH = 512
HG_WIDTH = 1024
HG_HEAD = 128
HG_HEADS = HG_WIDTH // HG_HEAD
HG_F_MIN = 1e-6
IN_WIDTH = FN_WIDTH + 3 * HY_WIDTH + 5 * HG_WIDTH
HY_BANDS = 16
HY_PAD = 128
N_EXPERTS = 16
EC_CAPACITY = 2
EXPERT_FF = 1024
EPS = 1e-6

HG_CHUNK = 128
HG_CHUNKS_PER_STEP = 4
HG_BASE = 8
HG_FAST_MAX_LOG2 = 115.0
LOG2_E = math.log2(math.e)
VMEM_LIMIT = 56 * 1024 * 1024

_COL_Q, _COL_FF, _COL_FB, _COL_I, _COL_G = 2, 3, 4, 5, 6


def _params(*sem):
    return pltpu.CompilerParams(dimension_semantics=sem, vmem_limit_bytes=VMEM_LIMIT)


def _dot(a, b):
    return jnp.dot(a, b, preferred_element_type=F32)


def _dot_nt(a, b):
    return lax.dot_general(a, b, (((1,), (1,)), ((), ())), preferred_element_type=F32)


def _split2(x):
    hi = x.astype(BF16)
    lo = (x - hi.astype(F32)).astype(BF16)
    return hi, lo


def _dot3(a, b, dot=_dot):
    ah, al = _split2(a)
    bh, bl = _split2(b)
    return dot(ah, bh) + dot(ah, bl) + dot(al, bh)


def _silu(x):
    return x * jax.nn.sigmoid(x)


def _norm_mod(x, g, sh, sc):
    ms = jnp.mean(x * x, axis=-1, keepdims=True)
    return (x * lax.rsqrt(ms + EPS) * g) * (1.0 + sc) + sh


def _mod_kernel(a_ref, w_ref, b_ref, o_ref):
    a = _silu(a_ref[...]).astype(BF16)
    o_ref[...] = _dot(a, w_ref[...].astype(BF16)) + b_ref[...]


def _modulation(cc, w_mod, b_mod3, l):
    rows, d = cc.shape
    n = w_mod.shape[-1]
    tn = 1024
    return pl.pallas_call(
        _mod_kernel,
        grid=(n // tn,),
        in_specs=[
            pl.BlockSpec((rows, d), lambda j: (0, 0)),
            pl.BlockSpec((None, d, tn), lambda j: (l, 0, j)),
            pl.BlockSpec((None, 1, tn), lambda j: (l, 0, j)),
        ],
        out_specs=pl.BlockSpec((rows, tn), lambda j: (0, j)),
        out_shape=jax.ShapeDtypeStruct((rows, n), F32),
        compiler_params=_params("arbitrary"),
        name="modulation",
    )(cc, w_mod, b_mod3)


def _in_kernel(x_ref, g_ref, sh_ref, sc_ref, w_ref, o_ref, xm_ref):
    @pl.when(pl.program_id(2) == 0)
    def _():
        xm_ref[...] = _norm_mod(x_ref[...], g_ref[...], sh_ref[...], sc_ref[...]).astype(BF16)

    o_ref[...] = _dot(xm_ref[...], w_ref[...]).astype(o_ref.dtype)


def _in_proj(x, mod4, mrow, gamma3, w_in_bf, l):
    bsz, seq, d = x.shape
    n = w_in_bf.shape[-1]
    tm = min(seq, 1024)
    tn = 1024
    return pl.pallas_call(
        _in_kernel,
        grid=(bsz, seq // tm, n // tn),
        in_specs=[
            pl.BlockSpec((None, tm, d), lambda b, i, j: (b, i, 0)),
            pl.BlockSpec((None, 1, d), lambda b, i, j: (l, 0, 0)),
            pl.BlockSpec((None, None, 1, d), lambda b, i, j: (mrow(b), 0, 0, 0)),
            pl.BlockSpec((None, None, 1, d), lambda b, i, j: (mrow(b), 1, 0, 0)),
            pl.BlockSpec((None, d, tn), lambda b, i, j: (l, 0, j)),
        ],
        out_specs=pl.BlockSpec((None, tm, tn), lambda b, i, j: (b, i, j)),
        out_shape=jax.ShapeDtypeStruct((bsz, seq, n), BF16),
        scratch_shapes=[pltpu.VMEM((tm, d), BF16)],
        compiler_params=_params("arbitrary", "arbitrary", "arbitrary"),
        name="in_proj",
    )(x, gamma3, mod4, mod4, w_in_bf)


def _hg_kernel(rev, fuse_out, *refs):
    if fuse_out:
        (zq_ref, zf_ref, zi_ref, g_ref, of_ref, lb_ref, gain_ref, s0_ref, y_ref, st_ref,
         s_ref, qh_ref, kh_ref, oi_ref, q_ref, kk_ref, b_ref) = refs
    else:
        (zq_ref, zf_ref, zi_ref, lb_ref, s0_ref, y_ref, st_ref,
         s_ref, qh_ref, kh_ref, oi_ref, q_ref, kk_ref, b_ref) = refs
    C = HG_CHUNK
    W = HG_WIDTH
    R = zq_ref.shape[0]
    chunks = [slice(k * C, (k + 1) * C) for k in range(R // C)]
    if rev:
        chunks = chunks[::-1]
    heads = [slice(h * HG_HEAD, (h + 1) * HG_HEAD) for h in range(HG_HEADS)]
    c = pl.program_id(1)

    @pl.when(c == 0)
    def _():
        s_ref[...] = s0_ref[...]

    def scan_order_iota(n):
        row = lax.broadcasted_iota(jnp.int32, (n, n), 0)
        col = lax.broadcasted_iota(jnp.int32, (n, n), 1)
        return (n - 1 - row, n - 1 - col) if rev else (row, col)

    row, col = scan_order_iota(C)
    tri = jnp.where(col <= row, 1.0, 0.0)
    tri_bf = tri.astype(BF16)

    def ref_rows(b, s, r):
        n = b.shape[0]
        b3 = b.reshape(n // s, s, W)
        return jnp.broadcast_to(b3[:, r:r + 1, :], (n // s, s, W)).reshape(n, W)

    def mid_offset(b, s):
        hh = s // 2
        return b - ref_rows(b, s, hh if rev else hh - 1)

    lb = lb_ref[...]
    half = C // 2
    spread = None
    for rs in chunks:
        zf = zf_ref[rs, :].astype(F32)
        q = _silu(zq_ref[rs, :])
        f = jnp.maximum(lb + (1.0 - lb) * jax.nn.sigmoid(zf), HG_F_MIN)
        kk = (1.0 - f).astype(BF16)
        hi, lo = _split2(jnp.log(f) * LOG2_E)
        b = _dot(tri_bf, hi) + _dot(tri_bf, lo)
        btot = b[0:1, :] if rev else b[C - 1:C, :]
        qe = q * jnp.exp2(b).astype(BF16)
        kd = kk * jnp.exp2(btot - b).astype(BF16)
        sdec = jnp.exp2(btot)
        v_t = zi_ref[rs, :].T
        for h, hs in enumerate(heads):
            st = s_ref[h]
            oi_ref[rs, hs] = _dot_nt(qe[:, hs], st.astype(BF16))
            s_ref[h] = st * sdec[:, hs] + _dot(v_t[hs, :], kd[:, hs])
        q_ref[rs, :] = q
        kk_ref[rs, :] = kk
        b_ref[rs, :] = b
        m = jnp.max(jnp.abs(mid_offset(b, half)))
        spread = m if spread is None else jnp.maximum(spread, m)

    def intra_chunk(halvings, block):
        q = q_ref[...]
        kk = kk_ref[...]
        b = b_ref[...]
        for i, s in enumerate(halvings):
            w = jnp.exp2(-jnp.abs(mid_offset(b, s))).astype(BF16)
            qh_ref[i] = q * w
            kh_ref[i] = kk * w
        d = mid_offset(b, block)
        nh = len(halvings)
        qh_ref[nh] = q * jnp.exp2(d).astype(BF16)
        kh_ref[nh] = kk * jnp.exp2(-d).astype(BF16)
        level_mask = [jnp.where(((row // s) == (col // s)) & ((row % s) >= s // 2) & ((col % s) < s // 2), 1.0, 0.0)
                      for s in halvings]
        block_mask = jnp.where((row // block) == (col // block), tri, 0.0)

        pairs = [(rs, hs) for rs in chunks for hs in heads]
        scores = [[_dot_nt(qh_ref[i, rs, hs], kh_ref[i, rs, hs]) for i in range(nh + 1)] for rs, hs in pairs]
        for (rs, hs), m in zip(pairs, scores):
            a = jnp.where(block_mask > 0.5, m[nh], 0.0)
            for i in range(nh):
                a = a + m[i] * level_mask[i]
            o = oi_ref[rs, hs] + _dot(a.astype(BF16), zi_ref[rs, hs])
            if fuse_out:
                o = o + of_ref[rs, hs]
                ms = jnp.mean(o * o, axis=-1, keepdims=True)
                g = g_ref[rs, hs].astype(F32)
                y_ref[rs, hs] = (o * lax.rsqrt(ms + EPS) * gain_ref[:, hs] * _silu(g)).astype(y_ref.dtype)
            else:
                y_ref[rs, hs] = o

    in_range = spread <= HG_FAST_MAX_LOG2

    @pl.when(in_range)
    def _():
        intra_chunk([C], half)

    @pl.when(jnp.logical_not(in_range))
    def _():
        sizes = []
        s = C
        while s > HG_BASE:
            sizes.append(s)
            s //= 2
        intra_chunk(sizes, HG_BASE)

    @pl.when(c == pl.num_programs(1) - 1)
    def _():
        st_ref[...] = s_ref[...]


def _hgrn(h, lbs4, gain3, s0, l, rev, o_fwd=None):
    bsz, seq, _ = h.shape
    R = min(HG_CHUNKS_PER_STEP * HG_CHUNK, seq)
    nc = seq // R
    W = HG_WIDTH
    fuse_out = o_fwd is not None
    n_factor = (HG_CHUNK // HG_BASE).bit_length()
    cidx = (lambda c: nc - 1 - c) if rev else (lambda c: c)

    def hcol(k):
        return pl.BlockSpec((None, R, W), lambda b, c: (b, cidx(c), k))

    lb_spec = pl.BlockSpec((None, None, 1, W), lambda b, c: (l, 1 if rev else 0, 0, 0))
    s_spec = pl.BlockSpec((None, HG_HEADS, HG_HEAD, HG_HEAD), lambda b, c: (b, 0, 0, 0))
    o_spec = pl.BlockSpec((None, R, W), lambda b, c: (b, cidx(c), 0))
    if fuse_out:
        in_specs = [hcol(_COL_Q), hcol(_COL_FB if rev else _COL_FF), hcol(_COL_I), hcol(_COL_G), o_spec,
                    lb_spec, pl.BlockSpec((None, 1, W), lambda b, c: (l, 0, 0)), s_spec]
        args = (h, h, h, h, o_fwd, lbs4, gain3, s0)
        out_dtype = BF16
    else:
        in_specs = [hcol(_COL_Q), hcol(_COL_FB if rev else _COL_FF), hcol(_COL_I), lb_spec, s_spec]
        args = (h, h, h, lbs4, s0)
        out_dtype = F32
    return pl.pallas_call(
        functools.partial(_hg_kernel, rev, fuse_out),
        grid=(bsz, nc),
        in_specs=in_specs,
        out_specs=[o_spec, s_spec],
        out_shape=[jax.ShapeDtypeStruct((bsz, seq, W), out_dtype),
                   jax.ShapeDtypeStruct((bsz, HG_HEADS, HG_HEAD, HG_HEAD), F32)],
        scratch_shapes=[pltpu.VMEM((HG_HEADS, HG_HEAD, HG_HEAD), F32),
                        pltpu.VMEM((n_factor, R, W), BF16), pltpu.VMEM((n_factor, R, W), BF16),
                        pltpu.VMEM((R, W), F32),
                        pltpu.VMEM((R, W), BF16), pltpu.VMEM((R, W), BF16), pltpu.VMEM((R, W), F32)],
        compiler_params=_params("arbitrary", "arbitrary"),
        name="hgrn_bwd" if rev else "hgrn_fwd",
    )(*args)


def _fn_prep_kernel(cs_ref, w_ref, o_ref):
    w = w_ref[...]
    o_ref[:, :FN_WIDTH] = _dot3(cs_ref[0], w).astype(BF16)
    o_ref[:, FN_WIDTH:] = _dot3(cs_ref[1], w).astype(BF16)


def _fn_prep(chan_dft, w_fnet, l):
    return pl.pallas_call(
        _fn_prep_kernel,
        grid=(1,),
        in_specs=[pl.BlockSpec((2, FN_WIDTH, FN_WIDTH), lambda i: (0, 0, 0)),
                  pl.BlockSpec((None, FN_WIDTH, FN_WIDTH), lambda i: (l, 0, 0))],
        out_specs=pl.BlockSpec((FN_WIDTH, 2 * FN_WIDTH), lambda i: (0, 0)),
        out_shape=jax.ShapeDtypeStruct((FN_WIDTH, 2 * FN_WIDTH), BF16),
        compiler_params=_params("arbitrary"),
        name="fnet_prep",
    )(chan_dft, w_fnet)


def _fn_kernel(u_ref, wc_ref, t_ref, o_ref, p_ref):
    seq = u_ref.shape[0]
    rc = min(seq, 512)

    @pl.when(pl.program_id(1) == 0)
    def _():
        def rows(i, carry):
            r0 = pl.multiple_of(i * rc, rc)
            p = _dot(u_ref[pl.ds(r0, rc), :], wc_ref[...])
            p_ref[pl.ds(r0, rc), :] = p[:, :FN_WIDTH].astype(BF16)
            p_ref[pl.ds(pl.multiple_of(seq + r0, rc), rc), :] = p[:, FN_WIDTH:].astype(BF16)
            return carry

        lax.fori_loop(0, seq // rc, rows, 0)

    o_ref[...] = _dot(t_ref[...], p_ref[...]).astype(o_ref.dtype)


def _fourier(h, wc, table):
    bsz, seq, _ = h.shape
    tt = min(seq, 1024)
    return pl.pallas_call(
        _fn_kernel,
        grid=(bsz, seq // tt),
        in_specs=[pl.BlockSpec((None, seq, FN_WIDTH), lambda b, t: (b, 0, 0)),
                  pl.BlockSpec((FN_WIDTH, 2 * FN_WIDTH), lambda b, t: (0, 0)),
                  pl.BlockSpec((tt, 2 * seq), lambda b, t: (t, 0))],
        out_specs=pl.BlockSpec((None, tt, FN_WIDTH), lambda b, t: (b, t, 0)),
        out_shape=jax.ShapeDtypeStruct((bsz, seq, FN_WIDTH), BF16),
        scratch_shapes=[pltpu.VMEM((2 * seq, FN_WIDTH), BF16)],
        compiler_params=_params("arbitrary", "arbitrary"),
        name="fourier",
    )(h, wc, table)


def _hy_filter_kernel(seq, wc_ref, ws_ref, w1_ref, b1_ref, w2_ref, b2_ref, w3_ref, b3_ref, wo_ref, fr_ref,
                      dl_ref, kc_ref, ks_ref, h_ref):
    i = pl.program_id(0)
    tf = wc_ref.shape[0]
    nfft = 2 * seq

    @pl.when(i == 0)
    def _():
        pos = lax.broadcasted_iota(jnp.int32, (seq, HY_PAD), 0).astype(F32)
        lane = lax.broadcasted_iota(jnp.int32, (seq, HY_PAD), 1)
        t = pos / float(max(seq - 1, 1))
        w = (2.0 * math.pi) * pos / float(seq)
        band_id = jnp.where(lane <= HY_BANDS, lane - 1, lane - 1 - HY_BANDS).astype(F32)
        band = 1e-4 + band_id * ((HY_BANDS - 1 - 1e-4) / (HY_BANDS - 1))
        arg = band * w
        z = jnp.where(lane == 0, t,
                      jnp.where(lane <= HY_BANDS, jnp.cos(arg),
                                jnp.where(lane <= 2 * HY_BANDS, -jnp.sin(arg), 0.0)))
        fr = fr_ref[...]
        hdn = jnp.sin(fr[0:1] * (_dot3(z, w1_ref[...]) + b1_ref[...]))
        hdn = jnp.sin(fr[1:2] * (_dot3(hdn, w2_ref[...]) + b2_ref[...]))
        hdn = jnp.sin(fr[2:3] * (_dot3(hdn, w3_ref[...]) + b3_ref[...]))
        hf = _dot3(hdn, wo_ref[...])
        decay = jnp.exp(-t[:, 0:1] * jnp.abs(dl_ref[...]))
        first = lax.broadcasted_iota(jnp.int32, (seq, HY_WIDTH), 0) == 0
        h_ref[:, :HY_WIDTH] = (hf[:, :HY_WIDTH] * decay).astype(BF16)
        h_ref[:, HY_WIDTH:] = jnp.where(first, 0.0, hf[:, HY_WIDTH:] * decay).astype(BF16)

    gc = _dot(wc_ref[...], h_ref[...])
    gs = _dot(ws_ref[...], h_ref[...])
    first = (lax.broadcasted_iota(jnp.int32, (tf, HY_WIDTH), 0) + i * tf) == 0
    scale = jnp.where(first, 1.0 / nfft, 2.0 / nfft)
    kc_ref[...] = (gc[:, :HY_WIDTH] + gc[:, HY_WIDTH:]) * scale
    ks_ref[...] = jnp.where(first, gs[:, :HY_WIDTH] + gs[:, HY_WIDTH:], gs[:, :HY_WIDTH] - gs[:, HY_WIDTH:]) * scale


def _hy_filter(seq, dft, mlp, deltas, l):
    w1p, b1p, w2p, b2p, w3p, b3p, wop, frp = mlp
    tf = min(seq, 512)
    nf = seq // tf

    def full(a):
        shp = a.shape[1:]
        return pl.BlockSpec((None,) + shp, lambda i: (l,) + (0,) * len(shp))

    return pl.pallas_call(
        functools.partial(_hy_filter_kernel, seq),
        grid=(nf,),
        in_specs=[pl.BlockSpec((tf, seq), lambda i: (i, 0)),
                  pl.BlockSpec((tf, seq), lambda i: (i + nf, 0)),
                  full(w1p), full(b1p), full(w2p), full(b2p), full(w3p), full(b3p), full(wop), full(frp),
                  pl.BlockSpec((1, HY_WIDTH), lambda i: (0, 0))],
        out_specs=[pl.BlockSpec((tf, HY_WIDTH), lambda i: (i, 0)),
                   pl.BlockSpec((tf, HY_WIDTH), lambda i: (i, 0))],
        out_shape=[jax.ShapeDtypeStruct((seq, HY_WIDTH), F32), jax.ShapeDtypeStruct((seq, HY_WIDTH), F32)],
        scratch_shapes=[pltpu.VMEM((seq, 2 * HY_WIDTH), BF16)],
        compiler_params=_params("arbitrary"),
        name="hyena_filter",
    )(dft, dft, w1p, b1p, w2p, b2p, w3p, b3p, wop, frp, deltas)


def _hy_pre_kernel(uv_ref, u1_ref, u0_ref, cw_ref, cb_ref, z_ref, x0_ref):
    seq = uv_ref.shape[0]
    rowi = lax.broadcasted_iota(jnp.int32, (seq, 128), 0)
    j = pl.program_id(1)

    def conv(u_ref, part):
        u = u_ref[...].astype(F32)
        prev = jnp.where(rowi == 0, 0.0, pltpu.roll(u, 1, 0))
        nxt = jnp.where(rowi == seq - 1, 0.0, pltpu.roll(u, seq - 1, 0))
        cs = pl.ds(pl.multiple_of(part * HY_WIDTH + j * 128, 128), 128)
        return prev * cw_ref[0:1, cs] + u * cw_ref[1:2, cs] + nxt * cw_ref[2:3, cs] + cb_ref[:, cs]

    z_ref[...] = (conv(u1_ref, 1) * conv(uv_ref, 0)).astype(BF16)
    x0_ref[...] = conv(u0_ref, 2).astype(BF16)


def _hy_pre(h, conv_w, conv_b3, l):
    bsz, seq, _ = h.shape
    lanes = 128
    nj = HY_WIDTH // lanes
    off = FN_WIDTH // lanes

    def part(p):
        return pl.BlockSpec((None, seq, lanes), lambda b, j: (b, 0, off + p * nj + j))

    o_spec = pl.BlockSpec((None, seq, lanes), lambda b, j: (b, 0, j))
    return pl.pallas_call(
        _hy_pre_kernel,
        grid=(bsz, nj),
        in_specs=[part(0), part(1), part(2),
                  pl.BlockSpec((None, 3, 3 * HY_WIDTH), lambda b, j: (l, 0, 0)),
                  pl.BlockSpec((None, 1, 3 * HY_WIDTH), lambda b, j: (l, 0, 0))],
        out_specs=[o_spec, o_spec],
        out_shape=[jax.ShapeDtypeStruct((bsz, seq, HY_WIDTH), BF16)] * 2,
        compiler_params=_params("arbitrary", "arbitrary"),
        name="hyena_pre",
    )(h, h, h, conv_w, conv_b3)


def _hy_fwd_kernel(wc_ref, ws_ref, z_ref, kc_ref, ks_ref, yc_ref, ys_ref):
    tf = wc_ref.shape[0]
    z = z_ref[...]
    uc = _dot(wc_ref[...], z)
    us = _dot(ws_ref[...], z)
    kc = kc_ref[...]
    ks = ks_ref[...]
    first = (lax.broadcasted_iota(jnp.int32, (tf, HY_WIDTH), 0) + pl.program_id(0) * tf) == 0
    ss = us * ks
    yc_ref[...] = (uc * kc - jnp.where(first, 0.0, ss)).astype(BF16)
    ys_ref[...] = jnp.where(first, ss, uc * ks + us * kc).astype(BF16)


def _hy_fwd(z, dft, kc, ks):
    bsz, seq, _ = z.shape
    tf = min(seq, 1024)
    nf = seq // tf
    k_spec = pl.BlockSpec((tf, HY_WIDTH), lambda i, b: (i, 0))
    y_spec = pl.BlockSpec((None, tf, HY_WIDTH), lambda i, b: (b, i, 0))
    return pl.pallas_call(
        _hy_fwd_kernel,
        grid=(nf, bsz),
        in_specs=[pl.BlockSpec((tf, seq), lambda i, b: (i, 0)),
                  pl.BlockSpec((tf, seq), lambda i, b: (i + nf, 0)),
                  pl.BlockSpec((None, seq, HY_WIDTH), lambda i, b: (b, 0, 0)),
                  k_spec, k_spec],
        out_specs=[y_spec, y_spec],
        out_shape=[jax.ShapeDtypeStruct((bsz, seq, HY_WIDTH), BF16)] * 2,
        compiler_params=_params("arbitrary", "arbitrary"),
        name="hyena_dft",
    )(dft, dft, z, kc, ks)


def _hy_inv_kernel(tc_ref, ts_ref, yc_ref, ys_ref, z_ref, x0_ref, db_ref, o_ref):
    y = _dot(tc_ref[...], yc_ref[...]) + _dot(ts_ref[...], ys_ref[...])
    z = z_ref[...].astype(F32)
    o_ref[...] = (x0_ref[...].astype(F32) * (y + z * db_ref[...])).astype(BF16)


def _hy_inv(yc, ys, dft_t, z, x0, hy_bias3, l):
    bsz, seq, _ = z.shape
    tt = min(seq, 1024)
    y_spec = pl.BlockSpec((None, seq, HY_WIDTH), lambda t, b: (b, 0, 0))
    r_spec = pl.BlockSpec((None, tt, HY_WIDTH), lambda t, b: (b, t, 0))
    return pl.pallas_call(
        _hy_inv_kernel,
        grid=(seq // tt, bsz),
        in_specs=[pl.BlockSpec((tt, seq), lambda t, b: (t, 0)),
                  pl.BlockSpec((tt, seq), lambda t, b: (t, 1)),
                  y_spec, y_spec, r_spec, r_spec,
                  pl.BlockSpec((None, 1, HY_WIDTH), lambda t, b: (l, 0, 0))],
        out_specs=r_spec,
        out_shape=jax.ShapeDtypeStruct((bsz, seq, HY_WIDTH), BF16),
        compiler_params=_params("arbitrary", "arbitrary"),
        name="hyena_idft",
    )(dft_t, dft_t, yc, ys, z, x0, hy_bias3)


def _out_route_kernel(yf_ref, yh_ref, yg_ref, wf_ref, wh_ref, wg_ref, x_ref, g1_ref, gam_ref, sh_ref, sc_ref,
                      wr_ref, xo_ref, xm_ref, aff_ref):
    mix = _dot(yf_ref[...], wf_ref[...]) + _dot(yh_ref[...], wh_ref[...]) + _dot(yg_ref[...], wg_ref[...])
    x = x_ref[...] + g1_ref[...] * mix
    xo_ref[...] = x
    xm = _norm_mod(x, gam_ref[...], sh_ref[...], sc_ref[...])
    xm_ref[...] = xm.astype(BF16)
    logits = _dot3(wr_ref[...], xm, dot=_dot_nt)
    mx = jnp.max(logits, axis=0, keepdims=True)
    ex = jnp.exp(logits - mx)
    aff_ref[...] = ex / jnp.sum(ex, axis=0, keepdims=True)


def _out_route(x, y_fn, y_hy, y_hg, w_out_bf, mod4, mrow, gamma3, w_router_t, l):
    bsz, seq, d = x.shape
    tm = min(seq, 512)
    half = FN_WIDTH

    def mod_row(k):
        return pl.BlockSpec((None, None, 1, d), lambda b, i: (mrow(b), k, 0, 0))

    return pl.pallas_call(
        _out_route_kernel,
        grid=(bsz, seq // tm),
        in_specs=[pl.BlockSpec((None, tm, half), lambda b, i: (b, i, 0)),
                  pl.BlockSpec((None, tm, half), lambda b, i: (b, i, 0)),
                  pl.BlockSpec((None, tm, HG_WIDTH), lambda b, i: (b, i, 0)),
                  pl.BlockSpec((None, half, d), lambda b, i: (l, 0, 0)),
                  pl.BlockSpec((None, half, d), lambda b, i: (l, 1, 0)),
                  pl.BlockSpec((None, HG_WIDTH, d), lambda b, i: (l, 1, 0)),
                  pl.BlockSpec((None, tm, d), lambda b, i: (b, i, 0)),
                  mod_row(2),
                  pl.BlockSpec((None, 1, d), lambda b, i: (l, 0, 0)),
                  mod_row(3), mod_row(4),
                  pl.BlockSpec((None, N_EXPERTS, d), lambda b, i: (l, 0, 0))],
        out_specs=[pl.BlockSpec((None, tm, d), lambda b, i: (b, i, 0)),
                   pl.BlockSpec((None, tm, d), lambda b, i: (b, i, 0)),
                   pl.BlockSpec((None, N_EXPERTS, tm), lambda b, i: (b, 0, i))],
        out_shape=[jax.ShapeDtypeStruct((bsz, seq, d), F32),
                   jax.ShapeDtypeStruct((bsz, seq, d), BF16),
                   jax.ShapeDtypeStruct((bsz, N_EXPERTS, seq), F32)],
        compiler_params=_params("arbitrary", "arbitrary"),
        name="out_route",
    )(y_fn, y_hy, y_hg, w_out_bf, w_out_bf, w_out_bf, x, mod4, gamma3, mod4, mod4, w_router_t)


def _topk_kernel(cap, aff_ref, tri_ref, pos_ref):
    a = aff_ref[...]

    def count(mask):
        return jnp.sum(jnp.where(mask, 1.0, 0.0), axis=1, keepdims=True)

    def as_float(bits):
        return pltpu.bitcast(jnp.broadcast_to(bits, a.shape), F32)

    def step(i, thr_bits):
        cand = thr_bits | jnp.left_shift(jnp.int32(1), 30 - i)
        return jnp.where(count(a >= as_float(cand)) >= cap, cand, thr_bits)

    thr = as_float(lax.fori_loop(0, 31, step, jnp.zeros((a.shape[0], 1), jnp.int32)))
    above = a > thr
    tie = a == thr
    room = cap - count(above)
    tie_rank = _dot(jnp.where(tie, 1.0, 0.0).astype(BF16), tri_ref[...])
    sel = jnp.where(above, 1.0, jnp.where(tie, jnp.where(tie_rank <= room, 1.0, 0.0), 0.0))
    slot = _dot(sel.astype(BF16), tri_ref[...]) - 1.0
    pos_ref[...] = jnp.where(sel > 0.5, slot, -1.0).astype(jnp.int32)


def _topk(aff, tri_incl, cap):
    bsz, ne, seq = aff.shape
    return pl.pallas_call(
        functools.partial(_topk_kernel, cap),
        grid=(bsz,),
        in_specs=[pl.BlockSpec((None, ne, seq), lambda b: (b, 0, 0)),
                  pl.BlockSpec((seq, seq), lambda b: (0, 0))],
        out_specs=pl.BlockSpec((None, ne, seq), lambda b: (b, 0, 0)),
        out_shape=jax.ShapeDtypeStruct((bsz, ne, seq), jnp.int32),
        compiler_params=_params("arbitrary"),
        name="moe_topk",
    )(aff, tri_incl)


def _gather_kernel(cap, xm_ref, pos_ref, o_ref):
    seq = xm_ref.shape[0]
    pos = pos_ref[pl.ds(pl.program_id(1), 1), :]
    slot = lax.broadcasted_iota(jnp.int32, (cap, seq), 0)
    onehot = jnp.where(slot == pos, 1.0, 0.0).astype(BF16)
    o_ref[...] = _dot(onehot, xm_ref[...]).astype(BF16)


def _gather(xm, pos, cap):
    bsz, seq, d = xm.shape
    return pl.pallas_call(
        functools.partial(_gather_kernel, cap),
        grid=(bsz, N_EXPERTS),
        in_specs=[pl.BlockSpec((None, seq, d), lambda b, e: (b, 0, 0)),
                  pl.BlockSpec((None, N_EXPERTS, seq), lambda b, e: (b, 0, 0))],
        out_specs=pl.BlockSpec((None, None, cap, d), lambda b, e: (e, b, 0, 0)),
        out_shape=jax.ShapeDtypeStruct((N_EXPERTS, bsz, cap, d), BF16),
        compiler_params=_params("arbitrary", "arbitrary"),
        name="moe_gather",
    )(xm, pos)


def _ffn_kernel(xs_ref, wg_ref, wu_ref, wd_ref, o_ref, acc_ref):
    j = pl.program_id(2)

    @pl.when((pl.program_id(0) == 0) & (pl.program_id(1) == 0) & (j == 0))
    def _():
        acc_ref[...] = jnp.zeros_like(acc_ref)

    xs = xs_ref[...]
    hid = _silu(_dot(xs, wg_ref[...].astype(BF16))) * _dot(xs, wu_ref[...].astype(BF16))
    part = _dot(hid.astype(BF16), wd_ref[...].astype(BF16))
    total = part + jnp.where(j > 0, acc_ref[...], 0.0)
    acc_ref[...] = total
    o_ref[...] = total.astype(BF16)


def _ffn(xs, w_gate, w_up, w_down, l):
    ne, rows, d = xs.shape
    ff = w_gate.shape[-1]
    tm = min(rows, 1024)
    tj = 512
    return pl.pallas_call(
        _ffn_kernel,
        grid=(ne, rows // tm, ff // tj),
        in_specs=[pl.BlockSpec((None, tm, d), lambda e, m, j: (e, m, 0)),
                  pl.BlockSpec((None, None, d, tj), lambda e, m, j: (l, e, 0, j)),
                  pl.BlockSpec((None, None, d, tj), lambda e, m, j: (l, e, 0, j)),
                  pl.BlockSpec((None, None, tj, d), lambda e, m, j: (l, e, j, 0))],
        out_specs=pl.BlockSpec((None, tm, d), lambda e, m, j: (e, m, 0)),
        out_shape=jax.ShapeDtypeStruct((ne, rows, d), BF16),
        scratch_shapes=[pltpu.VMEM((tm, d), F32)],
        compiler_params=_params("arbitrary", "arbitrary", "arbitrary"),
        name="moe_ffn",
    )(xs, w_gate, w_up, w_down)


def _combine_kernel(cap, final_norm, *refs):
    if final_norm:
        ys_ref, pos_ref, aff_ref, x_ref, g_ref, fg_ref, o_ref, pos_t_ref, gate_t_ref, w_ref = refs
    else:
        ys_ref, pos_ref, aff_ref, x_ref, g_ref, o_ref, pos_t_ref, gate_t_ref, w_ref = refs
    i = pl.program_id(1)
    tm = x_ref.shape[0]
    ne = pos_ref.shape[0]
    last_tile = pl.num_programs(1) - 1

    def build_weights(tile, buf):
        r0 = pl.multiple_of(tile * tm, tm)
        pos = pos_t_ref[pl.ds(r0, tm), :]
        gate = gate_t_ref[pl.ds(r0, tm), :]
        if cap % 128 == 0:
            lane = lax.broadcasted_iota(jnp.int32, (tm, cap), 1).astype(F32)
            for e in range(ne):
                w_ref[buf, :, e * cap:(e + 1) * cap] = jnp.where(
                    lane == pos[:, e:e + 1], gate[:, e:e + 1], 0.0).astype(BF16)
        else:
            lane = lax.broadcasted_iota(jnp.int32, (tm, ne * cap), 1).astype(F32)
            w = jnp.zeros((tm, ne * cap), F32)
            for e in range(ne):
                hit = jnp.logical_and(lane == pos[:, e:e + 1] + float(e * cap), pos[:, e:e + 1] >= 0.0)
                w = w + jnp.where(hit, gate[:, e:e + 1], 0.0)
            w_ref[buf] = w.astype(BF16)

    @pl.when(i == 0)
    def _():
        pos_t_ref[...] = pos_ref[...].astype(F32).T
        gate_t_ref[...] = aff_ref[...].T
        build_weights(0, 0)

    build_weights(jnp.minimum(i + 1, last_tile), (i + 1) % 2)
    ys = ys_ref[...].reshape(ne * cap, ys_ref.shape[-1])
    x = x_ref[...] + g_ref[...] * _dot(w_ref[i % 2], ys)
    if final_norm:
        ms = jnp.mean(x * x, axis=-1, keepdims=True)
        x = x * lax.rsqrt(ms + EPS) * fg_ref[...]
    o_ref[...] = x


def _combine(ys, pos, aff, x, mod4, mrow, cap, final_g=None):
    bsz, seq, d = x.shape
    ne = pos.shape[1]
    tm = min(seq, 256)
    final_norm = final_g is not None
    in_specs = [pl.BlockSpec((ne, None, cap, d), lambda b, i: (0, b, 0, 0)),
                pl.BlockSpec((None, ne, seq), lambda b, i: (b, 0, 0)),
                pl.BlockSpec((None, ne, seq), lambda b, i: (b, 0, 0)),
                pl.BlockSpec((None, tm, d), lambda b, i: (b, i, 0)),
                pl.BlockSpec((None, None, 1, d), lambda b, i: (mrow(b), 5, 0, 0))]
    args = (ys, pos, aff, x, mod4)
    if final_norm:
        in_specs.append(pl.BlockSpec((1, d), lambda b, i: (0, 0)))
        args += (final_g,)
    return pl.pallas_call(
        functools.partial(_combine_kernel, cap, final_norm),
        grid=(bsz, seq // tm),
        in_specs=in_specs,
        out_specs=pl.BlockSpec((None, tm, d), lambda b, i: (b, i, 0)),
        out_shape=jax.ShapeDtypeStruct((bsz, seq, d), F32),
        scratch_shapes=[pltpu.VMEM((seq, ne), F32), pltpu.VMEM((seq, ne), F32),
                        pltpu.VMEM((2, tm, ne * cap), BF16)],
        compiler_params=_params("arbitrary", "arbitrary"),
        name="moe_combine",
    )(*args)


def _angles(row_ids, cols, n):
    c = lax.broadcasted_iota(jnp.int32, (row_ids.shape[0], cols), 1)
    return ((row_ids * c) % n).astype(F32) * (2.0 * math.pi / n)


def _cos_sin(rows, cols, n):
    step = min(rows, 32)
    hi = _angles(jnp.arange(0, rows, step, dtype=jnp.int32)[:, None], cols, n)[:, None, :]
    lo = _angles(jnp.arange(step, dtype=jnp.int32)[:, None], cols, n)[None, :, :]
    cos = jnp.cos(hi) * jnp.cos(lo) - jnp.sin(hi) * jnp.sin(lo)
    sin = jnp.sin(hi) * jnp.cos(lo) + jnp.cos(hi) * jnp.sin(lo)
    return cos.reshape(rows, cols), sin.reshape(rows, cols)


def _fourier_table(seq):
    cos, sin = _cos_sin(seq, seq, seq)
    return jnp.concatenate([cos, -sin], axis=1).astype(BF16)


def _channel_dft(seq):
    ang = _angles(jnp.arange(FN_GROUP, dtype=jnp.int32)[:, None], FN_GROUP, FN_GROUP)
    scale = 1.0 / math.sqrt(seq * FN_GROUP)
    eye = jnp.eye(FN_WIDTH // FN_GROUP, dtype=F32)
    return jnp.stack([jnp.kron(eye, jnp.cos(ang) * scale), jnp.kron(eye, jnp.sin(ang) * scale)])


def _hyena_dft(seq):
    cos, sin = _cos_sin(seq, seq, 2 * seq)
    r = lax.broadcasted_iota(jnp.int32, (seq, seq), 0)
    c = lax.broadcasted_iota(jnp.int32, (seq, seq), 1)
    dft = jnp.concatenate([cos, jnp.where(r == 0, (1 - 2 * (c % 2)).astype(F32), sin)], axis=0).astype(BF16)
    dft_t = jnp.concatenate([cos, jnp.where(c == 0, (1 - 2 * (r % 2)).astype(F32), sin)], axis=1).astype(BF16)
    return dft, dft_t


def _tri_incl(seq):
    r = lax.broadcasted_iota(jnp.int32, (seq, seq), 0)
    c = lax.broadcasted_iota(jnp.int32, (seq, seq), 1)
    return (r <= c).astype(BF16)


def _pad_to(a, shape):
    return jnp.pad(a, [(0, t - s) for s, t in zip(a.shape, shape)])


def kernel(x, c, ctx, c_ctx, norm_mix_g, norm_ffn_g, final_norm_g, w_mod, b_mod, w_in, w_out, w_fnet,
           hy_conv_w, hy_conv_b, hy_w1, hy_b1, hy_w2, hy_b2, hy_w3, hy_b3, hy_w_out, hy_freq, hy_bias,
           hg_lb, hg_norm_g, w_router, w_gate, w_up, w_down):
    bsz, seq, d = x.shape
    ctx_len = ctx.shape[1]
    depth = w_in.shape[0]

    p = jax.nn.softmax(hg_lb.astype(F32), axis=0)
    lbs4 = (jnp.cumsum(p, axis=0) - p[0:1]).reshape(depth, 2, 1, HG_WIDTH)
    w_in_bf = w_in.astype(BF16)
    w_out_bf = w_out.astype(BF16)
    w_router_t = jnp.swapaxes(w_router, 1, 2)
    g_mix3 = norm_mix_g.reshape(depth, 1, d)
    g_ffn3 = norm_ffn_g.reshape(depth, 1, d)
    gain3 = hg_norm_g.reshape(depth, 1, HG_WIDTH)
    b_mod3 = b_mod.reshape(depth, 1, 6 * d)
    conv_b3 = hy_conv_b.reshape(depth, 1, 3 * HY_WIDTH)
    hy_bias3 = hy_bias.reshape(depth, 1, HY_WIDTH)
    mlp = (_pad_to(hy_w1, (depth, HY_PAD, HY_PAD)), _pad_to(hy_b1.reshape(depth, 1, -1), (depth, 1, HY_PAD)),
           _pad_to(hy_w2, (depth, HY_PAD, HY_PAD)), _pad_to(hy_b2.reshape(depth, 1, -1), (depth, 1, HY_PAD)),
           _pad_to(hy_w3, (depth, HY_PAD, HY_PAD)), _pad_to(hy_b3.reshape(depth, 1, -1), (depth, 1, HY_PAD)),
           _pad_to(hy_w_out, (depth, HY_PAD, 2 * HY_WIDTH)), _pad_to(hy_freq, (depth, 8, HY_PAD)))
    max_decay = math.log(1e-2) / 0.3
    min_decay = math.log(1e-2) / 1.5
    deltas = jnp.linspace(min_decay, max_decay, HY_WIDTH, dtype=F32).reshape(1, HY_WIDTH)
    rows = 16
    cc = jnp.zeros((rows, d), F32).at[:bsz].set(c).at[bsz].set(c_ctx)

    tables = {}
    for n in {seq, ctx_len}:
        dft, dft_t = _hyena_dft(n)
        tables[n] = dict(fourier=_fourier_table(n), chan=_channel_dft(n), dft=dft, dft_t=dft_t, tri=_tri_incl(n))

    x_row = lambda b: b
    ctx_row = lambda b: bsz
    zero_state = jnp.zeros((bsz, HG_HEADS, HG_HEAD, HG_HEAD), F32)

    def mixers(hh, y_hg, n, l):
        t = tables[n]
        y_fn = _fourier(hh, _fn_prep(t["chan"], w_fnet, l), t["fourier"])
        kc, ks = _hy_filter(n, t["dft"], mlp, deltas, l)
        z, x0 = _hy_pre(hh, hy_conv_w, conv_b3, l)
        yc, ys = _hy_fwd(z, t["dft"], kc, ks)
        y_hy = _hy_inv(yc, ys, t["dft_t"], z, x0, hy_bias3, l)
        return y_fn, y_hy, y_hg

    def sublayers(xx, hh, y_hg, mod4, mrow, n, l, final_g=None):
        cap = EC_CAPACITY * n // N_EXPERTS
        xx, xm, aff = _out_route(xx, *mixers(hh, y_hg, n, l), w_out_bf, mod4, mrow, g_ffn3, w_router_t, l)
        pos = _topk(aff, tables[n]["tri"], cap)
        xs = _gather(xm, pos, cap)
        ys = _ffn(xs.reshape(N_EXPERTS, bsz * cap, d), w_gate, w_up, w_down, l)
        return _combine(ys.reshape(N_EXPERTS, bsz, cap, d), pos, aff, xx, mod4, mrow, cap, final_g)

    xc = ctx
    for l in range(depth):
        last = l == depth - 1
        mod4 = _modulation(cc, w_mod, b_mod3, l).reshape(rows, 6, 1, d)
        h = _in_proj(x, mod4, x_row, g_mix3, w_in_bf, l)
        hc = _in_proj(xc.reshape(1, bsz * ctx_len, d), mod4, ctx_row, g_mix3, w_in_bf, l).reshape(bsz, ctx_len, -1)
        o_cf, s_f = _hgrn(hc, lbs4, gain3, zero_state, l, False)
        y_hg_c, s_b = _hgrn(hc, lbs4, gain3, zero_state, l, True, o_fwd=o_cf)
        o_xf, _ = _hgrn(h, lbs4, gain3, s_f, l, False)
        y_hg_x, _ = _hgrn(h, lbs4, gain3, s_b, l, True, o_fwd=o_xf)
        x = sublayers(x, h, y_hg_x, mod4, x_row, seq, l, final_norm_g.reshape(1, d) if last else None)
        if not last:
            xc = sublayers(xc, hc, y_hg_c, mod4, ctx_row, ctx_len, l)
    return x
```

```python
import functools
import math

import jax
import jax.numpy as jnp
from jax import lax
from jax.experimental import pallas as pl
from jax.experimental.pallas import tpu as pltpu

F32 = jnp.float32
BF16 = jnp.bfloat16

D_MODEL = 2048
FN_WIDTH = 512
FN_GROUP = 128
HY_WIDTH = 512
HG_WIDTH = 1024
HG_HEAD = 128
HG_HEADS = HG_WIDTH // HG_HEAD
HG_F_MIN = 1e-6
IN_WIDTH = FN_WIDTH + 3 * HY_WIDTH + 5 * HG_WIDTH
HY_BANDS = 16
HY_PAD = 128
N_EXPERTS = 16
EC_CAPACITY = 2
EXPERT_FF = 1024
EPS = 1e-6

HG_CHUNK = 128
HG_CHUNKS_PER_STEP = 4
HG_BASE = 8
HG_FAST_MAX_LOG2 = 115.0
LOG2_E = math.log2(math.e)
VMEM_LIMIT = 56 * 1024 * 1024

_COL_Q, _COL_FF, _COL_FB, _COL_I, _COL_G = 2, 3, 4, 5, 6


def _params(*sem):
    return pltpu.CompilerParams(dimension_semantics=sem, vmem_limit_bytes=VMEM_LIMIT)


def _dot(a, b):
    return jnp.dot(a, b, preferred_element_type=F32)


def _dot_nt(a, b):
    return lax.dot_general(a, b, (((1,), (1,)), ((), ())), preferred_element_type=F32)


def _split2(x):
    hi = x.astype(BF16)
    lo = (x - hi.astype(F32)).astype(BF16)
    return hi, lo


def _dot3(a, b, dot=_dot):
    ah, al = _split2(a)
    bh, bl = _split2(b)
    return dot(ah, bh) + dot(ah, bl) + dot(al, bh)


def _silu(x):
    return x * jax.nn.sigmoid(x)


def _norm_mod(x, g, sh, sc):
    ms = jnp.mean(x * x, axis=-1, keepdims=True)
    return (x * lax.rsqrt(ms + EPS) * g) * (1.0 + sc) + sh


def _mod_kernel(a_ref, w_ref, b_ref, o_ref):
    a = _silu(a_ref[...]).astype(BF16)
    o_ref[...] = _dot(a, w_ref[...].astype(BF16)) + b_ref[...]


def _modulation(cc, w_mod, b_mod3, l):
    rows, d = cc.shape
    n = w_mod.shape[-1]
    tn = 1024
    return pl.pallas_call(
        _mod_kernel,
        grid=(n // tn,),
        in_specs=[
            pl.BlockSpec((rows, d), lambda j: (0, 0)),
            pl.BlockSpec((None, d, tn), lambda j: (l, 0, j)),
            pl.BlockSpec((None, 1, tn), lambda j: (l, 0, j)),
        ],
        out_specs=pl.BlockSpec((rows, tn), lambda j: (0, j)),
        out_shape=jax.ShapeDtypeStruct((rows, n), F32),
        compiler_params=_params("arbitrary"),
        name="modulation",
    )(cc, w_mod, b_mod3)


def _in_kernel(x_ref, g_ref, sh_ref, sc_ref, w_ref, o_ref, xm_ref):
    @pl.when(pl.program_id(2) == 0)
    def _():
        xm_ref[...] = _norm_mod(x_ref[...], g_ref[...], sh_ref[...], sc_ref[...]).astype(BF16)

    o_ref[...] = _dot(xm_ref[...], w_ref[...]).astype(o_ref.dtype)


def _in_proj(x, mod4, mrow, gamma3, w_in_bf, l):
    bsz, seq, d = x.shape
    n = w_in_bf.shape[-1]
    tm = min(seq, 1024)
    tn = 1792
    return pl.pallas_call(
        _in_kernel,
        grid=(bsz, seq // tm, n // tn),
        in_specs=[
            pl.BlockSpec((None, tm, d), lambda b, i, j: (b, i, 0)),
            pl.BlockSpec((None, 1, d), lambda b, i, j: (l, 0, 0)),
            pl.BlockSpec((None, None, 1, d), lambda b, i, j: (mrow(b), 0, 0, 0)),
            pl.BlockSpec((None, None, 1, d), lambda b, i, j: (mrow(b), 1, 0, 0)),
            pl.BlockSpec((None, d, tn), lambda b, i, j: (l, 0, j)),
        ],
        out_specs=pl.BlockSpec((None, tm, tn), lambda b, i, j: (b, i, j)),
        out_shape=jax.ShapeDtypeStruct((bsz, seq, n), BF16),
        scratch_shapes=[pltpu.VMEM((tm, d), BF16)],
        compiler_params=_params("arbitrary", "arbitrary", "arbitrary"),
        name="in_proj",
    )(x, gamma3, mod4, mod4, w_in_bf)


def _hg_kernel(rev, fuse_out, *refs):
    if fuse_out:
        (zq_ref, zf_ref, zi_ref, g_ref, of_ref, lb_ref, gain_ref, s0_ref, y_ref, st_ref,
         s_ref, qh_ref, kh_ref, oi_ref, q_ref, kk_ref, b_ref) = refs
    else:
        (zq_ref, zf_ref, zi_ref, lb_ref, s0_ref, y_ref, st_ref,
         s_ref, qh_ref, kh_ref, oi_ref, q_ref, kk_ref, b_ref) = refs
    C = HG_CHUNK
    W = HG_WIDTH
    R = zq_ref.shape[0]
    chunks = [slice(k * C, (k + 1) * C) for k in range(R // C)]
    if rev:
        chunks = chunks[::-1]
    heads = [slice(h * HG_HEAD, (h + 1) * HG_HEAD) for h in range(HG_HEADS)]
    c = pl.program_id(1)

    @pl.when(c == 0)
    def _():
        s_ref[...] = s0_ref[...]

    def scan_order_iota(n):
        row = lax.broadcasted_iota(jnp.int32, (n, n), 0)
        col = lax.broadcasted_iota(jnp.int32, (n, n), 1)
        return (n - 1 - row, n - 1 - col) if rev else (row, col)

    row, col = scan_order_iota(C)
    tri = jnp.where(col <= row, 1.0, 0.0)
    tri_bf = tri.astype(BF16)

    def ref_rows(b, s, r):
        n = b.shape[0]
        b3 = b.reshape(n // s, s, W)
        return jnp.broadcast_to(b3[:, r:r + 1, :], (n // s, s, W)).reshape(n, W)

    def mid_offset(b, s):
        hh = s // 2
        return b - ref_rows(b, s, hh if rev else hh - 1)

    lb = lb_ref[...]
    half = C // 2
    spread = None
    for rs in chunks:
        zf = zf_ref[rs, :].astype(F32)
        q = _silu(zq_ref[rs, :])
        f = jnp.maximum(lb + (1.0 - lb) * jax.nn.sigmoid(zf), HG_F_MIN)
        kk = (1.0 - f).astype(BF16)
        hi, lo = _split2(jnp.log(f) * LOG2_E)
        b = _dot(tri_bf, hi) + _dot(tri_bf, lo)
        btot = b[0:1, :] if rev else b[C - 1:C, :]
        qe = q * jnp.exp2(b).astype(BF16)
        kd = kk * jnp.exp2(btot - b).astype(BF16)
        sdec = jnp.exp2(btot)
        v_t = zi_ref[rs, :].T
        for h, hs in enumerate(heads):
            st = s_ref[h]
            oi_ref[rs, hs] = _dot_nt(qe[:, hs], st.astype(BF16))
            s_ref[h] = st * sdec[:, hs] + _dot(v_t[hs, :], kd[:, hs])
        q_ref[rs, :] = q
        kk_ref[rs, :] = kk
        b_ref[rs, :] = b
        m = jnp.max(jnp.abs(mid_offset(b, half)))
        spread = m if spread is None else jnp.maximum(spread, m)

    def intra_chunk(halvings, block):
        q = q_ref[...]
        kk = kk_ref[...]
        b = b_ref[...]
        for i, s in enumerate(halvings):
            w = jnp.exp2(-jnp.abs(mid_offset(b, s))).astype(BF16)
            qh_ref[i] = q * w
            kh_ref[i] = kk * w
        d = mid_offset(b, block)
        nh = len(halvings)
        qh_ref[nh] = q * jnp.exp2(d).astype(BF16)
        kh_ref[nh] = kk * jnp.exp2(-d).astype(BF16)
        level_mask = [jnp.where(((row // s) == (col // s)) & ((row % s) >= s // 2) & ((col % s) < s // 2), 1.0, 0.0)
                      for s in halvings]
        block_mask = jnp.where((row // block) == (col // block), tri, 0.0)

        pairs = [(rs, hs) for rs in chunks for hs in heads]
        scores = [[_dot_nt(qh_ref[i, rs, hs], kh_ref[i, rs, hs]) for i in range(nh + 1)] for rs, hs in pairs]
        for (rs, hs), m in zip(pairs, scores):
            a = jnp.where(block_mask > 0.5, m[nh], 0.0)
            for i in range(nh):
                a = a + m[i] * level_mask[i]
            o = oi_ref[rs, hs] + _dot(a.astype(BF16), zi_ref[rs, hs])
            if fuse_out:
                o = o + of_ref[rs, hs]
                ms = jnp.mean(o * o, axis=-1, keepdims=True)
                g = g_ref[rs, hs].astype(F32)
                y_ref[rs, hs] = (o * lax.rsqrt(ms + EPS) * gain_ref[:, hs] * _silu(g)).astype(y_ref.dtype)
            else:
                y_ref[rs, hs] = o

    in_range = spread <= HG_FAST_MAX_LOG2

    @pl.when(in_range)
    def _():
        intra_chunk([C], half)

    @pl.when(jnp.logical_not(in_range))
    def _():
        sizes = []
        s = C
        while s > HG_BASE:
            sizes.append(s)
            s //= 2
        intra_chunk(sizes, HG_BASE)

    @pl.when(c == pl.num_programs(1) - 1)
    def _():
        st_ref[...] = s_ref[...]


def _hgrn(h, lbs4, gain3, s0, l, rev, o_fwd=None):
    bsz, seq, _ = h.shape
    R = min(HG_CHUNKS_PER_STEP * HG_CHUNK, seq)
    nc = seq // R
    W = HG_WIDTH
    fuse_out = o_fwd is not None
    n_factor = (HG_CHUNK // HG_BASE).bit_length()
    cidx = (lambda c: nc - 1 - c) if rev else (lambda c: c)

    def hcol(k):
        return pl.BlockSpec((None, R, W), lambda b, c: (b, cidx(c), k))

    lb_spec = pl.BlockSpec((None, None, 1, W), lambda b, c: (l, 1 if rev else 0, 0, 0))
    s_spec = pl.BlockSpec((None, HG_HEADS, HG_HEAD, HG_HEAD), lambda b, c: (b, 0, 0, 0))
    o_spec = pl.BlockSpec((None, R, W), lambda b, c: (b, cidx(c), 0))
    if fuse_out:
        in_specs = [hcol(_COL_Q), hcol(_COL_FB if rev else _COL_FF), hcol(_COL_I), hcol(_COL_G), o_spec,
                    lb_spec, pl.BlockSpec((None, 1, W), lambda b, c: (l, 0, 0)), s_spec]
        args = (h, h, h, h, o_fwd, lbs4, gain3, s0)
        out_dtype = BF16
    else:
        in_specs = [hcol(_COL_Q), hcol(_COL_FB if rev else _COL_FF), hcol(_COL_I), lb_spec, s_spec]
        args = (h, h, h, lbs4, s0)
        out_dtype = F32
    return pl.pallas_call(
        functools.partial(_hg_kernel, rev, fuse_out),
        grid=(bsz, nc),
        in_specs=in_specs,
        out_specs=[o_spec, s_spec],
        out_shape=[jax.ShapeDtypeStruct((bsz, seq, W), out_dtype),
                   jax.ShapeDtypeStruct((bsz, HG_HEADS, HG_HEAD, HG_HEAD), F32)],
        scratch_shapes=[pltpu.VMEM((HG_HEADS, HG_HEAD, HG_HEAD), F32),
                        pltpu.VMEM((n_factor, R, W), BF16), pltpu.VMEM((n_factor, R, W), BF16),
                        pltpu.VMEM((R, W), F32),
                        pltpu.VMEM((R, W), BF16), pltpu.VMEM((R, W), BF16), pltpu.VMEM((R, W), F32)],
        compiler_params=_params("arbitrary", "arbitrary"),
        name="hgrn_bwd" if rev else "hgrn_fwd",
    )(*args)


def _fn_prep_kernel(cs_ref, w_ref, o_ref):
    w = w_ref[...]
    o_ref[:, :FN_WIDTH] = _dot3(cs_ref[0], w).astype(BF16)
    o_ref[:, FN_WIDTH:] = _dot3(cs_ref[1], w).astype(BF16)


def _fn_prep(chan_dft, w_fnet, l):
    return pl.pallas_call(
        _fn_prep_kernel,
        grid=(1,),
        in_specs=[pl.BlockSpec((2, FN_WIDTH, FN_WIDTH), lambda i: (0, 0, 0)),
                  pl.BlockSpec((None, FN_WIDTH, FN_WIDTH), lambda i: (l, 0, 0))],
        out_specs=pl.BlockSpec((FN_WIDTH, 2 * FN_WIDTH), lambda i: (0, 0)),
        out_shape=jax.ShapeDtypeStruct((FN_WIDTH, 2 * FN_WIDTH), BF16),
        compiler_params=_params("arbitrary"),
        name="fnet_prep",
    )(chan_dft, w_fnet)


def _fn_kernel(u_ref, wc_ref, cos_ref, sin_ref, flip_ref, alt_ref, o_ref, p_ref):
    seq = u_ref.shape[0]
    half = seq // 2
    rc = min(seq, 512)

    def rows(i, carry):
        r0 = pl.multiple_of(i * rc, rc)
        p = _dot(u_ref[pl.ds(r0, rc), :], wc_ref[...])
        p_ref[0, pl.ds(r0, rc), :] = p[:, :FN_WIDTH].astype(BF16)
        p_ref[1, pl.ds(r0, rc), :] = p[:, FN_WIDTH:].astype(BF16)
        return carry

    lax.fori_loop(0, seq // rc, rows, 0)
    yc = _dot(cos_ref[...], p_ref[0])
    ys = _dot(sin_ref[...], p_ref[1])
    o_ref[:half, :] = (yc - ys).astype(o_ref.dtype)
    mirrored = _dot(flip_ref[...], (yc + ys).astype(BF16))
    middle = _dot(alt_ref[...], p_ref[0])[0:1, :]
    first = lax.broadcasted_iota(jnp.int32, (half, FN_WIDTH), 0) == 0
    o_ref[half:, :] = jnp.where(first, middle, mirrored).astype(o_ref.dtype)


def _fourier(h, wc, tables):
    bsz, seq, _ = h.shape
    half = seq // 2
    cos, sin, flip, alt = tables
    return pl.pallas_call(
        _fn_kernel,
        grid=(bsz,),
        in_specs=[pl.BlockSpec((None, seq, FN_WIDTH), lambda b: (b, 0, 0)),
                  pl.BlockSpec((FN_WIDTH, 2 * FN_WIDTH), lambda b: (0, 0)),
                  pl.BlockSpec((half, seq), lambda b: (0, 0)),
                  pl.BlockSpec((half, seq), lambda b: (0, 0)),
                  pl.BlockSpec((half, half), lambda b: (0, 0)),
                  pl.BlockSpec((8, seq), lambda b: (0, 0))],
        out_specs=pl.BlockSpec((None, seq, FN_WIDTH), lambda b: (b, 0, 0)),
        out_shape=jax.ShapeDtypeStruct((bsz, seq, FN_WIDTH), BF16),
        scratch_shapes=[pltpu.VMEM((2, seq, FN_WIDTH), BF16)],
        compiler_params=_params("arbitrary"),
        name="fourier",
    )(h, wc, cos, sin, flip, alt)


def _hy_filter_kernel(seq, wc_ref, ws_ref, w1_ref, b1_ref, w2_ref, b2_ref, w3_ref, b3_ref, wo_ref, fr_ref,
                      dl_ref, kc_ref, ks_ref, h_ref):
    i = pl.program_id(0)
    tf = wc_ref.shape[0]
    nfft = 2 * seq

    @pl.when(i == 0)
    def _():
        pos = lax.broadcasted_iota(jnp.int32, (seq, HY_PAD), 0).astype(F32)
        lane = lax.broadcasted_iota(jnp.int32, (seq, HY_PAD), 1)
        t = pos / float(max(seq - 1, 1))
        w = (2.0 * math.pi) * pos / float(seq)
        band_id = jnp.where(lane <= HY_BANDS, lane - 1, lane - 1 - HY_BANDS).astype(F32)
        band = 1e-4 + band_id * ((HY_BANDS - 1 - 1e-4) / (HY_BANDS - 1))
        arg = band * w
        z = jnp.where(lane == 0, t,
                      jnp.where(lane <= HY_BANDS, jnp.cos(arg),
                                jnp.where(lane <= 2 * HY_BANDS, -jnp.sin(arg), 0.0)))
        fr = fr_ref[...]
        hdn = jnp.sin(fr[0:1] * (_dot3(z, w1_ref[...]) + b1_ref[...]))
        hdn = jnp.sin(fr[1:2] * (_dot3(hdn, w2_ref[...]) + b2_ref[...]))
        hdn = jnp.sin(fr[2:3] * (_dot3(hdn, w3_ref[...]) + b3_ref[...]))
        hf = _dot3(hdn, wo_ref[...])
        decay = jnp.exp(-t[:, 0:1] * jnp.abs(dl_ref[...]))
        first = lax.broadcasted_iota(jnp.int32, (seq, HY_WIDTH), 0) == 0
        h_ref[:, :HY_WIDTH] = (hf[:, :HY_WIDTH] * decay).astype(BF16)
        h_ref[:, HY_WIDTH:] = jnp.where(first, 0.0, hf[:, HY_WIDTH:] * decay).astype(BF16)

    gc = _dot(wc_ref[...], h_ref[...])
    gs = _dot(ws_ref[...], h_ref[...])
    first = (lax.broadcasted_iota(jnp.int32, (tf, HY_WIDTH), 0) + i * tf) == 0
    scale = jnp.where(first, 1.0 / nfft, 2.0 / nfft)
    kc_ref[...] = (gc[:, :HY_WIDTH] + gc[:, HY_WIDTH:]) * scale
    ks_ref[...] = jnp.where(first, gs[:, :HY_WIDTH] + gs[:, HY_WIDTH:], gs[:, :HY_WIDTH] - gs[:, HY_WIDTH:]) * scale


def _hy_filter(seq, dft, mlp, deltas, l):
    w1p, b1p, w2p, b2p, w3p, b3p, wop, frp = mlp
    tf = min(seq, 512)
    nf = seq // tf

    def full(a):
        shp = a.shape[1:]
        return pl.BlockSpec((None,) + shp, lambda i: (l,) + (0,) * len(shp))

    return pl.pallas_call(
        functools.partial(_hy_filter_kernel, seq),
        grid=(nf,),
        in_specs=[pl.BlockSpec((tf, seq), lambda i: (i, 0)),
                  pl.BlockSpec((tf, seq), lambda i: (i + nf, 0)),
                  full(w1p), full(b1p), full(w2p), full(b2p), full(w3p), full(b3p), full(wop), full(frp),
                  pl.BlockSpec((1, HY_WIDTH), lambda i: (0, 0))],
        out_specs=[pl.BlockSpec((tf, HY_WIDTH), lambda i: (i, 0)),
                   pl.BlockSpec((tf, HY_WIDTH), lambda i: (i, 0))],
        out_shape=[jax.ShapeDtypeStruct((seq, HY_WIDTH), F32), jax.ShapeDtypeStruct((seq, HY_WIDTH), F32)],
        scratch_shapes=[pltpu.VMEM((seq, 2 * HY_WIDTH), BF16)],
        compiler_params=_params("arbitrary"),
        name="hyena_filter",
    )(dft, dft, w1p, b1p, w2p, b2p, w3p, b3p, wop, frp, deltas)


def _hy_pre_kernel(uv_ref, u1_ref, u0_ref, cw_ref, cb_ref, z_ref, x0_ref):
    seq = uv_ref.shape[0]
    rowi = lax.broadcasted_iota(jnp.int32, (seq, 128), 0)
    j = pl.program_id(1)

    def conv(u_ref, part):
        u = u_ref[...].astype(F32)
        prev = jnp.where(rowi == 0, 0.0, pltpu.roll(u, 1, 0))
        nxt = jnp.where(rowi == seq - 1, 0.0, pltpu.roll(u, seq - 1, 0))
        cs = pl.ds(pl.multiple_of(part * HY_WIDTH + j * 128, 128), 128)
        return prev * cw_ref[0:1, cs] + u * cw_ref[1:2, cs] + nxt * cw_ref[2:3, cs] + cb_ref[:, cs]

    z_ref[...] = (conv(u1_ref, 1) * conv(uv_ref, 0)).astype(BF16)
    x0_ref[...] = conv(u0_ref, 2).astype(BF16)


def _hy_pre(h, conv_w, conv_b3, l):
    bsz, seq, _ = h.shape
    lanes = 128
    nj = HY_WIDTH // lanes
    off = FN_WIDTH // lanes

    def part(p):
        return pl.BlockSpec((None, seq, lanes), lambda b, j: (b, 0, off + p * nj + j))

    o_spec = pl.BlockSpec((None, seq, lanes), lambda b, j: (b, 0, j))
    return pl.pallas_call(
        _hy_pre_kernel,
        grid=(bsz, nj),
        in_specs=[part(0), part(1), part(2),
                  pl.BlockSpec((None, 3, 3 * HY_WIDTH), lambda b, j: (l, 0, 0)),
                  pl.BlockSpec((None, 1, 3 * HY_WIDTH), lambda b, j: (l, 0, 0))],
        out_specs=[o_spec, o_spec],
        out_shape=[jax.ShapeDtypeStruct((bsz, seq, HY_WIDTH), BF16)] * 2,
        compiler_params=_params("arbitrary", "arbitrary"),
        name="hyena_pre",
    )(h, h, h, conv_w, conv_b3)


def _hy_fwd_kernel(wc_ref, ws_ref, z_ref, kc_ref, ks_ref, yc_ref, ys_ref):
    tf = wc_ref.shape[0]
    z = z_ref[...]
    uc = _dot(wc_ref[...], z)
    us = _dot(ws_ref[...], z)
    kc = kc_ref[...]
    ks = ks_ref[...]
    first = (lax.broadcasted_iota(jnp.int32, (tf, HY_WIDTH), 0) + pl.program_id(0) * tf) == 0
    ss = us * ks
    yc_ref[...] = (uc * kc - jnp.where(first, 0.0, ss)).astype(BF16)
    ys_ref[...] = jnp.where(first, ss, uc * ks + us * kc).astype(BF16)


def _hy_fwd(z, dft, kc, ks):
    bsz, seq, _ = z.shape
    tf = min(seq, 1024)
    nf = seq // tf
    k_spec = pl.BlockSpec((tf, HY_WIDTH), lambda i, b: (i, 0))
    y_spec = pl.BlockSpec((None, tf, HY_WIDTH), lambda i, b: (b, i, 0))
    return pl.pallas_call(
        _hy_fwd_kernel,
        grid=(nf, bsz),
        in_specs=[pl.BlockSpec((tf, seq), lambda i, b: (i, 0)),
                  pl.BlockSpec((tf, seq), lambda i, b: (i + nf, 0)),
                  pl.BlockSpec((None, seq, HY_WIDTH), lambda i, b: (b, 0, 0)),
                  k_spec, k_spec],
        out_specs=[y_spec, y_spec],
        out_shape=[jax.ShapeDtypeStruct((bsz, seq, HY_WIDTH), BF16)] * 2,
        compiler_params=_params("arbitrary", "arbitrary"),
        name="hyena_dft",
    )(dft, dft, z, kc, ks)


def _hy_inv_kernel(tc_ref, ts_ref, yc_ref, ys_ref, z_ref, x0_ref, db_ref, o_ref):
    y = _dot(tc_ref[...], yc_ref[...]) + _dot(ts_ref[...], ys_ref[...])
    z = z_ref[...].astype(F32)
    o_ref[...] = (x0_ref[...].astype(F32) * (y + z * db_ref[...])).astype(BF16)


def _hy_inv(yc, ys, dft_t, z, x0, hy_bias3, l):
    bsz, seq, _ = z.shape
    tt = min(seq, 1024)
    y_spec = pl.BlockSpec((None, seq, HY_WIDTH), lambda t, b: (b, 0, 0))
    r_spec = pl.BlockSpec((None, tt, HY_WIDTH), lambda t, b: (b, t, 0))
    return pl.pallas_call(
        _hy_inv_kernel,
        grid=(seq // tt, bsz),
        in_specs=[pl.BlockSpec((tt, seq), lambda t, b: (t, 0)),
                  pl.BlockSpec((tt, seq), lambda t, b: (t, 1)),
                  y_spec, y_spec, r_spec, r_spec,
                  pl.BlockSpec((None, 1, HY_WIDTH), lambda t, b: (l, 0, 0))],
        out_specs=r_spec,
        out_shape=jax.ShapeDtypeStruct((bsz, seq, HY_WIDTH), BF16),
        compiler_params=_params("arbitrary", "arbitrary"),
        name="hyena_idft",
    )(dft_t, dft_t, yc, ys, z, x0, hy_bias3)


def _out_route_kernel(yf_ref, yh_ref, yg_ref, wf_ref, wh_ref, wg_ref, x_ref, g1_ref, gam_ref, sh_ref, sc_ref,
                      wr_ref, xo_ref, xm_ref, aff_ref):
    mix = _dot(yf_ref[...], wf_ref[...]) + _dot(yh_ref[...], wh_ref[...]) + _dot(yg_ref[...], wg_ref[...])
    x = x_ref[...] + g1_ref[...] * mix
    xo_ref[...] = x
    xm = _norm_mod(x, gam_ref[...], sh_ref[...], sc_ref[...])
    xm_ref[...] = xm.astype(BF16)
    logits = _dot3(wr_ref[...], xm, dot=_dot_nt)
    mx = jnp.max(logits, axis=0, keepdims=True)
    ex = jnp.exp(logits - mx)
    aff_ref[...] = ex / jnp.sum(ex, axis=0, keepdims=True)


def _out_route(x, y_fn, y_hy, y_hg, w_out_bf, mod4, mrow, gamma3, w_router_t, l):
    bsz, seq, d = x.shape
    tm = min(seq, 512)
    half = FN_WIDTH

    def mod_row(k):
        return pl.BlockSpec((None, None, 1, d), lambda b, i: (mrow(b), k, 0, 0))

    return pl.pallas_call(
        _out_route_kernel,
        grid=(bsz, seq // tm),
        in_specs=[pl.BlockSpec((None, tm, half), lambda b, i: (b, i, 0)),
                  pl.BlockSpec((None, tm, half), lambda b, i: (b, i, 0)),
                  pl.BlockSpec((None, tm, HG_WIDTH), lambda b, i: (b, i, 0)),
                  pl.BlockSpec((None, half, d), lambda b, i: (l, 0, 0)),
                  pl.BlockSpec((None, half, d), lambda b, i: (l, 1, 0)),
                  pl.BlockSpec((None, HG_WIDTH, d), lambda b, i: (l, 1, 0)),
                  pl.BlockSpec((None, tm, d), lambda b, i: (b, i, 0)),
                  mod_row(2),
                  pl.BlockSpec((None, 1, d), lambda b, i: (l, 0, 0)),
                  mod_row(3), mod_row(4),
                  pl.BlockSpec((None, N_EXPERTS, d), lambda b, i: (l, 0, 0))],
        out_specs=[pl.BlockSpec((None, tm, d), lambda b, i: (b, i, 0)),
                   pl.BlockSpec((None, tm, d), lambda b, i: (b, i, 0)),
                   pl.BlockSpec((None, N_EXPERTS, tm), lambda b, i: (b, 0, i))],
        out_shape=[jax.ShapeDtypeStruct((bsz, seq, d), F32),
                   jax.ShapeDtypeStruct((bsz, seq, d), BF16),
                   jax.ShapeDtypeStruct((bsz, N_EXPERTS, seq), F32)],
        compiler_params=_params("arbitrary", "arbitrary"),
        name="out_route",
    )(y_fn, y_hy, y_hg, w_out_bf, w_out_bf, w_out_bf, x, mod4, gamma3, mod4, mod4, w_router_t)


def _topk_kernel(cap, aff_ref, tri_ref, pos_ref):
    a = aff_ref[...]

    def count(mask):
        return jnp.sum(jnp.where(mask, 1.0, 0.0), axis=1, keepdims=True)

    def as_float(bits):
        return pltpu.bitcast(jnp.broadcast_to(bits, a.shape), F32)

    def step(i, thr_bits):
        cand = thr_bits | jnp.left_shift(jnp.int32(1), 30 - i)
        return jnp.where(count(a >= as_float(cand)) >= cap, cand, thr_bits)

    thr = as_float(lax.fori_loop(0, 31, step, jnp.zeros((a.shape[0], 1), jnp.int32)))
    above = a > thr
    tie = a == thr
    room = cap - count(above)
    tie_rank = _dot(jnp.where(tie, 1.0, 0.0).astype(BF16), tri_ref[...])
    sel = jnp.where(above, 1.0, jnp.where(tie, jnp.where(tie_rank <= room, 1.0, 0.0), 0.0))
    slot = _dot(sel.astype(BF16), tri_ref[...]) - 1.0
    pos_ref[...] = jnp.where(sel > 0.5, slot, -1.0).astype(jnp.int32)


def _topk(aff, tri_incl, cap):
    bsz, ne, seq = aff.shape
    rows = bsz * ne
    pos = pl.pallas_call(
        functools.partial(_topk_kernel, cap),
        grid=(1,),
        in_specs=[pl.BlockSpec((rows, seq), lambda i: (0, 0)),
                  pl.BlockSpec((seq, seq), lambda i: (0, 0))],
        out_specs=pl.BlockSpec((rows, seq), lambda i: (0, 0)),
        out_shape=jax.ShapeDtypeStruct((rows, seq), jnp.int32),
        compiler_params=_params("arbitrary"),
        name="moe_topk",
    )(aff.reshape(rows, seq), tri_incl)
    return pos.reshape(bsz, ne, seq)


def _gather_kernel(cap, xm_ref, pos_ref, o_ref):
    seq = xm_ref.shape[0]
    pos = pos_ref[pl.ds(pl.program_id(1), 1), :]
    slot = lax.broadcasted_iota(jnp.int32, (cap, seq), 0)
    onehot = jnp.where(slot == pos, 1.0, 0.0).astype(BF16)
    o_ref[...] = _dot(onehot, xm_ref[...]).astype(BF16)


def _gather(xm, pos, cap):
    bsz, seq, d = xm.shape
    return pl.pallas_call(
        functools.partial(_gather_kernel, cap),
        grid=(bsz, N_EXPERTS),
        in_specs=[pl.BlockSpec((None, seq, d), lambda b, e: (b, 0, 0)),
                  pl.BlockSpec((None, N_EXPERTS, seq), lambda b, e: (b, 0, 0))],
        out_specs=pl.BlockSpec((None, None, cap, d), lambda b, e: (e, b, 0, 0)),
        out_shape=jax.ShapeDtypeStruct((N_EXPERTS, bsz, cap, d), BF16),
        compiler_params=_params("arbitrary", "arbitrary"),
        name="moe_gather",
    )(xm, pos)


def _ffn_kernel(xs_ref, wg_ref, wu_ref, wd_ref, o_ref, acc_ref):
    j = pl.program_id(2)

    @pl.when((pl.program_id(0) == 0) & (pl.program_id(1) == 0) & (j == 0))
    def _():
        acc_ref[...] = jnp.zeros_like(acc_ref)

    xs = xs_ref[...]
    hid = _silu(_dot(xs, wg_ref[...].astype(BF16))) * _dot(xs, wu_ref[...].astype(BF16))
    part = _dot(hid.astype(BF16), wd_ref[...].astype(BF16))
    total = part + jnp.where(j > 0, acc_ref[...], 0.0)
    acc_ref[...] = total
    o_ref[...] = total.astype(BF16)


def _ffn(xs, w_gate, w_up, w_down, l):
    ne, rows, d = xs.shape
    ff = w_gate.shape[-1]
    tm = min(rows, 1024)
    tj = 512
    return pl.pallas_call(
        _ffn_kernel,
        grid=(ne, rows // tm, ff // tj),
        in_specs=[pl.BlockSpec((None, tm, d), lambda e, m, j: (e, m, 0)),
                  pl.BlockSpec((None, None, d, tj), lambda e, m, j: (l, e, 0, j)),
                  pl.BlockSpec((None, None, d, tj), lambda e, m, j: (l, e, 0, j)),
                  pl.BlockSpec((None, None, tj, d), lambda e, m, j: (l, e, j, 0))],
        out_specs=pl.BlockSpec((None, tm, d), lambda e, m, j: (e, m, 0)),
        out_shape=jax.ShapeDtypeStruct((ne, rows, d), BF16),
        scratch_shapes=[pltpu.VMEM((tm, d), F32)],
        compiler_params=_params("arbitrary", "arbitrary", "arbitrary"),
        name="moe_ffn",
    )(xs, w_gate, w_up, w_down)


def _combine_kernel(cap, final_norm, *refs):
    if final_norm:
        ys_ref, pos_ref, aff_ref, x_ref, g_ref, fg_ref, o_ref, pos_t_ref, gate_t_ref, w_ref = refs
    else:
        ys_ref, pos_ref, aff_ref, x_ref, g_ref, o_ref, pos_t_ref, gate_t_ref, w_ref = refs
    i = pl.program_id(1)
    tm = x_ref.shape[0]
    ne = pos_ref.shape[0]

    @pl.when(i == 0)
    def _():
        pos_t_ref[...] = pos_ref[...].astype(F32).T
        gate_t_ref[...] = aff_ref[...].T

    r0 = pl.multiple_of(i * tm, tm)
    pos = pos_t_ref[pl.ds(r0, tm), :]
    gate = gate_t_ref[pl.ds(r0, tm), :]
    if cap % 128 == 0:
        lane = lax.broadcasted_iota(jnp.int32, (tm, cap), 1).astype(F32)
        for e in range(ne):
            w_ref[:, e * cap:(e + 1) * cap] = jnp.where(lane == pos[:, e:e + 1], gate[:, e:e + 1], 0.0).astype(BF16)
    else:
        lane = lax.broadcasted_iota(jnp.int32, (tm, ne * cap), 1).astype(F32)
        w = jnp.zeros((tm, ne * cap), F32)
        for e in range(ne):
            hit = jnp.logical_and(lane == pos[:, e:e + 1] + float(e * cap), pos[:, e:e + 1] >= 0.0)
            w = w + jnp.where(hit, gate[:, e:e + 1], 0.0)
        w_ref[...] = w.astype(BF16)
    ys = ys_ref[...].reshape(ne * cap, ys_ref.shape[-1])
    x = x_ref[...] + g_ref[...] * _dot(w_ref[...], ys)
    if final_norm:
        ms = jnp.mean(x * x, axis=-1, keepdims=True)
        x = x * lax.rsqrt(ms + EPS) * fg_ref[...]
    o_ref[...] = x


def _combine(ys, pos, aff, x, mod4, mrow, cap, final_g=None):
    bsz, seq, d = x.shape
    ne = pos.shape[1]
    tm = min(seq, 256)
    final_norm = final_g is not None
    in_specs = [pl.BlockSpec((ne, None, cap, d), lambda b, i: (0, b, 0, 0)),
                pl.BlockSpec((None, ne, seq), lambda b, i: (b, 0, 0)),
                pl.BlockSpec((None, ne, seq), lambda b, i: (b, 0, 0)),
                pl.BlockSpec((None, tm, d), lambda b, i: (b, i, 0)),
                pl.BlockSpec((None, None, 1, d), lambda b, i: (mrow(b), 5, 0, 0))]
    args = (ys, pos, aff, x, mod4)
    if final_norm:
        in_specs.append(pl.BlockSpec((1, d), lambda b, i: (0, 0)))
        args += (final_g,)
    return pl.pallas_call(
        functools.partial(_combine_kernel, cap, final_norm),
        grid=(bsz, seq // tm),
        in_specs=in_specs,
        out_specs=pl.BlockSpec((None, tm, d), lambda b, i: (b, i, 0)),
        out_shape=jax.ShapeDtypeStruct((bsz, seq, d), F32),
        scratch_shapes=[pltpu.VMEM((seq, ne), F32), pltpu.VMEM((seq, ne), F32), pltpu.VMEM((tm, ne * cap), BF16)],
        compiler_params=_params("arbitrary", "arbitrary"),
        name="moe_combine",
    )(*args)


def _angles(row_ids, cols, n):
    c = lax.broadcasted_iota(jnp.int32, (row_ids.shape[0], cols), 1)
    return ((row_ids * c) % n).astype(F32) * (2.0 * math.pi / n)


def _cos_sin(rows, cols, n):
    step = min(rows, 32)
    hi = _angles(jnp.arange(0, rows, step, dtype=jnp.int32)[:, None], cols, n)[:, None, :]
    lo = _angles(jnp.arange(step, dtype=jnp.int32)[:, None], cols, n)[None, :, :]
    cos = jnp.cos(hi) * jnp.cos(lo) - jnp.sin(hi) * jnp.sin(lo)
    sin = jnp.sin(hi) * jnp.cos(lo) + jnp.cos(hi) * jnp.sin(lo)
    return cos.reshape(rows, cols), sin.reshape(rows, cols)


def _fourier_table(seq):
    half = seq // 2
    cos, sin = _cos_sin(half, seq, seq)
    r = lax.broadcasted_iota(jnp.int32, (half, half), 0)
    c = lax.broadcasted_iota(jnp.int32, (half, half), 1)
    flip = ((r >= 1) & (c == half - r)).astype(BF16)
    t = lax.broadcasted_iota(jnp.int32, (8, seq), 1)
    alt = (1 - 2 * (t % 2)).astype(BF16)
    return cos.astype(BF16), sin.astype(BF16), flip, alt


def _channel_dft(seq):
    ang = _angles(jnp.arange(FN_GROUP, dtype=jnp.int32)[:, None], FN_GROUP, FN_GROUP)
    scale = 1.0 / math.sqrt(seq * FN_GROUP)
    eye = jnp.eye(FN_WIDTH // FN_GROUP, dtype=F32)
    return jnp.stack([jnp.kron(eye, jnp.cos(ang) * scale), jnp.kron(eye, jnp.sin(ang) * scale)])


def _hyena_dft(seq):
    cos, sin = _cos_sin(seq, seq, 2 * seq)
    r = lax.broadcasted_iota(jnp.int32, (seq, seq), 0)
    c = lax.broadcasted_iota(jnp.int32, (seq, seq), 1)
    dft = jnp.concatenate([cos, jnp.where(r == 0, (1 - 2 * (c % 2)).astype(F32), sin)], axis=0).astype(BF16)
    dft_t = jnp.concatenate([cos, jnp.where(c == 0, (1 - 2 * (r % 2)).astype(F32), sin)], axis=1).astype(BF16)
    return dft, dft_t


def _tri_incl(seq):
    r = lax.broadcasted_iota(jnp.int32, (seq, seq), 0)
    c = lax.broadcasted_iota(jnp.int32, (seq, seq), 1)
    return (r <= c).astype(BF16)


def _pad_to(a, shape):
    return jnp.pad(a, [(0, t - s) for s, t in zip(a.shape, shape)])


def kernel(x, c, ctx, c_ctx, norm_mix_g, norm_ffn_g, final_norm_g, w_mod, b_mod, w_in, w_out, w_fnet,
           hy_conv_w, hy_conv_b, hy_w1, hy_b1, hy_w2, hy_b2, hy_w3, hy_b3, hy_w_out, hy_freq, hy_bias,
           hg_lb, hg_norm_g, w_router, w_gate, w_up, w_down):
    bsz, seq, d = x.shape
    ctx_len = ctx.shape[1]
    depth = w_in.shape[0]

    p = jax.nn.softmax(hg_lb.astype(F32), axis=0)
    lbs4 = (jnp.cumsum(p, axis=0) - p[0:1]).reshape(depth, 2, 1, HG_WIDTH)
    w_in_bf = w_in.astype(BF16)
    w_out_bf = w_out.astype(BF16)
    w_router_t = jnp.swapaxes(w_router, 1, 2)
    g_mix3 = norm_mix_g.reshape(depth, 1, d)
    g_ffn3 = norm_ffn_g.reshape(depth, 1, d)
    gain3 = hg_norm_g.reshape(depth, 1, HG_WIDTH)
    b_mod3 = b_mod.reshape(depth, 1, 6 * d)
    conv_b3 = hy_conv_b.reshape(depth, 1, 3 * HY_WIDTH)
    hy_bias3 = hy_bias.reshape(depth, 1, HY_WIDTH)
    mlp = (_pad_to(hy_w1, (depth, HY_PAD, HY_PAD)), _pad_to(hy_b1.reshape(depth, 1, -1), (depth, 1, HY_PAD)),
           _pad_to(hy_w2, (depth, HY_PAD, HY_PAD)), _pad_to(hy_b2.reshape(depth, 1, -1), (depth, 1, HY_PAD)),
           _pad_to(hy_w3, (depth, HY_PAD, HY_PAD)), _pad_to(hy_b3.reshape(depth, 1, -1), (depth, 1, HY_PAD)),
           _pad_to(hy_w_out, (depth, HY_PAD, 2 * HY_WIDTH)), _pad_to(hy_freq, (depth, 8, HY_PAD)))
    max_decay = math.log(1e-2) / 0.3
    min_decay = math.log(1e-2) / 1.5
    deltas = jnp.linspace(min_decay, max_decay, HY_WIDTH, dtype=F32).reshape(1, HY_WIDTH)
    rows = 16
    cc = jnp.zeros((rows, d), F32).at[:bsz].set(c).at[bsz].set(c_ctx)

    tables = {}
    for n in {seq, ctx_len}:
        dft, dft_t = _hyena_dft(n)
        tables[n] = dict(fourier=_fourier_table(n), chan=_channel_dft(n), dft=dft, dft_t=dft_t, tri=_tri_incl(n))

    x_row = lambda b: b
    ctx_row = lambda b: bsz
    zero_state = jnp.zeros((bsz, HG_HEADS, HG_HEAD, HG_HEAD), F32)

    def mixers(hh, y_hg, n, l):
        t = tables[n]
        y_fn = _fourier(hh, _fn_prep(t["chan"], w_fnet, l), t["fourier"])
        kc, ks = _hy_filter(n, t["dft"], mlp, deltas, l)
        z, x0 = _hy_pre(hh, hy_conv_w, conv_b3, l)
        yc, ys = _hy_fwd(z, t["dft"], kc, ks)
        y_hy = _hy_inv(yc, ys, t["dft_t"], z, x0, hy_bias3, l)
        return y_fn, y_hy, y_hg

    def sublayers(xx, hh, y_hg, mod4, mrow, n, l, final_g=None):
        cap = EC_CAPACITY * n // N_EXPERTS
        xx, xm, aff = _out_route(xx, *mixers(hh, y_hg, n, l), w_out_bf, mod4, mrow, g_ffn3, w_router_t, l)
        pos = _topk(aff, tables[n]["tri"], cap)
        xs = _gather(xm, pos, cap)
        ys = _ffn(xs.reshape(N_EXPERTS, bsz * cap, d), w_gate, w_up, w_down, l)
        return _combine(ys.reshape(N_EXPERTS, bsz, cap, d), pos, aff, xx, mod4, mrow, cap, final_g)

    xc = ctx
    for l in range(depth):
        last = l == depth - 1
        mod4 = _modulation(cc, w_mod, b_mod3, l).reshape(rows, 6, 1, d)
        h = _in_proj(x, mod4, x_row, g_mix3, w_in_bf, l)
        hc = _in_proj(xc.reshape(1, bsz * ctx_len, d), mod4, ctx_row, g_mix3, w_in_bf, l).reshape(bsz, ctx_len, -1)
        o_cf, s_f = _hgrn(hc, lbs4, gain3, zero_state, l, False)
        y_hg_c, s_b = _hgrn(hc, lbs4, gain3, zero_state, l, True, o_fwd=o_cf)
        o_xf, _ = _hgrn(h, lbs4, gain3, s_f, l, False)
        y_hg_x, _ = _hgrn(h, lbs4, gain3, s_b, l, True, o_fwd=o_xf)
        x = sublayers(x, h, y_hg_x, mod4, x_row, seq, l, final_norm_g.reshape(1, d) if last else None)
        if not last:
            xc = sublayers(xc, hc, y_hg_c, mod4, ctx_row, ctx_len, l)
    return x
```

```python
import functools
import math

import jax
import jax.numpy as jnp
from jax import lax
from jax.experimental import pallas as pl
from jax.experimental.pallas import tpu as pltpu

F32 = jnp.float32
BF16 = jnp.bfloat16

D_MODEL = 2048
FN_WIDTH = 512
FN_GROUP = 128
HY_WIDTH = 512
HG_WIDTH = 1024
HG_HEAD = 128
HG_HEADS = HG_WIDTH // HG_HEAD
HG_F_MIN = 1e-6
IN_WIDTH = FN_WIDTH + 3 * HY_WIDTH + 5 * HG_WIDTH
HY_BANDS = 16
HY_PAD = 128
N_EXPERTS = 16
EC_CAPACITY = 2
EXPERT_FF = 1024
EPS = 1e-6

HG_CHUNK = 128
HG_CHUNKS_PER_STEP = 4
HG_BASE = 8
HG_FAST_MAX_LOG2 = 115.0
HG_FAST_MAX_Q = 1024.0
LOG2_E = math.log2(math.e)
VMEM_LIMIT = 56 * 1024 * 1024

_COL_Q, _COL_FF, _COL_FB, _COL_I, _COL_G = 2, 3, 4, 5, 6


def _params(*sem):
    return pltpu.CompilerParams(dimension_semantics=sem, vmem_limit_bytes=VMEM_LIMIT)


def _dot(a, b):
    return jnp.dot(a, b, preferred_element_type=F32)


def _dot_nt(a, b):
    return lax.dot_general(a, b, (((1,), (1,)), ((), ())), preferred_element_type=F32)


def _split2(x):
    hi = x.astype(BF16)
    lo = (x - hi.astype(F32)).astype(BF16)
    return hi, lo


def _dot3(a, b, dot=_dot):
    ah, al = _split2(a)
    bh, bl = _split2(b)
    return dot(ah, bh) + dot(ah, bl) + dot(al, bh)


def _silu(x):
    return x * jax.nn.sigmoid(x)


def _norm_mod(x, g, sh, sc):
    ms = jnp.mean(x * x, axis=-1, keepdims=True)
    return (x * lax.rsqrt(ms + EPS) * g) * (1.0 + sc) + sh


def _mod_kernel(a_ref, w_ref, b_ref, o_ref):
    a = _silu(a_ref[...]).astype(BF16)
    o_ref[...] = _dot(a, w_ref[...].astype(BF16)) + b_ref[...]


def _modulation(cc, w_mod, b_mod3, l):
    rows, d = cc.shape
    n = w_mod.shape[-1]
    tn = 1024
    return pl.pallas_call(
        _mod_kernel,
        grid=(n // tn,),
        in_specs=[
            pl.BlockSpec((rows, d), lambda j: (0, 0)),
            pl.BlockSpec((None, d, tn), lambda j: (l, 0, j)),
            pl.BlockSpec((None, 1, tn), lambda j: (l, 0, j)),
        ],
        out_specs=pl.BlockSpec((rows, tn), lambda j: (0, j)),
        out_shape=jax.ShapeDtypeStruct((rows, n), F32),
        compiler_params=_params("arbitrary"),
        name="modulation",
    )(cc, w_mod, b_mod3)


def _in_kernel(x_ref, g_ref, sh_ref, sc_ref, w_ref, o_ref, xm_ref):
    @pl.when(pl.program_id(2) == 0)
    def _():
        xm_ref[...] = _norm_mod(x_ref[...], g_ref[...], sh_ref[...], sc_ref[...]).astype(BF16)

    o_ref[...] = _dot(xm_ref[...], w_ref[...]).astype(o_ref.dtype)


def _in_proj(x, mod4, mrow, gamma3, w_in_bf, l):
    bsz, seq, d = x.shape
    n = w_in_bf.shape[-1]
    tm = min(seq, 1024)
    tn = 1792
    return pl.pallas_call(
        _in_kernel,
        grid=(bsz, seq // tm, n // tn),
        in_specs=[
            pl.BlockSpec((None, tm, d), lambda b, i, j: (b, i, 0)),
            pl.BlockSpec((None, 1, d), lambda b, i, j: (l, 0, 0)),
            pl.BlockSpec((None, None, 1, d), lambda b, i, j: (mrow(b), 0, 0, 0)),
            pl.BlockSpec((None, None, 1, d), lambda b, i, j: (mrow(b), 1, 0, 0)),
            pl.BlockSpec((None, d, tn), lambda b, i, j: (l, 0, j)),
        ],
        out_specs=pl.BlockSpec((None, tm, tn), lambda b, i, j: (b, i, j)),
        out_shape=jax.ShapeDtypeStruct((bsz, seq, n), BF16),
        scratch_shapes=[pltpu.VMEM((tm, d), BF16)],
        compiler_params=_params("arbitrary", "arbitrary", "arbitrary"),
        name="in_proj",
    )(x, gamma3, mod4, mod4, w_in_bf)


def _hg_kernel(rev, fuse_out, *refs):
    if fuse_out:
        (zq_ref, zf_ref, zi_ref, g_ref, of_ref, lb_ref, gain_ref, s0_ref, y_ref, st_ref,
         s_ref, qh_ref, kh_ref, oi_ref, q_ref, kk_ref, b_ref) = refs
    else:
        (zq_ref, zf_ref, zi_ref, lb_ref, s0_ref, y_ref, st_ref,
         s_ref, qh_ref, kh_ref, oi_ref, q_ref, kk_ref, b_ref) = refs
    C = HG_CHUNK
    W = HG_WIDTH
    R = zq_ref.shape[0]
    chunks = [slice(k * C, (k + 1) * C) for k in range(R // C)]
    if rev:
        chunks = chunks[::-1]
    heads = [slice(h * HG_HEAD, (h + 1) * HG_HEAD) for h in range(HG_HEADS)]
    c = pl.program_id(1)

    @pl.when(c == 0)
    def _():
        s_ref[...] = s0_ref[...]

    def scan_order_iota(n):
        row = lax.broadcasted_iota(jnp.int32, (n, n), 0)
        col = lax.broadcasted_iota(jnp.int32, (n, n), 1)
        return (n - 1 - row, n - 1 - col) if rev else (row, col)

    row, col = scan_order_iota(C)
    tri = jnp.where(col <= row, 1.0, 0.0)
    tri_bf = tri.astype(BF16)

    def ref_rows(b, s, r):
        n = b.shape[0]
        b3 = b.reshape(n // s, s, W)
        return jnp.broadcast_to(b3[:, r:r + 1, :], (n // s, s, W)).reshape(n, W)

    def mid_offset(b, s):
        hh = s // 2
        return b - ref_rows(b, s, hh if rev else hh - 1)

    lb = lb_ref[...]
    half = C // 2
    spread = None
    q_peak = None
    for rs in chunks:
        zf = zf_ref[rs, :].astype(F32)
        q = _silu(zq_ref[rs, :])
        f = jnp.maximum(lb + (1.0 - lb) * jax.nn.sigmoid(zf), HG_F_MIN)
        kk = (1.0 - f).astype(BF16)
        hi, lo = _split2(jnp.log(f) * LOG2_E)
        b = _dot(tri_bf, hi) + _dot(tri_bf, lo)
        btot = b[0:1, :] if rev else b[C - 1:C, :]
        qe = q * jnp.exp2(b).astype(BF16)
        kd = kk * jnp.exp2(btot - b).astype(BF16)
        sdec = jnp.exp2(btot)
        v_t = zi_ref[rs, :].T
        for h, hs in enumerate(heads):
            st = s_ref[h]
            oi_ref[rs, hs] = _dot_nt(qe[:, hs], st.astype(BF16))
            s_ref[h] = st * sdec[:, hs] + _dot(v_t[hs, :], kd[:, hs])
        q_ref[rs, :] = q
        kk_ref[rs, :] = kk
        b_ref[rs, :] = b
        m = jnp.max(jnp.abs(mid_offset(b, half)))
        spread = m if spread is None else jnp.maximum(spread, m)
        m = jnp.max(jnp.abs(q.astype(F32)))
        q_peak = m if q_peak is None else jnp.maximum(q_peak, m)

    def intra_chunk(halvings, block):
        q = q_ref[...]
        kk = kk_ref[...]
        b = b_ref[...]
        for i, s in enumerate(halvings):
            w = jnp.exp2(-jnp.abs(mid_offset(b, s))).astype(BF16)
            qh_ref[i] = q * w
            kh_ref[i] = kk * w
        d = mid_offset(b, block)
        nh = len(halvings)
        qh_ref[nh] = q * jnp.exp2(d).astype(BF16)
        kh_ref[nh] = kk * jnp.exp2(-d).astype(BF16)
        level_mask = [jnp.where(((row // s) == (col // s)) & ((row % s) >= s // 2) & ((col % s) < s // 2), 1.0, 0.0)
                      for s in halvings]
        block_mask = jnp.where((row // block) == (col // block), tri, 0.0)

        pairs = [(rs, hs) for rs in chunks for hs in heads]
        scores = [[_dot_nt(qh_ref[i, rs, hs], kh_ref[i, rs, hs]) for i in range(nh + 1)] for rs, hs in pairs]
        for (rs, hs), m in zip(pairs, scores):
            a = jnp.where(block_mask > 0.5, m[nh], 0.0)
            for i in range(nh):
                a = a + m[i] * level_mask[i]
            o = oi_ref[rs, hs] + _dot(a.astype(BF16), zi_ref[rs, hs])
            if fuse_out:
                o = o + of_ref[rs, hs]
                ms = jnp.mean(o * o, axis=-1, keepdims=True)
                g = g_ref[rs, hs].astype(F32)
                y_ref[rs, hs] = (o * lax.rsqrt(ms + EPS) * gain_ref[:, hs] * _silu(g)).astype(y_ref.dtype)
            else:
                y_ref[rs, hs] = o

    in_range = (spread <= HG_FAST_MAX_LOG2) & (q_peak <= HG_FAST_MAX_Q)

    @pl.when(in_range)
    def _():
        intra_chunk([C], half)

    @pl.when(jnp.logical_not(in_range))
    def _():
        sizes = []
        s = C
        while s > HG_BASE:
            sizes.append(s)
            s //= 2
        intra_chunk(sizes, HG_BASE)

    @pl.when(c == pl.num_programs(1) - 1)
    def _():
        st_ref[...] = s_ref[...]


def _hgrn(h, lbs4, gain3, s0, l, rev, o_fwd=None):
    bsz, seq, _ = h.shape
    R = min(HG_CHUNKS_PER_STEP * HG_CHUNK, seq)
    nc = seq // R
    W = HG_WIDTH
    fuse_out = o_fwd is not None
    n_factor = (HG_CHUNK // HG_BASE).bit_length()
    cidx = (lambda c: nc - 1 - c) if rev else (lambda c: c)

    def hcol(k):
        return pl.BlockSpec((None, R, W), lambda b, c: (b, cidx(c), k))

    lb_spec = pl.BlockSpec((None, None, 1, W), lambda b, c: (l, 1 if rev else 0, 0, 0))
    s_spec = pl.BlockSpec((None, HG_HEADS, HG_HEAD, HG_HEAD), lambda b, c: (b, 0, 0, 0))
    o_spec = pl.BlockSpec((None, R, W), lambda b, c: (b, cidx(c), 0))
    if fuse_out:
        in_specs = [hcol(_COL_Q), hcol(_COL_FB if rev else _COL_FF), hcol(_COL_I), hcol(_COL_G), o_spec,
                    lb_spec, pl.BlockSpec((None, 1, W), lambda b, c: (l, 0, 0)), s_spec]
        args = (h, h, h, h, o_fwd, lbs4, gain3, s0)
        out_dtype = BF16
    else:
        in_specs = [hcol(_COL_Q), hcol(_COL_FB if rev else _COL_FF), hcol(_COL_I), lb_spec, s_spec]
        args = (h, h, h, lbs4, s0)
        out_dtype = F32
    return pl.pallas_call(
        functools.partial(_hg_kernel, rev, fuse_out),
        grid=(bsz, nc),
        in_specs=in_specs,
        out_specs=[o_spec, s_spec],
        out_shape=[jax.ShapeDtypeStruct((bsz, seq, W), out_dtype),
                   jax.ShapeDtypeStruct((bsz, HG_HEADS, HG_HEAD, HG_HEAD), F32)],
        scratch_shapes=[pltpu.VMEM((HG_HEADS, HG_HEAD, HG_HEAD), F32),
                        pltpu.VMEM((n_factor, R, W), BF16), pltpu.VMEM((n_factor, R, W), BF16),
                        pltpu.VMEM((R, W), F32),
                        pltpu.VMEM((R, W), BF16), pltpu.VMEM((R, W), BF16), pltpu.VMEM((R, W), F32)],
        compiler_params=_params("arbitrary", "arbitrary"),
        name="hgrn_bwd" if rev else "hgrn_fwd",
    )(*args)


def _fn_prep_kernel(cs_ref, w_ref, o_ref):
    w = w_ref[...]
    o_ref[:, :FN_WIDTH] = _dot3(cs_ref[0], w).astype(BF16)
    o_ref[:, FN_WIDTH:] = _dot3(cs_ref[1], w).astype(BF16)


def _fn_prep(chan_dft, w_fnet, l):
    return pl.pallas_call(
        _fn_prep_kernel,
        grid=(1,),
        in_specs=[pl.BlockSpec((2, FN_WIDTH, FN_WIDTH), lambda i: (0, 0, 0)),
                  pl.BlockSpec((None, FN_WIDTH, FN_WIDTH), lambda i: (l, 0, 0))],
        out_specs=pl.BlockSpec((FN_WIDTH, 2 * FN_WIDTH), lambda i: (0, 0)),
        out_shape=jax.ShapeDtypeStruct((FN_WIDTH, 2 * FN_WIDTH), BF16),
        compiler_params=_params("arbitrary"),
        name="fnet_prep",
    )(chan_dft, w_fnet)


def _fn_kernel(u_ref, wc_ref, cos_ref, sin_ref, flip_ref, alt_ref, o_ref, p_ref):
    seq = u_ref.shape[0]
    half = seq // 2
    rc = min(seq, 512)

    def rows(i, carry):
        r0 = pl.multiple_of(i * rc, rc)
        p = _dot(u_ref[pl.ds(r0, rc), :], wc_ref[...])
        p_ref[0, pl.ds(r0, rc), :] = p[:, :FN_WIDTH].astype(BF16)
        p_ref[1, pl.ds(r0, rc), :] = p[:, FN_WIDTH:].astype(BF16)
        return carry

    lax.fori_loop(0, seq // rc, rows, 0)
    yc = _dot(cos_ref[...], p_ref[0])
    ys = _dot(sin_ref[...], p_ref[1])
    o_ref[:half, :] = (yc - ys).astype(o_ref.dtype)
    mirrored = _dot(flip_ref[...], (yc + ys).astype(BF16))
    middle = _dot(alt_ref[...], p_ref[0])[0:1, :]
    first = lax.broadcasted_iota(jnp.int32, (half, FN_WIDTH), 0) == 0
    o_ref[half:, :] = jnp.where(first, middle, mirrored).astype(o_ref.dtype)


def _fourier(h, wc, tables):
    bsz, seq, _ = h.shape
    half = seq // 2
    cos, sin, flip, alt = tables
    return pl.pallas_call(
        _fn_kernel,
        grid=(bsz,),
        in_specs=[pl.BlockSpec((None, seq, FN_WIDTH), lambda b: (b, 0, 0)),
                  pl.BlockSpec((FN_WIDTH, 2 * FN_WIDTH), lambda b: (0, 0)),
                  pl.BlockSpec((half, seq), lambda b: (0, 0)),
                  pl.BlockSpec((half, seq), lambda b: (0, 0)),
                  pl.BlockSpec((half, half), lambda b: (0, 0)),
                  pl.BlockSpec((8, seq), lambda b: (0, 0))],
        out_specs=pl.BlockSpec((None, seq, FN_WIDTH), lambda b: (b, 0, 0)),
        out_shape=jax.ShapeDtypeStruct((bsz, seq, FN_WIDTH), BF16),
        scratch_shapes=[pltpu.VMEM((2, seq, FN_WIDTH), BF16)],
        compiler_params=_params("arbitrary"),
        name="fourier",
    )(h, wc, cos, sin, flip, alt)


def _hy_filter_kernel(seq, wc_ref, ws_ref, w1_ref, b1_ref, w2_ref, b2_ref, w3_ref, b3_ref, wo_ref, fr_ref,
                      dl_ref, kc_ref, ks_ref, h_ref):
    i = pl.program_id(0)
    tf = wc_ref.shape[0]
    nfft = 2 * seq

    @pl.when(i == 0)
    def _():
        pos = lax.broadcasted_iota(jnp.int32, (seq, HY_PAD), 0).astype(F32)
        lane = lax.broadcasted_iota(jnp.int32, (seq, HY_PAD), 1)
        t = pos / float(max(seq - 1, 1))
        w = (2.0 * math.pi) * pos / float(seq)
        band_id = jnp.where(lane <= HY_BANDS, lane - 1, lane - 1 - HY_BANDS).astype(F32)
        band = 1e-4 + band_id * ((HY_BANDS - 1 - 1e-4) / (HY_BANDS - 1))
        arg = band * w
        z = jnp.where(lane == 0, t,
                      jnp.where(lane <= HY_BANDS, jnp.cos(arg),
                                jnp.where(lane <= 2 * HY_BANDS, -jnp.sin(arg), 0.0)))
        fr = fr_ref[...]
        hdn = jnp.sin(fr[0:1] * (_dot3(z, w1_ref[...]) + b1_ref[...]))
        hdn = jnp.sin(fr[1:2] * (_dot3(hdn, w2_ref[...]) + b2_ref[...]))
        hdn = jnp.sin(fr[2:3] * (_dot3(hdn, w3_ref[...]) + b3_ref[...]))
        hf = _dot3(hdn, wo_ref[...])
        decay = jnp.exp(-t[:, 0:1] * jnp.abs(dl_ref[...]))
        first = lax.broadcasted_iota(jnp.int32, (seq, HY_WIDTH), 0) == 0
        h_ref[:, :HY_WIDTH] = (hf[:, :HY_WIDTH] * decay).astype(BF16)
        h_ref[:, HY_WIDTH:] = jnp.where(first, 0.0, hf[:, HY_WIDTH:] * decay).astype(BF16)

    gc = _dot(wc_ref[...], h_ref[...])
    gs = _dot(ws_ref[...], h_ref[...])
    first = (lax.broadcasted_iota(jnp.int32, (tf, HY_WIDTH), 0) + i * tf) == 0
    scale = jnp.where(first, 1.0 / nfft, 2.0 / nfft)
    kc_ref[...] = (gc[:, :HY_WIDTH] + gc[:, HY_WIDTH:]) * scale
    ks_ref[...] = jnp.where(first, gs[:, :HY_WIDTH] + gs[:, HY_WIDTH:], gs[:, :HY_WIDTH] - gs[:, HY_WIDTH:]) * scale


def _hy_filter(seq, dft, mlp, deltas, l):
    w1p, b1p, w2p, b2p, w3p, b3p, wop, frp = mlp
    tf = min(seq, 512)
    nf = seq // tf

    def full(a):
        shp = a.shape[1:]
        return pl.BlockSpec((None,) + shp, lambda i: (l,) + (0,) * len(shp))

    return pl.pallas_call(
        functools.partial(_hy_filter_kernel, seq),
        grid=(nf,),
        in_specs=[pl.BlockSpec((tf, seq), lambda i: (i, 0)),
                  pl.BlockSpec((tf, seq), lambda i: (i + nf, 0)),
                  full(w1p), full(b1p), full(w2p), full(b2p), full(w3p), full(b3p), full(wop), full(frp),
                  pl.BlockSpec((1, HY_WIDTH), lambda i: (0, 0))],
        out_specs=[pl.BlockSpec((tf, HY_WIDTH), lambda i: (i, 0)),
                   pl.BlockSpec((tf, HY_WIDTH), lambda i: (i, 0))],
        out_shape=[jax.ShapeDtypeStruct((seq, HY_WIDTH), F32), jax.ShapeDtypeStruct((seq, HY_WIDTH), F32)],
        scratch_shapes=[pltpu.VMEM((seq, 2 * HY_WIDTH), BF16)],
        compiler_params=_params("arbitrary"),
        name="hyena_filter",
    )(dft, dft, w1p, b1p, w2p, b2p, w3p, b3p, wop, frp, deltas)


def _hy_pre_kernel(uv_ref, u1_ref, u0_ref, cw_ref, cb_ref, z_ref, x0_ref):
    seq = uv_ref.shape[0]
    rowi = lax.broadcasted_iota(jnp.int32, (seq, 128), 0)
    j = pl.program_id(1)

    def conv(u_ref, part):
        u = u_ref[...].astype(F32)
        prev = jnp.where(rowi == 0, 0.0, pltpu.roll(u, 1, 0))
        nxt = jnp.where(rowi == seq - 1, 0.0, pltpu.roll(u, seq - 1, 0))
        cs = pl.ds(pl.multiple_of(part * HY_WIDTH + j * 128, 128), 128)
        return prev * cw_ref[0:1, cs] + u * cw_ref[1:2, cs] + nxt * cw_ref[2:3, cs] + cb_ref[:, cs]

    z_ref[...] = (conv(u1_ref, 1) * conv(uv_ref, 0)).astype(BF16)
    x0_ref[...] = conv(u0_ref, 2).astype(BF16)


def _hy_pre(h, conv_w, conv_b3, l):
    bsz, seq, _ = h.shape
    lanes = 128
    nj = HY_WIDTH // lanes
    off = FN_WIDTH // lanes

    def part(p):
        return pl.BlockSpec((None, seq, lanes), lambda b, j: (b, 0, off + p * nj + j))

    o_spec = pl.BlockSpec((None, seq, lanes), lambda b, j: (b, 0, j))
    return pl.pallas_call(
        _hy_pre_kernel,
        grid=(bsz, nj),
        in_specs=[part(0), part(1), part(2),
                  pl.BlockSpec((None, 3, 3 * HY_WIDTH), lambda b, j: (l, 0, 0)),
                  pl.BlockSpec((None, 1, 3 * HY_WIDTH), lambda b, j: (l, 0, 0))],
        out_specs=[o_spec, o_spec],
        out_shape=[jax.ShapeDtypeStruct((bsz, seq, HY_WIDTH), BF16)] * 2,
        compiler_params=_params("arbitrary", "arbitrary"),
        name="hyena_pre",
    )(h, h, h, conv_w, conv_b3)


def _hy_fwd_kernel(wc_ref, ws_ref, z_ref, kc_ref, ks_ref, yc_ref, ys_ref):
    tf = wc_ref.shape[0]
    z = z_ref[...]
    uc = _dot(wc_ref[...], z)
    us = _dot(ws_ref[...], z)
    kc = kc_ref[...]
    ks = ks_ref[...]
    first = (lax.broadcasted_iota(jnp.int32, (tf, HY_WIDTH), 0) + pl.program_id(0) * tf) == 0
    ss = us * ks
    yc_ref[...] = (uc * kc - jnp.where(first, 0.0, ss)).astype(BF16)
    ys_ref[...] = jnp.where(first, ss, uc * ks + us * kc).astype(BF16)


def _hy_fwd(z, dft, kc, ks):
    bsz, seq, _ = z.shape
    tf = min(seq, 1024)
    nf = seq // tf
    k_spec = pl.BlockSpec((tf, HY_WIDTH), lambda i, b: (i, 0))
    y_spec = pl.BlockSpec((None, tf, HY_WIDTH), lambda i, b: (b, i, 0))
    return pl.pallas_call(
        _hy_fwd_kernel,
        grid=(nf, bsz),
        in_specs=[pl.BlockSpec((tf, seq), lambda i, b: (i, 0)),
                  pl.BlockSpec((tf, seq), lambda i, b: (i + nf, 0)),
                  pl.BlockSpec((None, seq, HY_WIDTH), lambda i, b: (b, 0, 0)),
                  k_spec, k_spec],
        out_specs=[y_spec, y_spec],
        out_shape=[jax.ShapeDtypeStruct((bsz, seq, HY_WIDTH), BF16)] * 2,
        compiler_params=_params("arbitrary", "arbitrary"),
        name="hyena_dft",
    )(dft, dft, z, kc, ks)


def _hy_inv_kernel(tc_ref, ts_ref, yc_ref, ys_ref, z_ref, x0_ref, db_ref, o_ref):
    y = _dot(tc_ref[...], yc_ref[...]) + _dot(ts_ref[...], ys_ref[...])
    z = z_ref[...].astype(F32)
    o_ref[...] = (x0_ref[...].astype(F32) * (y + z * db_ref[...])).astype(BF16)


def _hy_inv(yc, ys, dft_t, z, x0, hy_bias3, l):
    bsz, seq, _ = z.shape
    tt = min(seq, 1024)
    y_spec = pl.BlockSpec((None, seq, HY_WIDTH), lambda t, b: (b, 0, 0))
    r_spec = pl.BlockSpec((None, tt, HY_WIDTH), lambda t, b: (b, t, 0))
    return pl.pallas_call(
        _hy_inv_kernel,
        grid=(seq // tt, bsz),
        in_specs=[pl.BlockSpec((tt, seq), lambda t, b: (t, 0)),
                  pl.BlockSpec((tt, seq), lambda t, b: (t, 1)),
                  y_spec, y_spec, r_spec, r_spec,
                  pl.BlockSpec((None, 1, HY_WIDTH), lambda t, b: (l, 0, 0))],
        out_specs=r_spec,
        out_shape=jax.ShapeDtypeStruct((bsz, seq, HY_WIDTH), BF16),
        compiler_params=_params("arbitrary", "arbitrary"),
        name="hyena_idft",
    )(dft_t, dft_t, yc, ys, z, x0, hy_bias3)


def _out_route_kernel(yf_ref, yh_ref, yg_ref, wf_ref, wh_ref, wg_ref, x_ref, g1_ref, gam_ref, sh_ref, sc_ref,
                      wr_ref, xo_ref, xm_ref, aff_ref):
    mix = _dot(yf_ref[...], wf_ref[...]) + _dot(yh_ref[...], wh_ref[...]) + _dot(yg_ref[...], wg_ref[...])
    x = x_ref[...] + g1_ref[...] * mix
    xo_ref[...] = x
    xm = _norm_mod(x, gam_ref[...], sh_ref[...], sc_ref[...])
    xm_ref[...] = xm.astype(BF16)
    logits = _dot3(wr_ref[...], xm, dot=_dot_nt)
    mx = jnp.max(logits, axis=0, keepdims=True)
    ex = jnp.exp(logits - mx)
    aff_ref[...] = ex / jnp.sum(ex, axis=0, keepdims=True)


def _out_route(x, y_fn, y_hy, y_hg, w_out_bf, mod4, mrow, gamma3, w_router_t, l):
    bsz, seq, d = x.shape
    tm = min(seq, 512)
    half = FN_WIDTH

    def mod_row(k):
        return pl.BlockSpec((None, None, 1, d), lambda b, i: (mrow(b), k, 0, 0))

    return pl.pallas_call(
        _out_route_kernel,
        grid=(bsz, seq // tm),
        in_specs=[pl.BlockSpec((None, tm, half), lambda b, i: (b, i, 0)),
                  pl.BlockSpec((None, tm, half), lambda b, i: (b, i, 0)),
                  pl.BlockSpec((None, tm, HG_WIDTH), lambda b, i: (b, i, 0)),
                  pl.BlockSpec((None, half, d), lambda b, i: (l, 0, 0)),
                  pl.BlockSpec((None, half, d), lambda b, i: (l, 1, 0)),
                  pl.BlockSpec((None, HG_WIDTH, d), lambda b, i: (l, 1, 0)),
                  pl.BlockSpec((None, tm, d), lambda b, i: (b, i, 0)),
                  mod_row(2),
                  pl.BlockSpec((None, 1, d), lambda b, i: (l, 0, 0)),
                  mod_row(3), mod_row(4),
                  pl.BlockSpec((None, N_EXPERTS, d), lambda b, i: (l, 0, 0))],
        out_specs=[pl.BlockSpec((None, tm, d), lambda b, i: (b, i, 0)),
                   pl.BlockSpec((None, tm, d), lambda b, i: (b, i, 0)),
                   pl.BlockSpec((None, N_EXPERTS, tm), lambda b, i: (b, 0, i))],
        out_shape=[jax.ShapeDtypeStruct((bsz, seq, d), F32),
                   jax.ShapeDtypeStruct((bsz, seq, d), BF16),
                   jax.ShapeDtypeStruct((bsz, N_EXPERTS, seq), F32)],
        compiler_params=_params("arbitrary", "arbitrary"),
        name="out_route",
    )(y_fn, y_hy, y_hg, w_out_bf, w_out_bf, w_out_bf, x, mod4, gamma3, mod4, mod4, w_router_t)


def _topk_kernel(cap, aff_ref, tri_ref, pos_ref):
    a = aff_ref[...]

    def count(mask):
        return jnp.sum(jnp.where(mask, 1.0, 0.0), axis=1, keepdims=True)

    def as_float(bits):
        return pltpu.bitcast(jnp.broadcast_to(bits, a.shape), F32)

    def step(i, thr_bits):
        cand = thr_bits | jnp.left_shift(jnp.int32(1), 30 - i)
        return jnp.where(count(a >= as_float(cand)) >= cap, cand, thr_bits)

    thr = as_float(lax.fori_loop(0, 31, step, jnp.zeros((a.shape[0], 1), jnp.int32)))
    above = a > thr
    tie = a == thr
    room = cap - count(above)
    tie_rank = _dot(jnp.where(tie, 1.0, 0.0).astype(BF16), tri_ref[...])
    sel = jnp.where(above, 1.0, jnp.where(tie, jnp.where(tie_rank <= room, 1.0, 0.0), 0.0))
    slot = _dot(sel.astype(BF16), tri_ref[...]) - 1.0
    pos_ref[...] = jnp.where(sel > 0.5, slot, -1.0).astype(jnp.int32)


def _topk(aff, tri_incl, cap):
    bsz, ne, seq = aff.shape
    rows = bsz * ne
    pos = pl.pallas_call(
        functools.partial(_topk_kernel, cap),
        grid=(1,),
        in_specs=[pl.BlockSpec((rows, seq), lambda i: (0, 0)),
                  pl.BlockSpec((seq, seq), lambda i: (0, 0))],
        out_specs=pl.BlockSpec((rows, seq), lambda i: (0, 0)),
        out_shape=jax.ShapeDtypeStruct((rows, seq), jnp.int32),
        compiler_params=_params("arbitrary"),
        name="moe_topk",
    )(aff.reshape(rows, seq), tri_incl)
    return pos.reshape(bsz, ne, seq)


def _gather_kernel(cap, xm_ref, pos_ref, o_ref):
    seq = xm_ref.shape[0]
    pos = pos_ref[pl.ds(pl.program_id(1), 1), :]
    slot = lax.broadcasted_iota(jnp.int32, (cap, seq), 0)
    onehot = jnp.where(slot == pos, 1.0, 0.0).astype(BF16)
    o_ref[...] = _dot(onehot, xm_ref[...]).astype(BF16)


def _gather(xm, pos, cap):
    bsz, seq, d = xm.shape
    return pl.pallas_call(
        functools.partial(_gather_kernel, cap),
        grid=(bsz, N_EXPERTS),
        in_specs=[pl.BlockSpec((None, seq, d), lambda b, e: (b, 0, 0)),
                  pl.BlockSpec((None, N_EXPERTS, seq), lambda b, e: (b, 0, 0))],
        out_specs=pl.BlockSpec((None, None, cap, d), lambda b, e: (e, b, 0, 0)),
        out_shape=jax.ShapeDtypeStruct((N_EXPERTS, bsz, cap, d), BF16),
        compiler_params=_params("arbitrary", "arbitrary"),
        name="moe_gather",
    )(xm, pos)


def _ffn_kernel(xs_ref, wg_ref, wu_ref, wd_ref, o_ref, acc_ref):
    j = pl.program_id(2)

    @pl.when((pl.program_id(0) == 0) & (pl.program_id(1) == 0) & (j == 0))
    def _():
        acc_ref[...] = jnp.zeros_like(acc_ref)

    xs = xs_ref[...]
    hid = _silu(_dot(xs, wg_ref[...].astype(BF16))) * _dot(xs, wu_ref[...].astype(BF16))
    part = _dot(hid.astype(BF16), wd_ref[...].astype(BF16))
    total = part + jnp.where(j > 0, acc_ref[...], 0.0)
    acc_ref[...] = total
    o_ref[...] = total.astype(BF16)


def _ffn(xs, w_gate, w_up, w_down, l):
    ne, rows, d = xs.shape
    ff = w_gate.shape[-1]
    tm = min(rows, 1024)
    tj = 512
    return pl.pallas_call(
        _ffn_kernel,
        grid=(ne, rows // tm, ff // tj),
        in_specs=[pl.BlockSpec((None, tm, d), lambda e, m, j: (e, m, 0)),
                  pl.BlockSpec((None, None, d, tj), lambda e, m, j: (l, e, 0, j)),
                  pl.BlockSpec((None, None, d, tj), lambda e, m, j: (l, e, 0, j)),
                  pl.BlockSpec((None, None, tj, d), lambda e, m, j: (l, e, j, 0))],
        out_specs=pl.BlockSpec((None, tm, d), lambda e, m, j: (e, m, 0)),
        out_shape=jax.ShapeDtypeStruct((ne, rows, d), BF16),
        scratch_shapes=[pltpu.VMEM((tm, d), F32)],
        compiler_params=_params("arbitrary", "arbitrary", "arbitrary"),
        name="moe_ffn",
    )(xs, w_gate, w_up, w_down)


def _combine_kernel(cap, final_norm, *refs):
    if final_norm:
        ys_ref, pos_ref, aff_ref, x_ref, g_ref, fg_ref, o_ref, pos_t_ref, gate_t_ref, w_ref = refs
    else:
        ys_ref, pos_ref, aff_ref, x_ref, g_ref, o_ref, pos_t_ref, gate_t_ref, w_ref = refs
    i = pl.program_id(1)
    tm = x_ref.shape[0]
    ne = pos_ref.shape[0]

    @pl.when(i == 0)
    def _():
        pos_t_ref[...] = pos_ref[...].astype(F32).T
        gate_t_ref[...] = aff_ref[...].T

    r0 = pl.multiple_of(i * tm, tm)
    pos = pos_t_ref[pl.ds(r0, tm), :]
    gate = gate_t_ref[pl.ds(r0, tm), :]
    if cap % 128 == 0:
        lane = lax.broadcasted_iota(jnp.int32, (tm, cap), 1).astype(F32)
        for e in range(ne):
            w_ref[:, e * cap:(e + 1) * cap] = jnp.where(lane == pos[:, e:e + 1], gate[:, e:e + 1], 0.0).astype(BF16)
    else:
        lane = lax.broadcasted_iota(jnp.int32, (tm, ne * cap), 1).astype(F32)
        w = jnp.zeros((tm, ne * cap), F32)
        for e in range(ne):
            hit = jnp.logical_and(lane == pos[:, e:e + 1] + float(e * cap), pos[:, e:e + 1] >= 0.0)
            w = w + jnp.where(hit, gate[:, e:e + 1], 0.0)
        w_ref[...] = w.astype(BF16)
    ys = ys_ref[...].reshape(ne * cap, ys_ref.shape[-1])
    x = x_ref[...] + g_ref[...] * _dot(w_ref[...], ys)
    if final_norm:
        ms = jnp.mean(x * x, axis=-1, keepdims=True)
        x = x * lax.rsqrt(ms + EPS) * fg_ref[...]
    o_ref[...] = x


def _combine(ys, pos, aff, x, mod4, mrow, cap, final_g=None):
    bsz, seq, d = x.shape
    ne = pos.shape[1]
    tm = min(seq, 256)
    final_norm = final_g is not None
    in_specs = [pl.BlockSpec((ne, None, cap, d), lambda b, i: (0, b, 0, 0)),
                pl.BlockSpec((None, ne, seq), lambda b, i: (b, 0, 0)),
                pl.BlockSpec((None, ne, seq), lambda b, i: (b, 0, 0)),
                pl.BlockSpec((None, tm, d), lambda b, i: (b, i, 0)),
                pl.BlockSpec((None, None, 1, d), lambda b, i: (mrow(b), 5, 0, 0))]
    args = (ys, pos, aff, x, mod4)
    if final_norm:
        in_specs.append(pl.BlockSpec((1, d), lambda b, i: (0, 0)))
        args += (final_g,)
    return pl.pallas_call(
        functools.partial(_combine_kernel, cap, final_norm),
        grid=(bsz, seq // tm),
        in_specs=in_specs,
        out_specs=pl.BlockSpec((None, tm, d), lambda b, i: (b, i, 0)),
        out_shape=jax.ShapeDtypeStruct((bsz, seq, d), F32),
        scratch_shapes=[pltpu.VMEM((seq, ne), F32), pltpu.VMEM((seq, ne), F32), pltpu.VMEM((tm, ne * cap), BF16)],
        compiler_params=_params("arbitrary", "arbitrary"),
        name="moe_combine",
    )(*args)


def _angles(row_ids, cols, n):
    c = lax.broadcasted_iota(jnp.int32, (row_ids.shape[0], cols), 1)
    return ((row_ids * c) % n).astype(F32) * (2.0 * math.pi / n)


def _cos_sin(rows, cols, n):
    step = min(rows, 32)
    hi = _angles(jnp.arange(0, rows, step, dtype=jnp.int32)[:, None], cols, n)[:, None, :]
    lo = _angles(jnp.arange(step, dtype=jnp.int32)[:, None], cols, n)[None, :, :]
    cos = jnp.cos(hi) * jnp.cos(lo) - jnp.sin(hi) * jnp.sin(lo)
    sin = jnp.sin(hi) * jnp.cos(lo) + jnp.cos(hi) * jnp.sin(lo)
    return cos.reshape(rows, cols), sin.reshape(rows, cols)


def _fourier_table(seq):
    half = seq // 2
    cos, sin = _cos_sin(half, seq, seq)
    r = lax.broadcasted_iota(jnp.int32, (half, half), 0)
    c = lax.broadcasted_iota(jnp.int32, (half, half), 1)
    flip = ((r >= 1) & (c == half - r)).astype(BF16)
    t = lax.broadcasted_iota(jnp.int32, (8, seq), 1)
    alt = (1 - 2 * (t % 2)).astype(BF16)
    return cos.astype(BF16), sin.astype(BF16), flip, alt


def _channel_dft(seq):
    ang = _angles(jnp.arange(FN_GROUP, dtype=jnp.int32)[:, None], FN_GROUP, FN_GROUP)
    scale = 1.0 / math.sqrt(seq * FN_GROUP)
    eye = jnp.eye(FN_WIDTH // FN_GROUP, dtype=F32)
    return jnp.stack([jnp.kron(eye, jnp.cos(ang) * scale), jnp.kron(eye, jnp.sin(ang) * scale)])


def _hyena_dft(seq):
    cos, sin = _cos_sin(seq, seq, 2 * seq)
    r = lax.broadcasted_iota(jnp.int32, (seq, seq), 0)
    c = lax.broadcasted_iota(jnp.int32, (seq, seq), 1)
    dft = jnp.concatenate([cos, jnp.where(r == 0, (1 - 2 * (c % 2)).astype(F32), sin)], axis=0).astype(BF16)
    dft_t = jnp.concatenate([cos, jnp.where(c == 0, (1 - 2 * (r % 2)).astype(F32), sin)], axis=1).astype(BF16)
    return dft, dft_t


def _tri_incl(seq):
    r = lax.broadcasted_iota(jnp.int32, (seq, seq), 0)
    c = lax.broadcasted_iota(jnp.int32, (seq, seq), 1)
    return (r <= c).astype(BF16)


def _pad_to(a, shape):
    return jnp.pad(a, [(0, t - s) for s, t in zip(a.shape, shape)])


def kernel(x, c, ctx, c_ctx, norm_mix_g, norm_ffn_g, final_norm_g, w_mod, b_mod, w_in, w_out, w_fnet,
           hy_conv_w, hy_conv_b, hy_w1, hy_b1, hy_w2, hy_b2, hy_w3, hy_b3, hy_w_out, hy_freq, hy_bias,
           hg_lb, hg_norm_g, w_router, w_gate, w_up, w_down):
    bsz, seq, d = x.shape
    ctx_len = ctx.shape[1]
    depth = w_in.shape[0]

    p = jax.nn.softmax(hg_lb.astype(F32), axis=0)
    lbs4 = (jnp.cumsum(p, axis=0) - p[0:1]).reshape(depth, 2, 1, HG_WIDTH)
    w_in_bf = w_in.astype(BF16)
    w_out_bf = w_out.astype(BF16)
    w_router_t = jnp.swapaxes(w_router, 1, 2)
    g_mix3 = norm_mix_g.reshape(depth, 1, d)
    g_ffn3 = norm_ffn_g.reshape(depth, 1, d)
    gain3 = hg_norm_g.reshape(depth, 1, HG_WIDTH)
    b_mod3 = b_mod.reshape(depth, 1, 6 * d)
    conv_b3 = hy_conv_b.reshape(depth, 1, 3 * HY_WIDTH)
    hy_bias3 = hy_bias.reshape(depth, 1, HY_WIDTH)
    mlp = (_pad_to(hy_w1, (depth, HY_PAD, HY_PAD)), _pad_to(hy_b1.reshape(depth, 1, -1), (depth, 1, HY_PAD)),
           _pad_to(hy_w2, (depth, HY_PAD, HY_PAD)), _pad_to(hy_b2.reshape(depth, 1, -1), (depth, 1, HY_PAD)),
           _pad_to(hy_w3, (depth, HY_PAD, HY_PAD)), _pad_to(hy_b3.reshape(depth, 1, -1), (depth, 1, HY_PAD)),
           _pad_to(hy_w_out, (depth, HY_PAD, 2 * HY_WIDTH)), _pad_to(hy_freq, (depth, 8, HY_PAD)))
    max_decay = math.log(1e-2) / 0.3
    min_decay = math.log(1e-2) / 1.5
    deltas = jnp.linspace(min_decay, max_decay, HY_WIDTH, dtype=F32).reshape(1, HY_WIDTH)
    rows = 16
    cc = jnp.zeros((rows, d), F32).at[:bsz].set(c).at[bsz].set(c_ctx)

    tables = {}
    for n in {seq, ctx_len}:
        dft, dft_t = _hyena_dft(n)
        tables[n] = dict(fourier=_fourier_table(n), chan=_channel_dft(n), dft=dft, dft_t=dft_t, tri=_tri_incl(n))

    x_row = lambda b: b
    ctx_row = lambda b: bsz
    zero_state = jnp.zeros((bsz, HG_HEADS, HG_HEAD, HG_HEAD), F32)

    def mixers(hh, y_hg, n, l):
        t = tables[n]
        y_fn = _fourier(hh, _fn_prep(t["chan"], w_fnet, l), t["fourier"])
        kc, ks = _hy_filter(n, t["dft"], mlp, deltas, l)
        z, x0 = _hy_pre(hh, hy_conv_w, conv_b3, l)
        yc, ys = _hy_fwd(z, t["dft"], kc, ks)
        y_hy = _hy_inv(yc, ys, t["dft_t"], z, x0, hy_bias3, l)
        return y_fn, y_hy, y_hg

    def sublayers(xx, hh, y_hg, mod4, mrow, n, l, final_g=None):
        cap = EC_CAPACITY * n // N_EXPERTS
        xx, xm, aff = _out_route(xx, *mixers(hh, y_hg, n, l), w_out_bf, mod4, mrow, g_ffn3, w_router_t, l)
        pos = _topk(aff, tables[n]["tri"], cap)
        xs = _gather(xm, pos, cap)
        ys = _ffn(xs.reshape(N_EXPERTS, bsz * cap, d), w_gate, w_up, w_down, l)
        return _combine(ys.reshape(N_EXPERTS, bsz, cap, d), pos, aff, xx, mod4, mrow, cap, final_g)

    xc = ctx
    for l in range(depth):
        last = l == depth - 1
        mod4 = _modulation(cc, w_mod, b_mod3, l).reshape(rows, 6, 1, d)
        h = _in_proj(x, mod4, x_row, g_mix3, w_in_bf, l)
        hc = _in_proj(xc.reshape(1, bsz * ctx_len, d), mod4, ctx_row, g_mix3, w_in_bf, l).reshape(bsz, ctx_len, -1)
        o_cf, s_f = _hgrn(hc, lbs4, gain3, zero_state, l, False)
        y_hg_c, s_b = _hgrn(hc, lbs4, gain3, zero_state, l, True, o_fwd=o_cf)
        o_xf, _ = _hgrn(h, lbs4, gain3, s_f, l, False)
        y_hg_x, _ = _hgrn(h, lbs4, gain3, s_b, l, True, o_fwd=o_xf)
        x = sublayers(x, h, y_hg_x, mod4, x_row, seq, l, final_norm_g.reshape(1, d) if last else None)
        if not last:
            xc = sublayers(xc, hc, y_hg_c, mod4, ctx_row, ctx_len, l)
    return x
```
